```python
import math
import jax, jax.numpy as jnp
from jax import lax
import numpy as np

D_MODEL = 1024
BATCH = 8
SEQ = 2048
DEPTH = 4
DEC_BATCH = 128
DEC_SEQ = 1
PAST_LEN = 16384
PAGE_SIZE = 128

N_BRANCH = 4
BRANCH_W = D_MODEL // N_BRANCH
CONV_W = 4
LRU_BLOCKS = 4
LRU_BLK = BRANCH_W // LRU_BLOCKS
LRU_C = 8.0
GDN_HEADS = 4
GDN_DK = BRANCH_W // GDN_HEADS
GDN_DV = BRANCH_W // GDN_HEADS
GDN_CHUNK = 64
S5_GROUP = 16
S5_GROUPS = BRANCH_W // S5_GROUP
S5_STATE = 64
ML_HEADS = 4
ML_DH = BRANCH_W // ML_HEADS
ML_CHUNK = 64
EPS = 1e-6

SEGMENTS = (('lru_x', BRANCH_W), ('lru_z', BRANCH_W),
            ('gdn_q', BRANCH_W), ('gdn_k', BRANCH_W), ('gdn_v', BRANCH_W), ('gdn_z', BRANCH_W),
            ('gdn_a', GDN_HEADS), ('gdn_b', GDN_HEADS),
            ('s5_u', BRANCH_W), ('s5_z', BRANCH_W),
            ('ml_q', BRANCH_W), ('ml_k', BRANCH_W), ('ml_v', BRANCH_W), ('ml_o', BRANCH_W), ('ml_z', BRANCH_W),
            ('ml_i', ML_HEADS), ('ml_f', ML_HEADS),
            ('merge', N_BRANCH * D_MODEL))
IN_WIDTH = 13 * BRANCH_W + 2 * GDN_HEADS + 2 * ML_HEADS + N_BRANCH * D_MODEL
STATE_NAMES = ('lru_h', 'lru_conv', 'gdn_S', 'gdn_conv', 's5_re', 's5_im', 'ml_C', 'ml_n', 'ml_m', 'ml_conv')

kernel_name = 'hybrid_rglru_gdn_s5_mlstm_step'


def state_shapes(batch):
    return {'lru_h': (DEPTH, batch, BRANCH_W),
            'lru_conv': (DEPTH, batch, CONV_W - 1, BRANCH_W),
            'gdn_S': (DEPTH, batch, GDN_HEADS, GDN_DK, GDN_DV),
            'gdn_conv': (DEPTH, batch, CONV_W - 1, 3 * BRANCH_W),
            's5_re': (DEPTH, batch, S5_GROUPS, S5_STATE),
            's5_im': (DEPTH, batch, S5_GROUPS, S5_STATE),
            'ml_C': (DEPTH, batch, ML_HEADS, ML_DH, ML_DH),
            'ml_n': (DEPTH, batch, ML_HEADS, ML_DH),
            'ml_m': (DEPTH, batch, ML_HEADS),
            'ml_conv': (DEPTH, batch, CONV_W - 1, 2 * BRANCH_W)}


def rmsnorm(x, w):
    xf = x.astype(jnp.float32)
    y = xf * lax.rsqrt(jnp.mean(xf * xf, axis=-1, keepdims=True) + EPS)
    return (y * w.astype(jnp.float32)).astype(x.dtype)


def l2norm(x):
    return x * lax.rsqrt(jnp.sum(x * x, axis=-1, keepdims=True) + EPS)


def causal_conv(x, buf, w):
    T = x.shape[1]
    xp = jnp.concatenate([buf.astype(x.dtype), x], axis=1)
    y = sum(xp[:, j:j + T] * w[j] for j in range(CONV_W))
    return y, xp[:, T:]


def linear_scan(a, b, h0):
    b = b.at[:, 0].add(a[:, 0] * h0)
    def comb(l, r):
        return (l[0] * r[0], r[0] * l[1] + r[1])
    _, h = lax.associative_scan(comb, (a, b), axis=1)
    return h


def split_cols(proj):
    out = {}
    off = 0
    for name, width in SEGMENTS:
        out[name] = proj[..., off:off + width]
        off += width
    return out


def to_chunks(t, C, value=0.0):
    B, T = t.shape[0], t.shape[1]
    pad = (-T) % C
    t = jnp.pad(t, [(0, 0), (0, pad)] + [(0, 0)] * (t.ndim - 2), constant_values=value)
    t = t.reshape((B, (T + pad) // C, C) + t.shape[2:])
    t = jnp.moveaxis(t, 1, 0)
    return jnp.moveaxis(t, 3, 2)


def from_chunks(o, T):
    N, B, H, C, D = o.shape
    return jnp.transpose(o, (1, 0, 3, 2, 4)).reshape(B, N * C, H, D)[:, :T]


def rglru(xs, h0, w_a, b_a, w_x, b_x, lam):
    B, T, _ = xs.shape
    xb = xs.reshape(B, T, LRU_BLOCKS, LRU_BLK)
    r = jax.nn.sigmoid(jnp.einsum('btnc,ncd->btnd', xb, w_a).reshape(B, T, BRANCH_W) + b_a)
    i = jax.nn.sigmoid(jnp.einsum('btnc,ncd->btnd', xb, w_x).reshape(B, T, BRANCH_W) + b_x)
    log_a = -LRU_C * r * jax.nn.softplus(-lam)
    a = jnp.exp(log_a)
    b = jnp.sqrt(-jnp.expm1(2.0 * log_a)) * (i * xs)
    h = linear_scan(a, b, h0)
    return h, h[:, -1]


def gdn_chunked(q, k, v, g, beta, S0):
    T = q.shape[1]
    DV = v.shape[-1]
    C = min(GDN_CHUNK, T)
    q, k, v, g, beta = (to_chunks(t, C) for t in (q, k, v, g, beta))
    gc = jnp.cumsum(g, axis=-1)
    tri = jnp.tril(jnp.ones((C, C), bool))
    strict = jnp.tril(jnp.ones((C, C), bool), -1)
    diff = gc[..., :, None] - gc[..., None, :]
    decay = jnp.where(tri, jnp.exp(jnp.where(tri, diff, 0.0)), 0.0)
    kb = k * beta[..., None]
    M = jnp.where(strict, jnp.einsum('nbhid,nbhjd->nbhij', kb, k) * decay, 0.0)
    A = M + jnp.eye(C, dtype=M.dtype)
    rhs = jnp.concatenate([v * beta[..., None], kb * jnp.exp(gc)[..., None]], axis=-1)
    X = lax.linalg.triangular_solve(A, rhs, left_side=True, lower=True, unit_diagonal=True)
    u, w = X[..., :DV], X[..., DV:]
    attn = jnp.einsum('nbhid,nbhjd->nbhij', q, k) * decay
    q_dec = q * jnp.exp(gc)[..., None]
    g_last = gc[..., -1]
    k_dec = k * jnp.exp(g_last[..., None] - gc)[..., None]

    def step(S, inp):
        u_c, w_c, attn_c, qd_c, kd_c, gl_c = inp
        v_new = u_c - jnp.einsum('bhck,bhkv->bhcv', w_c, S)
        o = jnp.einsum('bhck,bhkv->bhcv', qd_c, S) + jnp.einsum('bhij,bhjv->bhiv', attn_c, v_new)
        S = S * jnp.exp(gl_c)[..., None, None] + jnp.einsum('bhck,bhcv->bhkv', kd_c, v_new)
        return S, o

    S, o = lax.scan(step, S0, (u, w, attn, q_dec, k_dec, g_last))
    return from_chunks(o, T), S


def mlstm_chunked(q, k, v, i_pre, logf, C0, n0, m0):
    T = q.shape[1]
    L = min(ML_CHUNK, T)
    q, k, v, logf = (to_chunks(t, L) for t in (q, k, v, logf))
    i_pre = to_chunks(i_pre, L, -jnp.inf)
    tri = jnp.tril(jnp.ones((L, L), bool))
    b = jnp.cumsum(logf, axis=-1)
    dlog = jnp.where(tri, b[..., :, None] - b[..., None, :] + i_pre[..., None, :], -jnp.inf)
    dmax = jnp.max(dlog, axis=-1)
    qk = jnp.einsum('nbhld,nbhsd->nbhls', q, k)
    b_last = b[..., -1]
    e_log = b_last[..., None] - b + i_pre

    def step(carry, inp):
        Cm, nm, mm = carry
        q_c, v_c, k_c, b_c, dlog_c, dmax_c, qk_c, bl_c, e_c = inp
        inter = b_c + mm[..., None]
        m_t = jnp.maximum(inter, dmax_c)
        s = qk_c * jnp.exp(dlog_c - m_t[..., None])
        w_int = jnp.exp(inter - m_t)
        num = w_int[..., None] * jnp.einsum('bhld,bhde->bhle', q_c, Cm) + jnp.einsum('bhls,bhse->bhle', s, v_c)
        den = w_int * jnp.einsum('bhld,bhd->bhl', q_c, nm) + jnp.sum(s, axis=-1)
        h = num / jnp.maximum(jnp.abs(den), jnp.exp(-m_t))[..., None]
        m_new = jnp.maximum(bl_c + mm, jnp.max(e_c, axis=-1))
        dec = jnp.exp(bl_c + mm - m_new)
        wk = jnp.exp(e_c - m_new[..., None])[..., None] * k_c
        C_new = dec[..., None, None] * Cm + jnp.einsum('bhld,bhle->bhde', wk, v_c)
        n_new = dec[..., None] * nm + jnp.sum(wk, axis=-2)
        return (C_new, n_new, m_new), h

    (Cf, nf, mf), h = lax.scan(step, (C0, n0, m0), (q, v, k, b, dlog, dmax, qk, b_last, e_log))
    return from_chunks(h, T), Cf, nf, mf


def s5_layer(u, h_re0, h_im0, lam_re, lam_im, log_dt, b_re, b_im, c_re, c_im, d, w_glu):
    B, T, _ = u.shape
    ug = u.reshape(B, T, S5_GROUPS, S5_GROUP)
    dt = jnp.exp(log_dt)[:, None]
    mag = jnp.exp(lam_re * dt)
    ang = lam_im * dt
    ar, ai = mag * jnp.cos(ang), mag * jnp.sin(ang)
    den = lam_re * lam_re + lam_im * lam_im
    cr = ((ar - 1.0) * lam_re + ai * lam_im) / den
    ci = (ai * lam_re - (ar - 1.0) * lam_im) / den
    bb_re = cr[..., None] * b_re - ci[..., None] * b_im
    bb_im = cr[..., None] * b_im + ci[..., None] * b_re
    xr = jnp.einsum('btgc,gnc->btgn', ug, bb_re)
    xi = jnp.einsum('btgc,gnc->btgn', ug, bb_im)
    xr = xr.at[:, 0].add(ar * h_re0 - ai * h_im0)
    xi = xi.at[:, 0].add(ar * h_im0 + ai * h_re0)
    a_r = jnp.broadcast_to(ar, xr.shape)
    a_i = jnp.broadcast_to(ai, xr.shape)

    def comb(l, r):
        lar, lai, lbr, lbi = l
        rar, rai, rbr, rbi = r
        return (rar * lar - rai * lai, rar * lai + rai * lar,
                rar * lbr - rai * lbi + rbr, rar * lbi + rai * lbr + rbi)

    _, _, hr, hi = lax.associative_scan(comb, (a_r, a_i, xr, xi), axis=1)
    y = (jnp.einsum('btgn,gcn->btgc', hr, c_re) - jnp.einsum('btgn,gcn->btgc', hi, c_im)
         + d.reshape(S5_GROUPS, S5_GROUP) * ug).reshape(B, T, BRANCH_W)
    y = jax.nn.gelu(y)
    y = y * jax.nn.sigmoid(y @ w_glu)
    return y, hr[:, -1], hi[:, -1]


def mixer_layer(x, st, p):
    B, T, _ = x.shape
    f32 = jnp.float32
    pr = split_cols(rmsnorm(x, p['norm_w']) @ p['w_in'])

    xa, lru_conv = causal_conv(pr['lru_x'], st['lru_conv'], p['lru_conv_w'])
    ha, lru_h = rglru(xa.astype(f32), st['lru_h'].astype(f32), p['lru_w_a'], p['lru_b_a'],
                      p['lru_w_x'], p['lru_b_x'], p['lru_lambda'])
    out_a = ha * jax.nn.silu(pr['lru_z'].astype(f32))

    qkv, gdn_conv = causal_conv(jnp.concatenate([pr['gdn_q'], pr['gdn_k'], pr['gdn_v']], axis=-1),
                                st['gdn_conv'], p['gdn_conv_w'])
    qkv = jax.nn.silu(qkv.astype(f32))
    gq = l2norm(qkv[..., :BRANCH_W].reshape(B, T, GDN_HEADS, GDN_DK)) * GDN_DK ** -0.5
    gk = l2norm(qkv[..., BRANCH_W:2 * BRANCH_W].reshape(B, T, GDN_HEADS, GDN_DK))
    gv = qkv[..., 2 * BRANCH_W:].reshape(B, T, GDN_HEADS, GDN_DV)
    g = -jnp.exp(p['gdn_A_log']) * jax.nn.softplus(pr['gdn_a'].astype(f32) + p['gdn_dt_bias'])
    beta = jax.nn.sigmoid(pr['gdn_b'].astype(f32))
    ob, gdn_S = gdn_chunked(gq, gk, gv, g, beta, st['gdn_S'].astype(f32))
    out_b = rmsnorm(ob, p['gdn_norm_w']).reshape(B, T, BRANCH_W) * jax.nn.silu(pr['gdn_z'].astype(f32))

    yc, s5_re, s5_im = s5_layer(pr['s5_u'].astype(f32), st['s5_re'].astype(f32), st['s5_im'].astype(f32),
                                p['s5_lam_re'], p['s5_lam_im'], p['s5_log_dt'], p['s5_b_re'], p['s5_b_im'],
                                p['s5_c_re'], p['s5_c_im'], p['s5_d'], p['s5_w_glu'])
    out_c = yc * jax.nn.silu(pr['s5_z'].astype(f32))

    mqk, ml_conv = causal_conv(jnp.concatenate([pr['ml_q'], pr['ml_k']], axis=-1), st['ml_conv'], p['ml_conv_w'])
    mqk = jax.nn.silu(mqk.astype(f32))
    mq = mqk[..., :BRANCH_W].reshape(B, T, ML_HEADS, ML_DH)
    mk = mqk[..., BRANCH_W:].reshape(B, T, ML_HEADS, ML_DH) * ML_DH ** -0.5
    mv = pr['ml_v'].astype(f32).reshape(B, T, ML_HEADS, ML_DH)
    i_pre = pr['ml_i'].astype(f32) + p['ml_b_i']
    logf = jax.nn.log_sigmoid(pr['ml_f'].astype(f32) + p['ml_b_f'])
    hd, ml_C, ml_n, ml_m = mlstm_chunked(mq, mk, mv, i_pre, logf, st['ml_C'].astype(f32),
                                         st['ml_n'].astype(f32), st['ml_m'].astype(f32))
    out_d = hd.reshape(B, T, BRANCH_W) * jax.nn.sigmoid(pr['ml_o'].astype(f32)) * jax.nn.silu(pr['ml_z'].astype(f32))

    br = jnp.stack([out_a, out_b, out_c, out_d], axis=2).astype(x.dtype)
    proj = jnp.einsum('btnw,nwd->btnd', br, p['w_branch'])
    gates = jax.nn.sigmoid(pr['merge'].reshape(B, T, N_BRANCH, D_MODEL))
    merged = jnp.sum(gates * proj, axis=2)
    y = x + merged @ p['w_out']
    new_st = {'lru_h': lru_h, 'lru_conv': lru_conv, 'gdn_S': gdn_S, 'gdn_conv': gdn_conv,
              's5_re': s5_re, 's5_im': s5_im, 'ml_C': ml_C, 'ml_n': ml_n, 'ml_m': ml_m, 'ml_conv': ml_conv}
    return y, new_st


def trunk(x, st, params, final_norm_w):
    new = {name: [] for name in STATE_NAMES}
    for l in range(DEPTH):
        p = {name: arr[l] for name, arr in params.items()}
        s = {name: st[name][l] for name in STATE_NAMES}
        x, ns = mixer_layer(x, s, p)
        for name in STATE_NAMES:
            new[name].append(ns[name])
    return rmsnorm(x, final_norm_w), {name: jnp.stack(v) for name, v in new.items()}


def setup_inputs(seed: int = 0) -> dict:
    key = jax.random.key(seed)
    k = jax.random.split(key, 40)
    f32 = jnp.float32
    W = BRANCH_W

    def nrm(i, shape, scale):
        return jax.random.normal(k[i], shape, f32) * scale

    def uni(i, shape, lo, hi):
        return jax.random.uniform(k[i], shape, f32, lo, hi)

    ss = state_shapes(DEC_BATCH)
    a8 = uni(19, (DEPTH, W), 0.9, 0.999)
    s_lam = a8 ** (1.0 / LRU_C)
    dt_g = jnp.exp(uni(22, (DEPTH, GDN_HEADS), math.log(1e-3), math.log(1e-1)))
    return {
        'x_prompt': nrm(0, (BATCH, SEQ, D_MODEL), 1.0),
        'x_sample': nrm(1, (DEC_BATCH, DEC_SEQ, D_MODEL), 1.0),
        'state_lru_h': nrm(2, ss['lru_h'], 0.5),
        'state_lru_conv': nrm(3, ss['lru_conv'], 1.0),
        'state_gdn_S': nrm(4, ss['gdn_S'], 0.3),
        'state_gdn_conv': nrm(5, ss['gdn_conv'], 1.0),
        'state_s5_re': nrm(6, ss['s5_re'], 0.1),
        'state_s5_im': nrm(7, ss['s5_im'], 0.1),
        'state_ml_C': nrm(8, ss['ml_C'], 0.3),
        'state_ml_n': nrm(9, ss['ml_n'], 0.3),
        'state_ml_m': nrm(10, ss['ml_m'], 1.0),
        'state_ml_conv': nrm(11, ss['ml_conv'], 1.0),
        'norm_w': 1.0 + nrm(12, (DEPTH, D_MODEL), 0.01),
        'w_in': nrm(13, (DEPTH, D_MODEL, IN_WIDTH), D_MODEL ** -0.5),
        'lru_conv_w': nrm(14, (DEPTH, CONV_W, W), 0.5),
        'lru_w_a': nrm(15, (DEPTH, LRU_BLOCKS, LRU_BLK, LRU_BLK), LRU_BLK ** -0.5),
        'lru_b_a': nrm(16, (DEPTH, W), 0.01),
        'lru_w_x': nrm(17, (DEPTH, LRU_BLOCKS, LRU_BLK, LRU_BLK), LRU_BLK ** -0.5),
        'lru_b_x': nrm(18, (DEPTH, W), 0.01),
        'lru_lambda': jnp.log(s_lam) - jnp.log1p(-s_lam),
        'gdn_conv_w': nrm(20, (DEPTH, CONV_W, 3 * W), 0.5),
        'gdn_A_log': jnp.log(uni(21, (DEPTH, GDN_HEADS), 1.0, 16.0)),
        'gdn_dt_bias': dt_g + jnp.log(-jnp.expm1(-dt_g)),
        'gdn_norm_w': 1.0 + nrm(23, (DEPTH, GDN_DV), 0.01),
        's5_lam_re': -0.5 + nrm(24, (DEPTH, S5_GROUPS, S5_STATE), 0.01),
        's5_lam_im': jnp.pi * jnp.arange(S5_STATE, dtype=f32) + nrm(25, (DEPTH, S5_GROUPS, S5_STATE), 0.01),
        's5_log_dt': uni(26, (DEPTH, S5_GROUPS), math.log(1e-3), math.log(1e-1)),
        's5_b_re': nrm(27, (DEPTH, S5_GROUPS, S5_STATE, S5_GROUP), (2 * S5_GROUP) ** -0.5),
        's5_b_im': nrm(28, (DEPTH, S5_GROUPS, S5_STATE, S5_GROUP), (2 * S5_GROUP) ** -0.5),
        's5_c_re': nrm(29, (DEPTH, S5_GROUPS, S5_GROUP, S5_STATE), 0.5),
        's5_c_im': nrm(30, (DEPTH, S5_GROUPS, S5_GROUP, S5_STATE), 0.5),
        's5_d': nrm(31, (DEPTH, W), 1.0),
        's5_w_glu': nrm(32, (DEPTH, W, W), W ** -0.5),
        'ml_conv_w': nrm(33, (DEPTH, CONV_W, 2 * W), 0.5),
        'ml_b_i': nrm(34, (DEPTH, ML_HEADS), 0.1),
        'ml_b_f': uni(35, (DEPTH, ML_HEADS), 3.0, 6.0),
        'w_branch': nrm(36, (DEPTH, N_BRANCH, W, D_MODEL), W ** -0.5),
        'w_out': nrm(37, (DEPTH, D_MODEL, D_MODEL), D_MODEL ** -0.5),
        'final_norm_w': 1.0 + nrm(38, (D_MODEL,), 0.01),
    }


def reference(x_prompt, x_sample, state_lru_h, state_lru_conv, state_gdn_S, state_gdn_conv, state_s5_re,
              state_s5_im, state_ml_C, state_ml_n, state_ml_m, state_ml_conv, norm_w, w_in, lru_conv_w,
              lru_w_a, lru_b_a, lru_w_x, lru_b_x, lru_lambda, gdn_conv_w, gdn_A_log, gdn_dt_bias, gdn_norm_w,
              s5_lam_re, s5_lam_im, s5_log_dt, s5_b_re, s5_b_im, s5_c_re, s5_c_im, s5_d, s5_w_glu,
              ml_conv_w, ml_b_i, ml_b_f, w_branch, w_out, final_norm_w):
    params = {'norm_w': norm_w, 'w_in': w_in, 'lru_conv_w': lru_conv_w, 'lru_w_a': lru_w_a, 'lru_b_a': lru_b_a,
              'lru_w_x': lru_w_x, 'lru_b_x': lru_b_x, 'lru_lambda': lru_lambda, 'gdn_conv_w': gdn_conv_w,
              'gdn_A_log': gdn_A_log, 'gdn_dt_bias': gdn_dt_bias, 'gdn_norm_w': gdn_norm_w,
              's5_lam_re': s5_lam_re, 's5_lam_im': s5_lam_im, 's5_log_dt': s5_log_dt, 's5_b_re': s5_b_re,
              's5_b_im': s5_b_im, 's5_c_re': s5_c_re, 's5_c_im': s5_c_im, 's5_d': s5_d, 's5_w_glu': s5_w_glu,
              'ml_conv_w': ml_conv_w, 'ml_b_i': ml_b_i, 'ml_b_f': ml_b_f, 'w_branch': w_branch, 'w_out': w_out}
    st_prompt = {name: jnp.zeros(shape, jnp.float32) for name, shape in state_shapes(x_prompt.shape[0]).items()}
    st_sample = {'lru_h': state_lru_h, 'lru_conv': state_lru_conv, 'gdn_S': state_gdn_S,
                 'gdn_conv': state_gdn_conv, 's5_re': state_s5_re, 's5_im': state_s5_im, 'ml_C': state_ml_C,
                 'ml_n': state_ml_n, 'ml_m': state_ml_m, 'ml_conv': state_ml_conv}
    y_prompt, np_ = trunk(x_prompt, st_prompt, params, final_norm_w)
    y_sample, ns_ = trunk(x_sample, st_sample, params, final_norm_w)
    return (y_prompt, y_sample,
            np_['lru_h'], ns_['lru_h'], np_['lru_conv'], ns_['lru_conv'],
            np_['gdn_S'], ns_['gdn_S'], np_['gdn_conv'], ns_['gdn_conv'],
            np_['s5_re'], ns_['s5_re'], np_['s5_im'], ns_['s5_im'],
            np_['ml_C'], ns_['ml_C'], np_['ml_n'], ns_['ml_n'], np_['ml_m'], ns_['ml_m'],
            np_['ml_conv'], ns_['ml_conv'])
```

```python
import functools
import math

import jax
import jax.numpy as jnp
import numpy as np
from jax import lax
from jax.experimental import pallas as pl
from jax.experimental.pallas import tpu as pltpu

F32 = jnp.float32
BF16 = jnp.bfloat16

D_MODEL = 1024
BW = 256
NH = 4
DH = 64
CONV_W = 4
CHUNK = 64
LRU_C = 8.0
S5_G = 16
S5_N = 64
S5_W = S5_G * S5_N
EPS = 1e-6

W_LRU = 2 * BW
W_S5 = 2 * BW
W_GDN = 4 * BW + 128
W_ML = 5 * BW + 128
W_MIX = (W_LRU, W_S5, W_GDN, W_ML)

VMEM_LIMIT_BYTES = 56 * 1024 * 1024


def _cparams(sem):
    return pltpu.CompilerParams(dimension_semantics=sem, vmem_limit_bytes=VMEM_LIMIT_BYTES)


def _sigmoid(x):
    return 1.0 / (1.0 + jnp.exp(-x))


def _silu(x):
    return x * _sigmoid(x)


def _softplus(x):
    return jnp.maximum(x, 0.0) + jnp.log(1.0 + jnp.exp(-jnp.abs(x)))


def _gelu_tanh(x):
    c = math.sqrt(2.0 / math.pi)
    return x * (0.5 * (1.0 + jnp.tanh(c * (x + 0.044715 * (x * x * x)))))


def _split3(x):
    hi = x.astype(BF16)
    r = x - hi.astype(F32)
    mid = r.astype(BF16)
    lo = (r - mid.astype(F32)).astype(BF16)
    return hi, mid, lo


_NN = (((1,), (0,)), ((), ()))
_NT = (((1,), (1,)), ((), ()))
_TN = (((0,), (0,)), ((), ()))


def _sel_dot(sel, x, dims=_NN):
    out = None
    for p in _split3(x):
        t = lax.dot_general(sel, p, dims, preferred_element_type=F32)
        out = t if out is None else out + t
    return out


def _dot_sel(x, sel, dims=_NN):
    out = None
    for p in _split3(x):
        t = lax.dot_general(p, sel, dims, preferred_element_type=F32)
        out = t if out is None else out + t
    return out


def _bdot(a, b, dims=_NN):
    return lax.dot_general(a.astype(BF16), b.astype(BF16), dims, preferred_element_type=F32)


def _iota(shape, axis):
    return lax.broadcasted_iota(jnp.int32, shape, axis)


def _head_ones():
    return (_iota((BW, BW), 0) // DH == _iota((BW, BW), 1) // DH).astype(BF16)


def _expand_sel(offset):
    return (_iota((128, BW), 0) == _iota((128, BW), 1) // DH + offset).astype(BF16)


def _col_to_row(col, n):
    wide = jnp.broadcast_to(col, (n, 128))
    sel = (_iota((8, 128), 1) == 0).astype(BF16)
    return _sel_dot(sel, wide, _NT)[0:1, :]


def _row_to_col(row, n):
    eye = _iota((n, n), 0) == _iota((n, n), 1)
    return jnp.sum(jnp.where(eye, jnp.broadcast_to(row, (n, n)), 0.0), axis=-1, keepdims=True)


def _rms_rows(x, w_row):
    ms = jnp.mean(x * x, axis=-1, keepdims=True)
    return (x * lax.rsqrt(ms + EPS)) * w_row


def _inproj_body(x_ref, nw_ref, w_ref, *o_refs):
    xn = _rms_rows(x_ref[...], nw_ref[...]).astype(BF16)
    off = 0
    for o_ref in o_refs:
        wd = o_ref.shape[-1]
        o_ref[...] = jnp.dot(xn, w_ref[:, off:off + wd], preferred_element_type=F32)
        off += wd


def _in_proj(x, norm_w, w_mix, tm):
    b, t, _ = x.shape
    wtot = sum(W_MIX)
    return pl.pallas_call(
        _inproj_body,
        grid=(b, t // tm),
        in_specs=[
            pl.BlockSpec((None, tm, D_MODEL), lambda bi, i: (bi, i, 0)),
            pl.BlockSpec((1, D_MODEL), lambda bi, i: (0, 0)),
            pl.BlockSpec((D_MODEL, wtot), lambda bi, i: (0, 0)),
        ],
        out_specs=[pl.BlockSpec((tm, wd), lambda bi, i: (i, bi)) for wd in W_MIX],
        out_shape=[jax.ShapeDtypeStruct((t, b * wd), F32) for wd in W_MIX],
        compiler_params=_cparams(("arbitrary", "arbitrary")),
        name="in_proj",
    )(x, norm_w, w_mix)


def _merge_body(x_ref, a_ref, b_ref, c_ref, d_ref, nw_ref, wm_ref, wb_ref, wo_ref, fw_ref, y_ref, *, final):
    x = x_ref[...]
    xn = _rms_rows(x, nw_ref[...]).astype(BF16)
    merged = None
    for n, r in enumerate((a_ref, b_ref, c_ref, d_ref)):
        gate = _sigmoid(jnp.dot(xn, wm_ref[:, n * D_MODEL:(n + 1) * D_MODEL], preferred_element_type=F32))
        proj = jnp.dot(r[...].astype(BF16), wb_ref[n], preferred_element_type=F32)
        merged = gate * proj if merged is None else merged + gate * proj
    y = x + jnp.dot(merged.astype(BF16), wo_ref[...], preferred_element_type=F32)
    if final:
        y = _rms_rows(y, fw_ref[...])
    y_ref[...] = y


def _merge(x, branches, norm_w, w_merge, w_branch, w_out, final_w, tm, final):
    b, t, _ = x.shape
    br_spec = pl.BlockSpec((tm, BW), lambda bi, i: (i, bi))
    const2 = lambda bi, i: (0, 0)
    return pl.pallas_call(
        functools.partial(_merge_body, final=final),
        grid=(b, t // tm),
        in_specs=[
            pl.BlockSpec((None, tm, D_MODEL), lambda bi, i: (bi, i, 0)),
            br_spec, br_spec, br_spec, br_spec,
            pl.BlockSpec((1, D_MODEL), const2),
            pl.BlockSpec((D_MODEL, NH * D_MODEL), const2),
            pl.BlockSpec((NH, BW, D_MODEL), lambda bi, i: (0, 0, 0)),
            pl.BlockSpec((D_MODEL, D_MODEL), const2),
            pl.BlockSpec((1, D_MODEL), const2),
        ],
        out_specs=pl.BlockSpec((None, tm, D_MODEL), lambda bi, i: (bi, i, 0)),
        out_shape=jax.ShapeDtypeStruct((b, t, D_MODEL), F32),
        compiler_params=_cparams(("arbitrary", "arbitrary")),
        name="merge",
    )(x, *branches, norm_w, w_merge, w_branch, w_out, final_w)


def _lru_body(p_ref, h0_ref, c0_ref, cw_ref, wg_ref, bg_ref, sp_ref, out_ref, ht_ref, ct_ref,
              xp_s, a_s, b_s, h_s):
    tt, nb = p_ref.shape[0], p_ref.shape[1]

    @pl.when(pl.program_id(0) == 0)
    def _():
        h_s[...] = h0_ref[...]
        xp_s[0:CONV_W - 1] = c0_ref[...]

    xp_s[CONV_W - 1:CONV_W - 1 + tt] = p_ref[:, :, 0:BW]
    xa = None
    for j in range(CONV_W):
        term = xp_s[j:j + tt] * cw_ref[j:j + 1, :].reshape(1, 1, BW)
        xa = term if xa is None else xa + term
    tail = xp_s[tt:tt + CONV_W - 1]
    xp_s[0:CONV_W - 1] = tail
    ct_ref[...] = tail

    xa2 = xa.reshape(tt * nb, BW)
    gates = jnp.dot(xa2.astype(BF16), wg_ref[...], preferred_element_type=F32) + bg_ref[...]
    r = _sigmoid(gates[:, 0:BW])
    ig = _sigmoid(gates[:, BW:2 * BW])
    log_a = (-LRU_C * r) * sp_ref[...]
    a = jnp.exp(log_a)
    bb = jnp.sqrt(1.0 - jnp.exp(2.0 * log_a)) * (ig * xa2)
    a_s[...] = a.reshape(tt, nb, BW)
    b_s[...] = bb.reshape(tt, nb, BW)

    def step(t, h):
        h = a_s[t] * h + b_s[t]
        b_s[t] = h
        return h

    h = lax.fori_loop(0, tt, step, h_s[...], unroll=min(8, tt))
    h_s[...] = h
    ht_ref[...] = h
    out_ref[...] = b_s[...] * _silu(p_ref[:, :, BW:2 * BW])


def _lru(p, h0, c0, conv_w, w_gates, b_gates, sp_lam, tt):
    t, nb, _ = p.shape
    c2 = lambda i: (0, 0)
    c3 = lambda i: (0, 0, 0)
    return pl.pallas_call(
        _lru_body,
        grid=(t // tt,),
        in_specs=[
            pl.BlockSpec((tt, nb, W_LRU), lambda i: (i, 0, 0)),
            pl.BlockSpec((nb, BW), c2),
            pl.BlockSpec((CONV_W - 1, nb, BW), c3),
            pl.BlockSpec((CONV_W, BW), c2),
            pl.BlockSpec((BW, 2 * BW), c2),
            pl.BlockSpec((1, 2 * BW), c2),
            pl.BlockSpec((1, BW), c2),
        ],
        out_specs=[
            pl.BlockSpec((tt, nb, BW), lambda i: (i, 0, 0)),
            pl.BlockSpec((nb, BW), c2),
            pl.BlockSpec((CONV_W - 1, nb, BW), c3),
        ],
        out_shape=[
            jax.ShapeDtypeStruct((t, nb, BW), F32),
            jax.ShapeDtypeStruct((nb, BW), F32),
            jax.ShapeDtypeStruct((CONV_W - 1, nb, BW), F32),
        ],
        scratch_shapes=[
            pltpu.VMEM((tt + CONV_W - 1, nb, BW), F32),
            pltpu.VMEM((tt, nb, BW), F32),
            pltpu.VMEM((tt, nb, BW), F32),
            pltpu.VMEM((nb, BW), F32),
        ],
        compiler_params=_cparams(("arbitrary",)),
        name="rglru",
    )(p, h0, c0, conv_w, w_gates, b_gates, sp_lam)


def _s5_body(p_ref, hr0_ref, hi0_ref, ar_ref, ai_ref, bre_ref, bim_ref, cre_ref, cim_ref, d_ref, wglu_ref,
             out_ref, hrt_ref, hit_ref, xr_s, xi_s, hr_s, hi_s):
    tt, nb = p_ref.shape[0], p_ref.shape[1]

    @pl.when(pl.program_id(0) == 0)
    def _():
        hr_s[...] = hr0_ref[...]
        hi_s[...] = hi0_ref[...]

    u2 = p_ref[:, :, 0:BW].reshape(tt * nb, BW)
    ub = u2.astype(BF16)
    xr_s[...] = jnp.dot(ub, bre_ref[...], preferred_element_type=F32).reshape(tt, nb, S5_W)
    xi_s[...] = jnp.dot(ub, bim_ref[...], preferred_element_type=F32).reshape(tt, nb, S5_W)
    ar = jnp.broadcast_to(ar_ref[...], (nb, S5_W))
    ai = jnp.broadcast_to(ai_ref[...], (nb, S5_W))

    def step(t, c):
        hr, hi = c
        nr = ar * hr - ai * hi + xr_s[t]
        ni = ar * hi + ai * hr + xi_s[t]
        xr_s[t] = nr
        xi_s[t] = ni
        return nr, ni

    hr, hi = lax.fori_loop(0, tt, step, (hr_s[...], hi_s[...]), unroll=min(4, tt))
    hr_s[...] = hr
    hi_s[...] = hi
    hrt_ref[...] = hr
    hit_ref[...] = hi

    y = (jnp.dot(xr_s[...].reshape(tt * nb, S5_W).astype(BF16), cre_ref[...], preferred_element_type=F32)
         - jnp.dot(xi_s[...].reshape(tt * nb, S5_W).astype(BF16), cim_ref[...], preferred_element_type=F32)
         + d_ref[...] * u2)
    y = _gelu_tanh(y)
    y = y * _sigmoid(jnp.dot(y.astype(BF16), wglu_ref[...], preferred_element_type=F32))
    out_ref[...] = y.reshape(tt, nb, BW) * _silu(p_ref[:, :, BW:2 * BW])


def _s5(p, hr0, hi0, ar, ai, bre, bim, cre, cim, d, wglu, tt):
    t, nb, _ = p.shape
    c2 = lambda i: (0, 0)
    st = pl.BlockSpec((nb, S5_W), c2)
    row = pl.BlockSpec((1, S5_W), c2)
    return pl.pallas_call(
        _s5_body,
        grid=(t // tt,),
        in_specs=[
            pl.BlockSpec((tt, nb, W_S5), lambda i: (i, 0, 0)),
            st, st, row, row,
            pl.BlockSpec((BW, S5_W), c2), pl.BlockSpec((BW, S5_W), c2),
            pl.BlockSpec((S5_W, BW), c2), pl.BlockSpec((S5_W, BW), c2),
            pl.BlockSpec((1, BW), c2),
            pl.BlockSpec((BW, BW), c2),
        ],
        out_specs=[pl.BlockSpec((tt, nb, BW), lambda i: (i, 0, 0)), st, st],
        out_shape=[
            jax.ShapeDtypeStruct((t, nb, BW), F32),
            jax.ShapeDtypeStruct((nb, S5_W), F32),
            jax.ShapeDtypeStruct((nb, S5_W), F32),
        ],
        scratch_shapes=[
            pltpu.VMEM((tt, nb, S5_W), F32),
            pltpu.VMEM((tt, nb, S5_W), F32),
            pltpu.VMEM((nb, S5_W), F32),
            pltpu.VMEM((nb, S5_W), F32),
        ],
        compiler_params=_cparams(("arbitrary",)),
        name="s5",
    )(p, hr0, hi0, ar, ai, bre, bim, cre, cim, d, wglu)


def _conv_rows(xp_s, p_cols, c0_ref, cw_ref, tg, first):
    @pl.when(first)
    def _():
        xp_s[8 - (CONV_W - 1):8] = c0_ref[...]

    xp_s[8:8 + tg] = p_cols
    y = None
    for j in range(CONV_W):
        term = xp_s[8 - (CONV_W - 1) + j:8 - (CONV_W - 1) + j + tg] * cw_ref[j:j + 1, :]
        y = term if y is None else y + term
    tail = xp_s[8 + tg - (CONV_W - 1):8 + tg]
    xp_s[8 - (CONV_W - 1):8] = tail
    return y, tail


def _tri_masks():
    r = _iota((CHUNK, CHUNK), 0)
    c = _iota((CHUNK, CHUNK), 1)
    return r >= c, r > c


def _gdn_body(p_ref, s0_ref, c0_ref, cw_ref, nega_ref, dtb_ref, nw_ref, out_ref, st_ref, ct_ref,
              xp_s, q_s, k_s, v_s, g_s, o_s, s_s):
    tg = p_ref.shape[0]
    first = pl.program_id(1) == 0

    @pl.when(first)
    def _():
        s_s[...] = s0_ref[...]

    y, tail = _conv_rows(xp_s, p_ref[:, 0:3 * BW], c0_ref, cw_ref, tg, first)
    ct_ref[...] = tail
    qkv = _silu(y)
    ones_h = _head_ones()
    q = qkv[:, 0:BW]
    k = qkv[:, BW:2 * BW]
    q_s[...] = q * lax.rsqrt(_dot_sel(q * q, ones_h) + EPS) * (DH ** -0.5)
    k_s[...] = k * lax.rsqrt(_dot_sel(k * k, ones_h) + EPS)
    v_s[...] = qkv[:, 2 * BW:3 * BW]
    ga = p_ref[:, 4 * BW:4 * BW + 128]
    lane = _iota((tg, 128), 1)
    g = nega_ref[...] * _softplus(ga + dtb_ref[...])
    g_s[...] = jnp.where(lane < NH, g, _sigmoid(ga))

    tri, strict = _tri_masks()
    ltri = tri.astype(BF16)
    eye = (_iota((CHUNK, CHUNK), 0) == _iota((CHUNK, CHUNK), 1)).astype(F32)

    def chunk(c, carry):
        r0 = pl.multiple_of(c * CHUNK, CHUNK)
        rows = pl.ds(r0, CHUNK)
        gb = g_s[rows, :]
        gcum = _sel_dot(ltri, gb)
        for h in range(NH):
            hs = slice(h * DH, (h + 1) * DH)
            qh = q_s[rows, hs]
            kh = k_s[rows, hs]
            vh = v_s[rows, hs]
            gcol = gcum[:, h:h + 1]
            beta = gb[:, NH + h:NH + h + 1]
            grow = _col_to_row(gcol, CHUNK)
            decay = jnp.where(tri, jnp.exp(jnp.where(tri, gcol - grow, 0.0)), 0.0)
            kb = kh * beta
            m = jnp.where(strict, _bdot(kb, kh, _NT) * decay, 0.0)
            npow = -m
            tinv = eye + npow
            for _ in range(5):
                npow = _bdot(npow, npow)
                tinv = tinv + _bdot(tinv, npow)
            egc = jnp.exp(gcol)
            u = _bdot(tinv, vh * beta)
            w = _bdot(tinv, kb * egc)
            attn = _bdot(qh, kh, _NT) * decay
            glast = gcum[CHUNK - 1:CHUNK, h:h + 1]
            s_old = s_s[h]
            v_new = u - _bdot(w, s_old)
            o = _bdot(qh * egc, s_old) + _bdot(attn, v_new)
            s_s[h] = s_old * jnp.exp(glast) + _bdot(kh * jnp.exp(glast - gcol), v_new, _TN)
            o_s[rows, hs] = o
        return carry

    lax.fori_loop(0, tg // CHUNK, chunk, 0)
    st_ref[...] = s_s[...]
    o = o_s[...]
    ms = _dot_sel(o * o, ones_h) * (1.0 / DH)
    out_ref[...] = (o * lax.rsqrt(ms + EPS)) * nw_ref[...] * _silu(p_ref[:, 3 * BW:4 * BW])


def _gdn(p, s0, c0, conv_w, neg_a, dt_bias, norm_w, b, tg):
    t = p.shape[0]
    c2 = lambda bi, i: (0, 0)
    return pl.pallas_call(
        _gdn_body,
        grid=(b, t // tg),
        in_specs=[
            pl.BlockSpec((tg, W_GDN), lambda bi, i: (i, bi)),
            pl.BlockSpec((None, NH, DH, DH), lambda bi, i: (bi, 0, 0, 0)),
            pl.BlockSpec((None, CONV_W - 1, 3 * BW), lambda bi, i: (bi, 0, 0)),
            pl.BlockSpec((CONV_W, 3 * BW), c2),
            pl.BlockSpec((1, 128), c2),
            pl.BlockSpec((1, 128), c2),
            pl.BlockSpec((1, BW), c2),
        ],
        out_specs=[
            pl.BlockSpec((tg, BW), lambda bi, i: (i, bi)),
            pl.BlockSpec((None, NH, DH, DH), lambda bi, i: (bi, 0, 0, 0)),
            pl.BlockSpec((None, CONV_W - 1, 3 * BW), lambda bi, i: (bi, 0, 0)),
        ],
        out_shape=[
            jax.ShapeDtypeStruct((t, b * BW), F32),
            jax.ShapeDtypeStruct((b, NH, DH, DH), F32),
            jax.ShapeDtypeStruct((b, CONV_W - 1, 3 * BW), F32),
        ],
        scratch_shapes=[
            pltpu.VMEM((tg + 8, 3 * BW), F32),
            pltpu.VMEM((tg, BW), F32),
            pltpu.VMEM((tg, BW), F32),
            pltpu.VMEM((tg, BW), F32),
            pltpu.VMEM((tg, 128), F32),
            pltpu.VMEM((tg, BW), F32),
            pltpu.VMEM((NH, DH, DH), F32),
        ],
        compiler_params=_cparams(("arbitrary", "arbitrary")),
        name="gdn",
    )(p, s0, c0, conv_w, neg_a, dt_bias, norm_w)


def _ml_body(p_ref, c0m_ref, n0_ref, m0_ref, c0_ref, cw_ref, bi_ref, bf_ref, out_ref, ct_m_ref, nt_ref, mt_ref, cvt_ref,
             xp_s, q_s, k_s, g_s, h_s, cm_s, n_s, m_s):
    tg = p_ref.shape[0]
    first = pl.program_id(1) == 0

    @pl.when(first)
    def _():
        cm_s[...] = c0m_ref[...]
        n_s[...] = n0_ref[...]
        m_s[...] = m0_ref[...]

    y, tail = _conv_rows(xp_s, p_ref[:, 0:2 * BW], c0_ref, cw_ref, tg, first)
    cvt_ref[...] = tail
    qk = _silu(y)
    q_s[...] = qk[:, 0:BW]
    k_s[...] = qk[:, BW:2 * BW] * (DH ** -0.5)
    ga = p_ref[:, 5 * BW:5 * BW + 128]
    lane = _iota((tg, 128), 1)
    g_s[...] = jnp.where(lane < NH, ga + bi_ref[...], -_softplus(-(ga + bf_ref[...])))

    tri, _ = _tri_masks()
    ltri = tri.astype(BF16)

    def chunk(c, carry):
        r0 = pl.multiple_of(c * CHUNK, CHUNK)
        rows = pl.ds(r0, CHUNK)
        gb = g_s[rows, :]
        bcum = _sel_dot(ltri, gb)
        for h in range(NH):
            hs = slice(h * DH, (h + 1) * DH)
            qh = q_s[rows, hs]
            kh = k_s[rows, hs]
            vh = p_ref[rows, 2 * BW + h * DH:2 * BW + (h + 1) * DH]
            icol = gb[:, h:h + 1]
            bcol = bcum[:, NH + h:NH + h + 1]
            rrow = _col_to_row(icol - bcol, CHUNK)
            dlog = jnp.where(tri, bcol + rrow, -jnp.inf)
            dmax = jnp.max(dlog, axis=-1, keepdims=True)
            qkm = _bdot(qh, kh, _NT)
            blast = bcum[CHUNK - 1:CHUNK, NH + h:NH + h + 1]
            ecol = blast - bcol + icol
            cm = cm_s[h]
            nrow = n_s[h:h + 1, :]
            mm = m_s[h:h + 1, 0:1]
            inter = bcol + mm
            m_t = jnp.maximum(inter, dmax)
            s = qkm * jnp.exp(dlog - m_t)
            w_int = jnp.exp(inter - m_t)
            num = w_int * _bdot(qh, cm) + _bdot(s, vh)
            den = w_int * jnp.sum(qh * nrow, axis=-1, keepdims=True) + jnp.sum(s, axis=-1, keepdims=True)
            h_s[rows, hs] = num / jnp.maximum(jnp.abs(den), jnp.exp(-m_t))
            m_new = jnp.maximum(blast + mm, jnp.max(ecol, axis=0, keepdims=True))
            dec = jnp.exp(blast + mm - m_new)
            wk = jnp.exp(ecol - m_new) * kh
            cm_s[h] = dec * cm + _bdot(wk, vh, _TN)
            n_s[h:h + 1, :] = dec * nrow + jnp.sum(wk, axis=0, keepdims=True)
            m_s[h:h + 1, :] = jnp.broadcast_to(m_new, (1, 128))
        return carry

    lax.fori_loop(0, tg // CHUNK, chunk, 0)
    ct_m_ref[...] = cm_s[...]
    nt_ref[...] = n_s[...]
    mt_ref[...] = m_s[...]
    out_ref[...] = h_s[...] * _sigmoid(p_ref[:, 3 * BW:4 * BW]) * _silu(p_ref[:, 4 * BW:5 * BW])


def _mlstm(p, c0m, n0, m0, c0, conv_w, b_i, b_f, b, tg):
    t = p.shape[0]
    c2 = lambda bi, i: (0, 0)
    per_b3 = lambda bi, i: (bi, 0, 0)
    per_b4 = lambda bi, i: (bi, 0, 0, 0)
    return pl.pallas_call(
        _ml_body,
        grid=(b, t // tg),
        in_specs=[
            pl.BlockSpec((tg, W_ML), lambda bi, i: (i, bi)),
            pl.BlockSpec((None, NH, DH, DH), per_b4),
            pl.BlockSpec((None, 8, DH), per_b3),
            pl.BlockSpec((None, 8, 128), per_b3),
            pl.BlockSpec((None, CONV_W - 1, 2 * BW), per_b3),
            pl.BlockSpec((CONV_W, 2 * BW), c2),
            pl.BlockSpec((1, 128), c2),
            pl.BlockSpec((1, 128), c2),
        ],
        out_specs=[
            pl.BlockSpec((tg, BW), lambda bi, i: (i, bi)),
            pl.BlockSpec((None, NH, DH, DH), per_b4),
            pl.BlockSpec((None, 8, DH), per_b3),
            pl.BlockSpec((None, 8, 128), per_b3),
            pl.BlockSpec((None, CONV_W - 1, 2 * BW), per_b3),
        ],
        out_shape=[
            jax.ShapeDtypeStruct((t, b * BW), F32),
            jax.ShapeDtypeStruct((b, NH, DH, DH), F32),
            jax.ShapeDtypeStruct((b, 8, DH), F32),
            jax.ShapeDtypeStruct((b, 8, 128), F32),
            jax.ShapeDtypeStruct((b, CONV_W - 1, 2 * BW), F32),
        ],
        scratch_shapes=[
            pltpu.VMEM((tg + 8, 2 * BW), F32),
            pltpu.VMEM((tg, BW), F32),
            pltpu.VMEM((tg, BW), F32),
            pltpu.VMEM((tg, 128), F32),
            pltpu.VMEM((tg, BW), F32),
            pltpu.VMEM((NH, DH, DH), F32),
            pltpu.VMEM((8, DH), F32),
            pltpu.VMEM((8, 128), F32),
        ],
        compiler_params=_cparams(("arbitrary", "arbitrary")),
        name="mlstm",
    )(p, c0m, n0, m0, c0, conv_w, b_i, b_f)


def _head_stack(row):
    return jnp.stack([row[:, h * DH:(h + 1) * DH] for h in range(NH)], axis=0)


def _gdn_step_body(p_ref, s_ref, c_ref, cw_ref, nega_ref, dtb_ref, nw_ref, out_ref, so_ref, co_ref,
                   q_s, k_s, v_s, eg_s, be_s):
    nb = p_ref.shape[0]
    x = p_ref[:, 0:3 * BW]
    xa = x * cw_ref[CONV_W - 1:CONV_W, :]
    for j in range(CONV_W - 1):
        xa = xa + c_ref[j] * cw_ref[j:j + 1, :]
    co_ref[0] = c_ref[1]
    co_ref[1] = c_ref[2]
    co_ref[2] = x
    qkv = _silu(xa)
    ones_h = _head_ones()
    q = qkv[:, 0:BW]
    k = qkv[:, BW:2 * BW]
    q_s[...] = q * lax.rsqrt(_dot_sel(q * q, ones_h) + EPS) * (DH ** -0.5)
    k_s[...] = k * lax.rsqrt(_dot_sel(k * k, ones_h) + EPS)
    v_s[...] = qkv[:, 2 * BW:3 * BW]
    ga = p_ref[:, 4 * BW:4 * BW + 128]
    g = nega_ref[...] * _softplus(ga + dtb_ref[...])
    eg_s[...] = jnp.exp(_dot_sel(g, _expand_sel(0)))
    be_s[...] = _dot_sel(_sigmoid(ga), _expand_sel(NH))
    nw3 = nw_ref[...].reshape(1, 1, DH)

    def one(bi, carry):
        r = pl.ds(bi, 1)
        s = s_ref[bi]
        k3 = _row_to_col(k_s[r, :], BW).reshape(NH, DH, 1)
        q3 = _row_to_col(q_s[r, :], BW).reshape(NH, DH, 1)
        eg3 = _row_to_col(eg_s[r, :], BW).reshape(NH, DH, 1)
        be3 = _row_to_col(be_s[r, :], BW).reshape(NH, DH, 1)
        eg_h = eg3[:, 0:1, :]
        be_h = be3[:, 0:1, :]
        v3 = _head_stack(v_s[r, :])
        z3 = _head_stack(p_ref[r, 3 * BW:4 * BW])
        v_new = v3 * be_h - jnp.sum(s * (k3 * be3 * eg3), axis=1, keepdims=True)
        qk = jnp.sum(q3 * k3, axis=1, keepdims=True)
        o = jnp.sum(s * (q3 * eg3), axis=1, keepdims=True) + qk * v_new
        so_ref[bi] = s * eg_h + k3 * v_new
        ms = jnp.mean(o * o, axis=-1, keepdims=True)
        out_ref[bi] = (o * lax.rsqrt(ms + EPS)) * nw3 * _silu(z3)
        return carry

    lax.fori_loop(0, nb, one, 0)


def _gdn_step(p, s0, c0, conv_w, neg_a, dt_bias, norm_w):
    nb = p.shape[0]
    return pl.pallas_call(
        _gdn_step_body,
        out_shape=[
            jax.ShapeDtypeStruct((nb, NH, 1, DH), F32),
            jax.ShapeDtypeStruct((nb, NH, DH, DH), F32),
            jax.ShapeDtypeStruct((CONV_W - 1, nb, 3 * BW), F32),
        ],
        scratch_shapes=[pltpu.VMEM((nb, BW), F32) for _ in range(5)],
        compiler_params=pltpu.CompilerParams(vmem_limit_bytes=VMEM_LIMIT_BYTES),
        name="gdn_step",
    )(p, s0, c0, conv_w, neg_a, dt_bias, norm_w)


def _ml_step_body(p_ref, c_ref, n_ref, m_ref, cv_ref, cw_ref, bi_ref, bf_ref, out_ref, co_ref, no_ref, mo_ref, cvo_ref,
                  q_s, k_s, i_s, f_s, m_s):
    nb = p_ref.shape[0]
    x = p_ref[:, 0:2 * BW]
    xa = x * cw_ref[CONV_W - 1:CONV_W, :]
    for j in range(CONV_W - 1):
        xa = xa + cv_ref[j] * cw_ref[j:j + 1, :]
    cvo_ref[0] = cv_ref[1]
    cvo_ref[1] = cv_ref[2]
    cvo_ref[2] = x
    qk = _silu(xa)
    q_s[...] = qk[:, 0:BW]
    k_s[...] = qk[:, BW:2 * BW] * (DH ** -0.5)
    ga = p_ref[:, 5 * BW:5 * BW + 128]
    i_s[...] = _dot_sel(ga + bi_ref[...], _expand_sel(0))
    f_s[...] = _dot_sel(-_softplus(-(ga + bf_ref[...])), _expand_sel(NH))
    m_s[...] = _dot_sel(m_ref[...], _expand_sel(0))

    def one(bi, carry):
        r = pl.ds(bi, 1)
        cm = c_ref[bi]
        n3 = n_ref[bi]
        qrow = q_s[r, :]
        krow = k_s[r, :]
        q3r = _head_stack(qrow)
        k3r = _head_stack(krow)
        q3 = _row_to_col(qrow, BW).reshape(NH, DH, 1)
        k3 = _row_to_col(krow, BW).reshape(NH, DH, 1)
        ip = _row_to_col(i_s[r, :], BW).reshape(NH, DH, 1)[:, 0:1, :]
        lf = _row_to_col(f_s[r, :], BW).reshape(NH, DH, 1)[:, 0:1, :]
        mm = _row_to_col(m_s[r, :], BW).reshape(NH, DH, 1)[:, 0:1, :]
        v3 = _head_stack(p_ref[r, 2 * BW:3 * BW])
        o3 = _head_stack(p_ref[r, 3 * BW:4 * BW])
        z3 = _head_stack(p_ref[r, 4 * BW:5 * BW])
        qkm = jnp.sum(q3 * k3, axis=1, keepdims=True)
        inter = lf + mm
        m_t = jnp.maximum(inter, ip)
        s = qkm * jnp.exp(ip - m_t)
        w_int = jnp.exp(inter - m_t)
        num = w_int * jnp.sum(cm * q3, axis=1, keepdims=True) + s * v3
        den = w_int * jnp.sum(q3r * n3, axis=-1, keepdims=True) + s
        hd = num / jnp.maximum(jnp.abs(den), jnp.exp(-m_t))
        dec = jnp.exp(inter - m_t)
        wgt = jnp.exp(ip - m_t)
        co_ref[bi] = dec * cm + (wgt * k3) * v3
        no_ref[bi] = dec * n3 + wgt * k3r
        mo_ref[bi] = jnp.broadcast_to(m_t, (NH, 1, 128))
        out_ref[bi] = hd * _sigmoid(o3) * _silu(z3)
        return carry

    lax.fori_loop(0, nb, one, 0)


def _ml_step(p, c0m, n0, m0, c0, conv_w, b_i, b_f):
    nb = p.shape[0]
    return pl.pallas_call(
        _ml_step_body,
        out_shape=[
            jax.ShapeDtypeStruct((nb, NH, 1, DH), F32),
            jax.ShapeDtypeStruct((nb, NH, DH, DH), F32),
            jax.ShapeDtypeStruct((nb, NH, 1, DH), F32),
            jax.ShapeDtypeStruct((nb, NH, 1, 128), F32),
            jax.ShapeDtypeStruct((CONV_W - 1, nb, 2 * BW), F32),
        ],
        scratch_shapes=[pltpu.VMEM((nb, BW), F32) for _ in range(5)],
        compiler_params=pltpu.CompilerParams(vmem_limit_bytes=VMEM_LIMIT_BYTES),
        name="mlstm_step",
    )(p, c0m, n0, m0, c0, conv_w, b_i, b_f)


def _pad_lanes(x, width):
    return jnp.pad(x, [(0, 0)] * (x.ndim - 1) + [(0, width - x.shape[-1])])


def _prep_layer(l, prm):
    w_in = prm['w_in'][l]
    seg = {}
    off = 0
    for name, width in (('lru_x', BW), ('lru_z', BW), ('gdn_q', BW), ('gdn_k', BW), ('gdn_v', BW), ('gdn_z', BW),
                        ('gdn_a', NH), ('gdn_b', NH), ('s5_u', BW), ('s5_z', BW),
                        ('ml_q', BW), ('ml_k', BW), ('ml_v', BW), ('ml_o', BW), ('ml_z', BW),
                        ('ml_i', NH), ('ml_f', NH), ('merge', NH * D_MODEL)):
        seg[name] = w_in[:, off:off + width]
        off += width
    w_mix = jnp.concatenate([
        seg['lru_x'], seg['lru_z'],
        seg['s5_u'], seg['s5_z'],
        seg['gdn_q'], seg['gdn_k'], seg['gdn_v'], seg['gdn_z'], _pad_lanes(jnp.concatenate([seg['gdn_a'], seg['gdn_b']], 1), 128),
        seg['ml_q'], seg['ml_k'], seg['ml_v'], seg['ml_o'], seg['ml_z'], _pad_lanes(jnp.concatenate([seg['ml_i'], seg['ml_f']], 1), 128),
    ], axis=1).astype(BF16)

    eye4 = jnp.eye(4, dtype=F32)
    blockdiag = lambda w: jnp.einsum('ncd,nm->ncmd', w, eye4).reshape(BW, BW)
    lru_wg = jnp.concatenate([blockdiag(prm['lru_w_a'][l]), blockdiag(prm['lru_w_x'][l])], axis=1).astype(BF16)
    lru_bg = jnp.concatenate([prm['lru_b_a'][l], prm['lru_b_x'][l]])[None, :]
    lru_sp = jax.nn.softplus(-prm['lru_lambda'][l])[None, :]

    lam_re, lam_im = prm['s5_lam_re'][l], prm['s5_lam_im'][l]
    dt = jnp.exp(prm['s5_log_dt'][l])[:, None]
    mag = jnp.exp(lam_re * dt)
    ang = lam_im * dt
    ar, ai = mag * jnp.cos(ang), mag * jnp.sin(ang)
    den = lam_re * lam_re + lam_im * lam_im
    cr = ((ar - 1.0) * lam_re + ai * lam_im) / den
    ci = (ai * lam_re - (ar - 1.0) * lam_im) / den
    b_re, b_im = prm['s5_b_re'][l], prm['s5_b_im'][l]
    bb_re = cr[..., None] * b_re - ci[..., None] * b_im
    bb_im = cr[..., None] * b_im + ci[..., None] * b_re
    eye_g = jnp.eye(S5_G, dtype=F32)
    in_dense = lambda bb: jnp.einsum('gnc,gh->gchn', bb, eye_g).reshape(BW, S5_W).astype(BF16)
    out_dense = lambda cc: jnp.einsum('gcn,gh->gnhc', cc, eye_g).reshape(S5_W, BW).astype(BF16)

    gate_row = lambda a, b: _pad_lanes(jnp.concatenate([a, b])[None, :], 128)
    zeros4 = jnp.zeros((NH,), F32)
    return dict(
        norm_w=prm['norm_w'][l][None, :],
        w_mix=w_mix,
        w_merge=seg['merge'].astype(BF16),
        w_branch=prm['w_branch'][l].astype(BF16),
        w_out=prm['w_out'][l].astype(BF16),
        lru_cw=prm['lru_conv_w'][l], lru_wg=lru_wg, lru_bg=lru_bg, lru_sp=lru_sp,
        s5_ar=ar.reshape(1, S5_W), s5_ai=ai.reshape(1, S5_W),
        s5_bre=in_dense(bb_re), s5_bim=in_dense(bb_im),
        s5_cre=out_dense(prm['s5_c_re'][l]), s5_cim=out_dense(prm['s5_c_im'][l]),
        s5_d=prm['s5_d'][l][None, :], s5_wglu=prm['s5_w_glu'][l].astype(BF16),
        gdn_cw=prm['gdn_conv_w'][l],
        gdn_nega=gate_row(-jnp.exp(prm['gdn_A_log'][l]), zeros4),
        gdn_dtb=gate_row(prm['gdn_dt_bias'][l], zeros4),
        gdn_nw=jnp.tile(prm['gdn_norm_w'][l], NH)[None, :],
        gdn_nw1=prm['gdn_norm_w'][l][None, :],
        ml_cw=prm['ml_conv_w'][l],
        ml_bi=gate_row(prm['ml_b_i'][l], zeros4),
        ml_bf=gate_row(zeros4, prm['ml_b_f'][l]),
    )


def _prompt_layer(x, pp, final_w, final, tm, tt, tg):
    b, t, _ = x.shape
    p_lru, p_s5, p_gdn, p_ml = _in_proj(x, pp['norm_w'], pp['w_mix'], tm)
    z = lambda *s: jnp.zeros(s, F32)
    out_a, lru_h, lru_c = _lru(p_lru.reshape(t, b, W_LRU), z(b, BW), z(CONV_W - 1, b, BW),
                               pp['lru_cw'], pp['lru_wg'], pp['lru_bg'], pp['lru_sp'], tt)
    out_c, s5_re, s5_im = _s5(p_s5.reshape(t, b, W_S5), z(b, S5_W), z(b, S5_W), pp['s5_ar'], pp['s5_ai'],
                              pp['s5_bre'], pp['s5_bim'], pp['s5_cre'], pp['s5_cim'], pp['s5_d'], pp['s5_wglu'], tt)
    out_b, gdn_s, gdn_c = _gdn(p_gdn, z(b, NH, DH, DH), z(b, CONV_W - 1, 3 * BW), pp['gdn_cw'], pp['gdn_nega'],
                               pp['gdn_dtb'], pp['gdn_nw'], b, tg)
    out_d, ml_c, ml_n, ml_m, ml_cv = _mlstm(p_ml, z(b, NH, DH, DH), z(b, 8, DH), z(b, 8, 128), z(b, CONV_W - 1, 2 * BW),
                                            pp['ml_cw'], pp['ml_bi'], pp['ml_bf'], b, tg)
    y = _merge(x, (out_a.reshape(t, b * BW), out_b, out_c.reshape(t, b * BW), out_d), pp['norm_w'], pp['w_merge'],
               pp['w_branch'], pp['w_out'], final_w, tm, final)
    st = dict(lru_h=lru_h, lru_conv=jnp.transpose(lru_c, (1, 0, 2)), gdn_S=gdn_s, gdn_conv=gdn_c,
              s5_re=s5_re.reshape(b, S5_G, S5_N), s5_im=s5_im.reshape(b, S5_G, S5_N),
              ml_C=ml_c, ml_n=ml_n[:, :NH, :], ml_m=ml_m[:, :NH, 0], ml_conv=ml_cv)
    return y, st


def _sample_layer(x, st, pp, final_w, final):
    nb = x.shape[1]
    p_lru, p_s5, p_gdn, p_ml = _in_proj(x, pp['norm_w'], pp['w_mix'], nb)
    tmaj = lambda c: jnp.transpose(c, (1, 0, 2))
    out_a, lru_h, lru_c = _lru(p_lru.reshape(1, nb, W_LRU), st['lru_h'], tmaj(st['lru_conv']),
                               pp['lru_cw'], pp['lru_wg'], pp['lru_bg'], pp['lru_sp'], 1)
    out_c, s5_re, s5_im = _s5(p_s5.reshape(1, nb, W_S5), st['s5_re'].reshape(nb, S5_W), st['s5_im'].reshape(nb, S5_W),
                              pp['s5_ar'], pp['s5_ai'], pp['s5_bre'], pp['s5_bim'], pp['s5_cre'], pp['s5_cim'],
                              pp['s5_d'], pp['s5_wglu'], 1)
    out_b, gdn_s, gdn_c = _gdn_step(p_gdn, st['gdn_S'], tmaj(st['gdn_conv']), pp['gdn_cw'], pp['gdn_nega'],
                                    pp['gdn_dtb'], pp['gdn_nw1'])
    out_d, ml_c, ml_n, ml_m, ml_cv = _ml_step(p_ml, st['ml_C'], st['ml_n'].reshape(nb, NH, 1, DH),
                                              _pad_lanes(st['ml_m'], 128), tmaj(st['ml_conv']),
                                              pp['ml_cw'], pp['ml_bi'], pp['ml_bf'])
    y = _merge(x, (out_a.reshape(nb, BW), out_b.reshape(nb, BW), out_c.reshape(nb, BW), out_d.reshape(nb, BW)),
               pp['norm_w'], pp['w_merge'], pp['w_branch'], pp['w_out'], final_w, nb, final)
    new = dict(lru_h=lru_h, lru_conv=tmaj(lru_c), gdn_S=gdn_s, gdn_conv=tmaj(gdn_c),
               s5_re=s5_re.reshape(nb, S5_G, S5_N), s5_im=s5_im.reshape(nb, S5_G, S5_N),
               ml_C=ml_c, ml_n=ml_n.reshape(nb, NH, DH), ml_m=ml_m[:, :, 0, 0], ml_conv=tmaj(ml_cv))
    return y, new


STATE_NAMES = ('lru_h', 'lru_conv', 'gdn_S', 'gdn_conv', 's5_re', 's5_im', 'ml_C', 'ml_n', 'ml_m', 'ml_conv')


def kernel(x_prompt, x_sample, state_lru_h, state_lru_conv, state_gdn_S, state_gdn_conv, state_s5_re, state_s5_im, state_ml_C, state_ml_n, state_ml_m, state_ml_conv, norm_w, w_in, lru_conv_w, lru_w_a, lru_b_a, lru_w_x, lru_b_x, lru_lambda, gdn_conv_w, gdn_A_log, gdn_dt_bias, gdn_norm_w, s5_lam_re, s5_lam_im, s5_log_dt, s5_b_re, s5_b_im, s5_c_re, s5_c_im, s5_d, s5_w_glu, ml_conv_w, ml_b_i, ml_b_f, w_branch, w_out, final_norm_w):
    prm = dict(norm_w=norm_w, w_in=w_in, lru_conv_w=lru_conv_w, lru_w_a=lru_w_a, lru_b_a=lru_b_a, lru_w_x=lru_w_x,
               lru_b_x=lru_b_x, lru_lambda=lru_lambda, gdn_conv_w=gdn_conv_w, gdn_A_log=gdn_A_log,
               gdn_dt_bias=gdn_dt_bias, gdn_norm_w=gdn_norm_w, s5_lam_re=s5_lam_re, s5_lam_im=s5_lam_im,
               s5_log_dt=s5_log_dt, s5_b_re=s5_b_re, s5_b_im=s5_b_im, s5_c_re=s5_c_re, s5_c_im=s5_c_im, s5_d=s5_d,
               s5_w_glu=s5_w_glu, ml_conv_w=ml_conv_w, ml_b_i=ml_b_i, ml_b_f=ml_b_f, w_branch=w_branch, w_out=w_out)
    depth = w_in.shape[0]
    t = x_prompt.shape[1]
    nb = x_sample.shape[0]
    final_w = final_norm_w[None, :]
    st_in = dict(lru_h=state_lru_h, lru_conv=state_lru_conv, gdn_S=state_gdn_S, gdn_conv=state_gdn_conv,
                 s5_re=state_s5_re, s5_im=state_s5_im, ml_C=state_ml_C, ml_n=state_ml_n, ml_m=state_ml_m,
                 ml_conv=state_ml_conv)
    tm = min(512, t)
    tt = min(128, t)
    tg = min(256, t)

    xp = x_prompt
    xs = x_sample.reshape(1, nb, D_MODEL)
    new_p = {n: [] for n in STATE_NAMES}
    new_s = {n: [] for n in STATE_NAMES}
    for l in range(depth):
        pp = _prep_layer(l, prm)
        final = l == depth - 1
        xp, sp = _prompt_layer(xp, pp, final_w, final, tm, tt, tg)
        xs, ss = _sample_layer(xs, {n: st_in[n][l] for n in STATE_NAMES}, pp, final_w, final)
        for n in STATE_NAMES:
            new_p[n].append(sp[n])
            new_s[n].append(ss[n])
    np_ = {n: jnp.stack(v) for n, v in new_p.items()}
    ns_ = {n: jnp.stack(v) for n, v in new_s.items()}
    y_sample = xs.reshape(nb, 1, D_MODEL)
    return (xp, y_sample,
            np_['lru_h'], ns_['lru_h'], np_['lru_conv'], ns_['lru_conv'],
            np_['gdn_S'], ns_['gdn_S'], np_['gdn_conv'], ns_['gdn_conv'],
            np_['s5_re'], ns_['s5_re'], np_['s5_im'], ns_['s5_im'],
            np_['ml_C'], ns_['ml_C'], np_['ml_n'], ns_['ml_n'], np_['ml_m'], ns_['ml_m'],
            np_['ml_conv'], ns_['ml_conv'])
```

```python
import functools
import math

import jax
import jax.numpy as jnp
import numpy as np
from jax import lax
from jax.experimental import pallas as pl
from jax.experimental.pallas import tpu as pltpu

F32 = jnp.float32
BF16 = jnp.bfloat16

D_MODEL = 1024
BW = 256
NH = 4
DH = 64
CONV_W = 4
CHUNK = 64
LRU_C = 8.0
S5_G = 16
S5_N = 64
S5_W = S5_G * S5_N
EPS = 1e-6

W_LRU = 2 * BW
W_S5 = 2 * BW
W_GDN = 4 * BW + 128
W_ML = 5 * BW + 128
W_MIX = (W_LRU, W_S5, W_GDN, W_ML)

VMEM_LIMIT_BYTES = 56 * 1024 * 1024


def _cparams(sem):
    return pltpu.CompilerParams(dimension_semantics=sem, vmem_limit_bytes=VMEM_LIMIT_BYTES)


def _sigmoid(x):
    return 1.0 / (1.0 + jnp.exp(-x))


def _silu(x):
    return x * _sigmoid(x)


def _softplus(x):
    return jnp.maximum(x, 0.0) + jnp.log(1.0 + jnp.exp(-jnp.abs(x)))


def _gelu_tanh(x):
    c = math.sqrt(2.0 / math.pi)
    return x * (0.5 * (1.0 + jnp.tanh(c * (x + 0.044715 * (x * x * x)))))


def _split3(x):
    hi = x.astype(BF16)
    r = x - hi.astype(F32)
    mid = r.astype(BF16)
    lo = (r - mid.astype(F32)).astype(BF16)
    return hi, mid, lo


_NN = (((1,), (0,)), ((), ()))
_NT = (((1,), (1,)), ((), ()))
_TN = (((0,), (0,)), ((), ()))


def _sel_dot(sel, x, dims=_NN):
    out = None
    for p in _split3(x):
        t = lax.dot_general(sel, p, dims, preferred_element_type=F32)
        out = t if out is None else out + t
    return out


def _dot_sel(x, sel, dims=_NN):
    out = None
    for p in _split3(x):
        t = lax.dot_general(p, sel, dims, preferred_element_type=F32)
        out = t if out is None else out + t
    return out


def _bdot(a, b, dims=_NN):
    return lax.dot_general(a.astype(BF16), b.astype(BF16), dims, preferred_element_type=F32)


def _iota(shape, axis):
    return lax.broadcasted_iota(jnp.int32, shape, axis)


def _head_ones():
    return (_iota((BW, BW), 0) // DH == _iota((BW, BW), 1) // DH).astype(BF16)


def _expand_sel(offset):
    return (_iota((128, BW), 0) == _iota((128, BW), 1) // DH + offset).astype(BF16)


def _col_to_row(col, n):
    wide = jnp.broadcast_to(col, (n, 128))
    sel = (_iota((8, 128), 1) == 0).astype(BF16)
    return _sel_dot(sel, wide, _NT)[0:1, :]


def _row_to_col(row, n):
    eye = _iota((n, n), 0) == _iota((n, n), 1)
    return jnp.sum(jnp.where(eye, jnp.broadcast_to(row, (n, n)), 0.0), axis=-1, keepdims=True)


def _rms_rows(x, w_row):
    ms = jnp.mean(x * x, axis=-1, keepdims=True)
    return (x * lax.rsqrt(ms + EPS)) * w_row


def _inproj_body(x_ref, nw_ref, w_ref, *o_refs):
    xn = _rms_rows(x_ref[...], nw_ref[...]).astype(BF16)
    off = 0
    for o_ref in o_refs:
        wd = o_ref.shape[-1]
        o_ref[...] = jnp.dot(xn, w_ref[:, off:off + wd], preferred_element_type=F32)
        off += wd


def _in_proj(x, norm_w, w_mix, tm):
    b, t, _ = x.shape
    wtot = sum(W_MIX)
    return pl.pallas_call(
        _inproj_body,
        grid=(b, t // tm),
        in_specs=[
            pl.BlockSpec((None, tm, D_MODEL), lambda bi, i: (bi, i, 0)),
            pl.BlockSpec((1, D_MODEL), lambda bi, i: (0, 0)),
            pl.BlockSpec((D_MODEL, wtot), lambda bi, i: (0, 0)),
        ],
        out_specs=[pl.BlockSpec((tm, wd), lambda bi, i: (i, bi)) for wd in W_MIX],
        out_shape=[jax.ShapeDtypeStruct((t, b * wd), F32) for wd in W_MIX],
        compiler_params=_cparams(("arbitrary", "arbitrary")),
        name="in_proj",
    )(x, norm_w, w_mix)


def _merge_body(x_ref, a_ref, b_ref, c_ref, d_ref, nw_ref, wm_ref, wb_ref, wo_ref, fw_ref, y_ref, *, final):
    x = x_ref[...]
    xn = _rms_rows(x, nw_ref[...]).astype(BF16)
    merged = None
    for n, r in enumerate((a_ref, b_ref, c_ref, d_ref)):
        gate = _sigmoid(jnp.dot(xn, wm_ref[:, n * D_MODEL:(n + 1) * D_MODEL], preferred_element_type=F32))
        proj = jnp.dot(r[...].astype(BF16), wb_ref[n], preferred_element_type=F32)
        merged = gate * proj if merged is None else merged + gate * proj
    y = x + jnp.dot(merged.astype(BF16), wo_ref[...], preferred_element_type=F32)
    if final:
        y = _rms_rows(y, fw_ref[...])
    y_ref[...] = y


def _merge(x, branches, norm_w, w_merge, w_branch, w_out, final_w, tm, final):
    b, t, _ = x.shape
    br_spec = pl.BlockSpec((tm, BW), lambda bi, i: (i, bi))
    const2 = lambda bi, i: (0, 0)
    return pl.pallas_call(
        functools.partial(_merge_body, final=final),
        grid=(b, t // tm),
        in_specs=[
            pl.BlockSpec((None, tm, D_MODEL), lambda bi, i: (bi, i, 0)),
            br_spec, br_spec, br_spec, br_spec,
            pl.BlockSpec((1, D_MODEL), const2),
            pl.BlockSpec((D_MODEL, NH * D_MODEL), const2),
            pl.BlockSpec((NH, BW, D_MODEL), lambda bi, i: (0, 0, 0)),
            pl.BlockSpec((D_MODEL, D_MODEL), const2),
            pl.BlockSpec((1, D_MODEL), const2),
        ],
        out_specs=pl.BlockSpec((None, tm, D_MODEL), lambda bi, i: (bi, i, 0)),
        out_shape=jax.ShapeDtypeStruct((b, t, D_MODEL), F32),
        compiler_params=_cparams(("arbitrary", "arbitrary")),
        name="merge",
    )(x, *branches, norm_w, w_merge, w_branch, w_out, final_w)


def _lru_body(p_ref, h0_ref, c0_ref, cw_ref, wg_ref, bg_ref, sp_ref, out_ref, ht_ref, ct_ref,
              xp_s, a_s, b_s, h_s):
    tt, nb = p_ref.shape[0], p_ref.shape[1]

    @pl.when(pl.program_id(0) == 0)
    def _():
        h_s[...] = h0_ref[...]
        xp_s[0:CONV_W - 1] = c0_ref[...]

    xp_s[CONV_W - 1:CONV_W - 1 + tt] = p_ref[:, :, 0:BW]
    xa = None
    for j in range(CONV_W):
        term = xp_s[j:j + tt] * cw_ref[j:j + 1, :].reshape(1, 1, BW)
        xa = term if xa is None else xa + term
    tail = xp_s[tt:tt + CONV_W - 1]
    xp_s[0:CONV_W - 1] = tail
    ct_ref[...] = tail

    xa2 = xa.reshape(tt * nb, BW)
    gates = jnp.dot(xa2.astype(BF16), wg_ref[...], preferred_element_type=F32) + bg_ref[...]
    r = _sigmoid(gates[:, 0:BW])
    ig = _sigmoid(gates[:, BW:2 * BW])
    log_a = (-LRU_C * r) * sp_ref[...]
    a = jnp.exp(log_a)
    bb = jnp.sqrt(1.0 - jnp.exp(2.0 * log_a)) * (ig * xa2)
    a_s[...] = a.reshape(tt, nb, BW)
    b_s[...] = bb.reshape(tt, nb, BW)

    def step(t, h):
        h = a_s[t] * h + b_s[t]
        b_s[t] = h
        return h

    h = lax.fori_loop(0, tt, step, h_s[...], unroll=min(8, tt))
    h_s[...] = h
    ht_ref[...] = h
    out_ref[...] = b_s[...] * _silu(p_ref[:, :, BW:2 * BW])


def _lru(p, h0, c0, conv_w, w_gates, b_gates, sp_lam, tt):
    t, nb, _ = p.shape
    c2 = lambda i: (0, 0)
    c3 = lambda i: (0, 0, 0)
    return pl.pallas_call(
        _lru_body,
        grid=(t // tt,),
        in_specs=[
            pl.BlockSpec((tt, nb, W_LRU), lambda i: (i, 0, 0)),
            pl.BlockSpec((nb, BW), c2),
            pl.BlockSpec((CONV_W - 1, nb, BW), c3),
            pl.BlockSpec((CONV_W, BW), c2),
            pl.BlockSpec((BW, 2 * BW), c2),
            pl.BlockSpec((1, 2 * BW), c2),
            pl.BlockSpec((1, BW), c2),
        ],
        out_specs=[
            pl.BlockSpec((tt, nb, BW), lambda i: (i, 0, 0)),
            pl.BlockSpec((nb, BW), c2),
            pl.BlockSpec((CONV_W - 1, nb, BW), c3),
        ],
        out_shape=[
            jax.ShapeDtypeStruct((t, nb, BW), F32),
            jax.ShapeDtypeStruct((nb, BW), F32),
            jax.ShapeDtypeStruct((CONV_W - 1, nb, BW), F32),
        ],
        scratch_shapes=[
            pltpu.VMEM((tt + CONV_W - 1, nb, BW), F32),
            pltpu.VMEM((tt, nb, BW), F32),
            pltpu.VMEM((tt, nb, BW), F32),
            pltpu.VMEM((nb, BW), F32),
        ],
        compiler_params=_cparams(("arbitrary",)),
        name="rglru",
    )(p, h0, c0, conv_w, w_gates, b_gates, sp_lam)


def _s5_body(p_ref, hr0_ref, hi0_ref, ar_ref, ai_ref, bre_ref, bim_ref, cre_ref, cim_ref, d_ref, wglu_ref,
             out_ref, hrt_ref, hit_ref, xr_s, xi_s, hr_s, hi_s):
    tt, nb = p_ref.shape[0], p_ref.shape[1]

    @pl.when(pl.program_id(0) == 0)
    def _():
        hr_s[...] = hr0_ref[...]
        hi_s[...] = hi0_ref[...]

    u2 = p_ref[:, :, 0:BW].reshape(tt * nb, BW)
    ub = u2.astype(BF16)
    xr_s[...] = jnp.dot(ub, bre_ref[...], preferred_element_type=F32).reshape(tt, nb, S5_W)
    xi_s[...] = jnp.dot(ub, bim_ref[...], preferred_element_type=F32).reshape(tt, nb, S5_W)
    ar = jnp.broadcast_to(ar_ref[...], (nb, S5_W))
    ai = jnp.broadcast_to(ai_ref[...], (nb, S5_W))

    def step(t, c):
        hr, hi = c
        nr = ar * hr - ai * hi + xr_s[t]
        ni = ar * hi + ai * hr + xi_s[t]
        xr_s[t] = nr
        xi_s[t] = ni
        return nr, ni

    hr, hi = lax.fori_loop(0, tt, step, (hr_s[...], hi_s[...]), unroll=min(4, tt))
    hr_s[...] = hr
    hi_s[...] = hi
    hrt_ref[...] = hr
    hit_ref[...] = hi

    y = (jnp.dot(xr_s[...].reshape(tt * nb, S5_W).astype(BF16), cre_ref[...], preferred_element_type=F32)
         - jnp.dot(xi_s[...].reshape(tt * nb, S5_W).astype(BF16), cim_ref[...], preferred_element_type=F32)
         + d_ref[...] * u2)
    y = _gelu_tanh(y)
    y = y * _sigmoid(jnp.dot(y.astype(BF16), wglu_ref[...], preferred_element_type=F32))
    out_ref[...] = y.reshape(tt, nb, BW) * _silu(p_ref[:, :, BW:2 * BW])


def _s5(p, hr0, hi0, ar, ai, bre, bim, cre, cim, d, wglu, tt):
    t, nb, _ = p.shape
    c2 = lambda i: (0, 0)
    st = pl.BlockSpec((nb, S5_W), c2)
    row = pl.BlockSpec((1, S5_W), c2)
    return pl.pallas_call(
        _s5_body,
        grid=(t // tt,),
        in_specs=[
            pl.BlockSpec((tt, nb, W_S5), lambda i: (i, 0, 0)),
            st, st, row, row,
            pl.BlockSpec((BW, S5_W), c2), pl.BlockSpec((BW, S5_W), c2),
            pl.BlockSpec((S5_W, BW), c2), pl.BlockSpec((S5_W, BW), c2),
            pl.BlockSpec((1, BW), c2),
            pl.BlockSpec((BW, BW), c2),
        ],
        out_specs=[pl.BlockSpec((tt, nb, BW), lambda i: (i, 0, 0)), st, st],
        out_shape=[
            jax.ShapeDtypeStruct((t, nb, BW), F32),
            jax.ShapeDtypeStruct((nb, S5_W), F32),
            jax.ShapeDtypeStruct((nb, S5_W), F32),
        ],
        scratch_shapes=[
            pltpu.VMEM((tt, nb, S5_W), F32),
            pltpu.VMEM((tt, nb, S5_W), F32),
            pltpu.VMEM((nb, S5_W), F32),
            pltpu.VMEM((nb, S5_W), F32),
        ],
        compiler_params=_cparams(("arbitrary",)),
        name="s5",
    )(p, hr0, hi0, ar, ai, bre, bim, cre, cim, d, wglu)


def _conv_rows(xp_s, p_cols, c0_ref, cw_ref, tg, first):
    @pl.when(first)
    def _():
        xp_s[8 - (CONV_W - 1):8] = c0_ref[...]

    xp_s[8:8 + tg] = p_cols
    y = None
    for j in range(CONV_W):
        term = xp_s[8 - (CONV_W - 1) + j:8 - (CONV_W - 1) + j + tg] * cw_ref[j:j + 1, :]
        y = term if y is None else y + term
    tail = xp_s[8 + tg - (CONV_W - 1):8 + tg]
    xp_s[8 - (CONV_W - 1):8] = tail
    return y, tail


SE = NH * CHUNK


def _se_masks():
    r = _iota((SE, SE), 0)
    c = _iota((SE, SE), 1)
    same = (r // CHUNK) == (c // CHUNK)
    ri = r % CHUNK
    ci = c % CHUNK
    return same, same & (ri >= ci), same & (ri > ci)


def _tile_se(x, same):
    return jnp.where(same, jnp.concatenate([x] * NH, axis=0), 0.0)


def _se_col(mat, off):
    return jnp.concatenate([mat[:, off + h:off + h + 1] for h in range(NH)], axis=0)


def _se_last(mat, r, off):
    return jnp.concatenate([jnp.broadcast_to(mat[r:r + 1, off + h:off + h + 1], (CHUNK, 1)) for h in range(NH)], axis=0)


def _fold_se(x_se):
    out = x_se[0:CHUNK]
    for h in range(1, NH):
        out = out + x_se[h * CHUNK:(h + 1) * CHUNK]
    return out


def _load_bd(s_s, blocks_ref):
    s_s[...] = jnp.zeros((SE, SE), F32)
    for h in range(NH):
        s_s[h * DH:(h + 1) * DH, h * DH:(h + 1) * DH] = blocks_ref[h]


def _store_bd(blocks_ref, s_s):
    for h in range(NH):
        blocks_ref[h] = s_s[h * DH:(h + 1) * DH, h * DH:(h + 1) * DH]


def _gdn_body(p_ref, s0_ref, c0_ref, cw_ref, nega_ref, dtb_ref, nw_ref, out_ref, st_ref, ct_ref,
              xp_s, q_s, k_s, v_s, g_s, o_s, s_s):
    tg = p_ref.shape[0]
    nc = tg // CHUNK
    first = pl.program_id(1) == 0

    @pl.when(first)
    def _():
        _load_bd(s_s, s0_ref)

    y, tail = _conv_rows(xp_s, p_ref[:, 0:3 * BW], c0_ref, cw_ref, tg, first)
    ct_ref[...] = tail
    qkv = _silu(y)
    ones_h = _head_ones()
    q = qkv[:, 0:BW]
    k = qkv[:, BW:2 * BW]
    q_s[...] = q * lax.rsqrt(_dot_sel(q * q, ones_h) + EPS) * (DH ** -0.5)
    k_s[...] = k * lax.rsqrt(_dot_sel(k * k, ones_h) + EPS)
    v_s[...] = qkv[:, 2 * BW:3 * BW]
    ga = p_ref[:, 4 * BW:4 * BW + 128]
    lane = _iota((tg, 128), 1)
    g = nega_ref[...] * _softplus(ga + dtb_ref[...])
    gall = jnp.where(lane < NH, g, _sigmoid(ga))

    same, tri, strict = _se_masks()
    eye = (_iota((SE, SE), 0) == _iota((SE, SE), 1)).astype(F32)
    gcum = _sel_dot(tri.astype(BF16), gall)
    cs = range(nc)
    rows = [slice(c * CHUNK, (c + 1) * CHUNK) for c in cs]
    gcol = [_se_col(gcum[rows[c]], 0) for c in cs]
    beta = [_se_col(gall[rows[c]], NH) for c in cs]
    glast = [_se_last(gcum, c * CHUNK + CHUNK - 1, 0) for c in cs]
    grow_all = _col_to_row(jnp.concatenate(gcol, axis=0), nc * SE)
    grow = [grow_all[:, c * SE:(c + 1) * SE] for c in cs]

    k_se = [_tile_se(k_s[rows[c], :], same) for c in cs]
    q_se = [_tile_se(q_s[rows[c], :], same) for c in cs]
    v_se = [_tile_se(v_s[rows[c], :], same) for c in cs]
    decay = [jnp.where(tri, jnp.exp(jnp.where(tri, gcol[c] - grow[c], 0.0)), 0.0) for c in cs]
    kb = [k_se[c] * beta[c] for c in cs]
    k_bf = [k_se[c].astype(BF16) for c in cs]
    kk = [lax.dot_general(kb[c].astype(BF16), k_bf[c], _NT, preferred_element_type=F32) for c in cs]
    npow = [-jnp.where(strict, kk[c] * decay[c], 0.0) for c in cs]
    tinv = [eye + npow[c] for c in cs]
    for _ in range(5):
        nb = [n.astype(BF16) for n in npow]
        npow = [jnp.dot(n, n, preferred_element_type=F32) for n in nb]
        nb2 = [n.astype(BF16) for n in npow]
        tinv = [t + jnp.dot(t.astype(BF16), n, preferred_element_type=F32) for t, n in zip(tinv, nb2)]
    egc = [jnp.exp(gcol[c]) for c in cs]
    t_bf = [t.astype(BF16) for t in tinv]
    u = [jnp.dot(t_bf[c], (v_se[c] * beta[c]).astype(BF16), preferred_element_type=F32) for c in cs]
    w = [jnp.dot(t_bf[c], (kb[c] * egc[c]).astype(BF16), preferred_element_type=F32).astype(BF16) for c in cs]
    attn = [(lax.dot_general(q_se[c].astype(BF16), k_bf[c], _NT, preferred_element_type=F32) * decay[c]).astype(BF16)
            for c in cs]
    qd = [(q_se[c] * egc[c]).astype(BF16) for c in cs]
    kd = [(k_se[c] * jnp.exp(glast[c] - gcol[c])).astype(BF16) for c in cs]

    s = s_s[...]
    for c in cs:
        sb = s.astype(BF16)
        v_new = u[c] - jnp.dot(w[c], sb, preferred_element_type=F32)
        vb = v_new.astype(BF16)
        o_se = jnp.dot(qd[c], sb, preferred_element_type=F32) + jnp.dot(attn[c], vb, preferred_element_type=F32)
        s = s * jnp.exp(glast[c]) + lax.dot_general(kd[c], vb, _TN, preferred_element_type=F32)
        o_s[rows[c], :] = _fold_se(o_se)
    s_s[...] = s
    _store_bd(st_ref, s_s)

    o = o_s[...]
    ms = _dot_sel(o * o, ones_h) * (1.0 / DH)
    out_ref[...] = (o * lax.rsqrt(ms + EPS)) * nw_ref[...] * _silu(p_ref[:, 3 * BW:4 * BW])


def _gdn(p, s0, c0, conv_w, neg_a, dt_bias, norm_w, b, tg):
    assert tg == SE
    t = p.shape[0]
    c2 = lambda bi, i: (0, 0)
    return pl.pallas_call(
        _gdn_body,
        grid=(b, t // tg),
        in_specs=[
            pl.BlockSpec((tg, W_GDN), lambda bi, i: (i, bi)),
            pl.BlockSpec((None, NH, DH, DH), lambda bi, i: (bi, 0, 0, 0)),
            pl.BlockSpec((None, CONV_W - 1, 3 * BW), lambda bi, i: (bi, 0, 0)),
            pl.BlockSpec((CONV_W, 3 * BW), c2),
            pl.BlockSpec((1, 128), c2),
            pl.BlockSpec((1, 128), c2),
            pl.BlockSpec((1, BW), c2),
        ],
        out_specs=[
            pl.BlockSpec((tg, BW), lambda bi, i: (i, bi)),
            pl.BlockSpec((None, NH, DH, DH), lambda bi, i: (bi, 0, 0, 0)),
            pl.BlockSpec((None, CONV_W - 1, 3 * BW), lambda bi, i: (bi, 0, 0)),
        ],
        out_shape=[
            jax.ShapeDtypeStruct((t, b * BW), F32),
            jax.ShapeDtypeStruct((b, NH, DH, DH), F32),
            jax.ShapeDtypeStruct((b, CONV_W - 1, 3 * BW), F32),
        ],
        scratch_shapes=[
            pltpu.VMEM((tg + 8, 3 * BW), F32),
            pltpu.VMEM((tg, BW), F32),
            pltpu.VMEM((tg, BW), F32),
            pltpu.VMEM((tg, BW), F32),
            pltpu.VMEM((tg, 128), F32),
            pltpu.VMEM((tg, BW), F32),
            pltpu.VMEM((SE, SE), F32),
        ],
        compiler_params=_cparams(("arbitrary", "arbitrary")),
        name="gdn",
    )(p, s0, c0, conv_w, neg_a, dt_bias, norm_w)


def _ml_body(p_ref, c0m_ref, n0_ref, m0_ref, c0_ref, cw_ref, bi_ref, bf_ref, out_ref, ct_m_ref, nt_ref, mt_ref, cvt_ref,
             xp_s, q_s, k_s, h_s, cm_s, n_s, m_s):
    tg = p_ref.shape[0]
    nc = tg // CHUNK
    first = pl.program_id(1) == 0

    @pl.when(first)
    def _():
        _load_bd(cm_s, c0m_ref)
        n_s[...] = n0_ref[...]
        m_s[...] = m0_ref[...]

    y, tail = _conv_rows(xp_s, p_ref[:, 0:2 * BW], c0_ref, cw_ref, tg, first)
    cvt_ref[...] = tail
    qk = _silu(y)
    q_s[...] = qk[:, 0:BW]
    k_s[...] = qk[:, BW:2 * BW] * (DH ** -0.5)
    ga = p_ref[:, 5 * BW:5 * BW + 128]
    lane = _iota((tg, 128), 1)
    gall = jnp.where(lane < NH, ga + bi_ref[...], -_softplus(-(ga + bf_ref[...])))

    same, tri, _ = _se_masks()
    bcum = _sel_dot(tri.astype(BF16), gall)
    cs = range(nc)
    rows = [slice(c * CHUNK, (c + 1) * CHUNK) for c in cs]
    icol = [_se_col(gall[rows[c]], 0) for c in cs]
    bcol = [_se_col(bcum[rows[c]], NH) for c in cs]
    blast = [_se_last(bcum, c * CHUNK + CHUNK - 1, NH) for c in cs]
    ecol = [blast[c] - bcol[c] + icol[c] for c in cs]
    emax = [jnp.broadcast_to(jnp.max(ecol[c].reshape(NH, CHUNK, 1), axis=1, keepdims=True), (NH, CHUNK, 1)).reshape(SE, 1)
            for c in cs]
    cols = [icol[c] - bcol[c] for c in cs] + blast + emax
    rows_all = _col_to_row(jnp.concatenate(cols, axis=0), 3 * nc * SE)
    rrow = [rows_all[:, c * SE:(c + 1) * SE] for c in cs]
    blast_r = [rows_all[:, (nc + c) * SE:(nc + c + 1) * SE] for c in cs]
    emax_r = [rows_all[:, (2 * nc + c) * SE:(2 * nc + c + 1) * SE] for c in cs]

    q_se = [_tile_se(q_s[rows[c], :], same) for c in cs]
    k_se = [_tile_se(k_s[rows[c], :], same) for c in cs]
    v_bf = [_tile_se(p_ref[rows[c], 2 * BW:3 * BW], same).astype(BF16) for c in cs]
    q_bf = [q.astype(BF16) for q in q_se]
    dlog = [jnp.where(tri, bcol[c] + rrow[c], -jnp.inf) for c in cs]
    dmax = [jnp.max(dlog[c], axis=-1, keepdims=True) for c in cs]
    qkm = [lax.dot_general(q_bf[c], k_se[c].astype(BF16), _NT, preferred_element_type=F32) for c in cs]

    cm = cm_s[...]
    nrow = n_s[...]
    mrow = m_s[...]
    mcol = _row_to_col(mrow, SE)
    for c in cs:
        inter = bcol[c] + mcol
        m_t = jnp.maximum(inter, dmax[c])
        s = qkm[c] * jnp.exp(dlog[c] - m_t)
        w_int = jnp.exp(inter - m_t)
        num = (w_int * jnp.dot(q_bf[c], cm.astype(BF16), preferred_element_type=F32)
               + jnp.dot(s.astype(BF16), v_bf[c], preferred_element_type=F32))
        den = w_int * jnp.sum(q_se[c] * nrow, axis=-1, keepdims=True) + jnp.sum(s, axis=-1, keepdims=True)
        h_s[rows[c], :] = _fold_se(num / jnp.maximum(jnp.abs(den), jnp.exp(-m_t)))
        m_new = jnp.maximum(blast[c] + mcol, emax[c])
        m_new_r = jnp.maximum(blast_r[c] + mrow, emax_r[c])
        wk = jnp.exp(ecol[c] - m_new) * k_se[c]
        cm = jnp.exp(blast[c] + mcol - m_new) * cm + lax.dot_general(wk.astype(BF16), v_bf[c], _TN, preferred_element_type=F32)
        nrow = jnp.exp(blast_r[c] + mrow - m_new_r) * nrow + jnp.sum(wk, axis=0, keepdims=True)
        mcol = m_new
        mrow = m_new_r
    cm_s[...] = cm
    n_s[...] = nrow
    m_s[...] = mrow
    _store_bd(ct_m_ref, cm_s)
    nt_ref[...] = nrow
    mt_ref[...] = mrow
    out_ref[...] = h_s[...] * _sigmoid(p_ref[:, 3 * BW:4 * BW]) * _silu(p_ref[:, 4 * BW:5 * BW])


def _mlstm(p, c0m, n0, m0, c0, conv_w, b_i, b_f, b, tg):
    assert tg == SE
    t = p.shape[0]
    c2 = lambda bi, i: (0, 0)
    per_b3 = lambda bi, i: (bi, 0, 0)
    per_b4 = lambda bi, i: (bi, 0, 0, 0)
    return pl.pallas_call(
        _ml_body,
        grid=(b, t // tg),
        in_specs=[
            pl.BlockSpec((tg, W_ML), lambda bi, i: (i, bi)),
            pl.BlockSpec((None, NH, DH, DH), per_b4),
            pl.BlockSpec((None, 1, BW), per_b3),
            pl.BlockSpec((None, 1, BW), per_b3),
            pl.BlockSpec((None, CONV_W - 1, 2 * BW), per_b3),
            pl.BlockSpec((CONV_W, 2 * BW), c2),
            pl.BlockSpec((1, 128), c2),
            pl.BlockSpec((1, 128), c2),
        ],
        out_specs=[
            pl.BlockSpec((tg, BW), lambda bi, i: (i, bi)),
            pl.BlockSpec((None, NH, DH, DH), per_b4),
            pl.BlockSpec((None, 1, BW), per_b3),
            pl.BlockSpec((None, 1, BW), per_b3),
            pl.BlockSpec((None, CONV_W - 1, 2 * BW), per_b3),
        ],
        out_shape=[
            jax.ShapeDtypeStruct((t, b * BW), F32),
            jax.ShapeDtypeStruct((b, NH, DH, DH), F32),
            jax.ShapeDtypeStruct((b, 1, BW), F32),
            jax.ShapeDtypeStruct((b, 1, BW), F32),
            jax.ShapeDtypeStruct((b, CONV_W - 1, 2 * BW), F32),
        ],
        scratch_shapes=[
            pltpu.VMEM((tg + 8, 2 * BW), F32),
            pltpu.VMEM((tg, BW), F32),
            pltpu.VMEM((tg, BW), F32),
            pltpu.VMEM((tg, BW), F32),
            pltpu.VMEM((SE, SE), F32),
            pltpu.VMEM((1, BW), F32),
            pltpu.VMEM((1, BW), F32),
        ],
        compiler_params=_cparams(("arbitrary", "arbitrary")),
        name="mlstm",
    )(p, c0m, n0, m0, c0, conv_w, b_i, b_f)


def _head_stack(row):
    return jnp.stack([row[:, h * DH:(h + 1) * DH] for h in range(NH)], axis=0)


def _gdn_step_body(p_ref, s_ref, c_ref, cw_ref, nega_ref, dtb_ref, nw_ref, out_ref, so_ref, co_ref,
                   q_s, k_s, v_s, eg_s, be_s):
    nb = p_ref.shape[0]
    x = p_ref[:, 0:3 * BW]
    xa = x * cw_ref[CONV_W - 1:CONV_W, :]
    for j in range(CONV_W - 1):
        xa = xa + c_ref[j] * cw_ref[j:j + 1, :]
    co_ref[0] = c_ref[1]
    co_ref[1] = c_ref[2]
    co_ref[2] = x
    qkv = _silu(xa)
    ones_h = _head_ones()
    q = qkv[:, 0:BW]
    k = qkv[:, BW:2 * BW]
    q_s[...] = q * lax.rsqrt(_dot_sel(q * q, ones_h) + EPS) * (DH ** -0.5)
    k_s[...] = k * lax.rsqrt(_dot_sel(k * k, ones_h) + EPS)
    v_s[...] = qkv[:, 2 * BW:3 * BW]
    ga = p_ref[:, 4 * BW:4 * BW + 128]
    g = nega_ref[...] * _softplus(ga + dtb_ref[...])
    eg_s[...] = jnp.exp(_dot_sel(g, _expand_sel(0)))
    be_s[...] = _dot_sel(_sigmoid(ga), _expand_sel(NH))
    nw3 = nw_ref[...].reshape(1, 1, DH)

    def one(bi, carry):
        r = pl.ds(bi, 1)
        s = s_ref[bi]
        k3 = _row_to_col(k_s[r, :], BW).reshape(NH, DH, 1)
        q3 = _row_to_col(q_s[r, :], BW).reshape(NH, DH, 1)
        eg3 = _row_to_col(eg_s[r, :], BW).reshape(NH, DH, 1)
        be3 = _row_to_col(be_s[r, :], BW).reshape(NH, DH, 1)
        eg_h = eg3[:, 0:1, :]
        be_h = be3[:, 0:1, :]
        v3 = _head_stack(v_s[r, :])
        z3 = _head_stack(p_ref[r, 3 * BW:4 * BW])
        v_new = v3 * be_h - jnp.sum(s * (k3 * be3 * eg3), axis=1, keepdims=True)
        qk = jnp.sum(q3 * k3, axis=1, keepdims=True)
        o = jnp.sum(s * (q3 * eg3), axis=1, keepdims=True) + qk * v_new
        so_ref[bi] = s * eg_h + k3 * v_new
        ms = jnp.mean(o * o, axis=-1, keepdims=True)
        out_ref[bi] = (o * lax.rsqrt(ms + EPS)) * nw3 * _silu(z3)
        return carry

    lax.fori_loop(0, nb, one, 0)


def _gdn_step(p, s0, c0, conv_w, neg_a, dt_bias, norm_w):
    nb = p.shape[0]
    return pl.pallas_call(
        _gdn_step_body,
        out_shape=[
            jax.ShapeDtypeStruct((nb, NH, 1, DH), F32),
            jax.ShapeDtypeStruct((nb, NH, DH, DH), F32),
            jax.ShapeDtypeStruct((CONV_W - 1, nb, 3 * BW), F32),
        ],
        scratch_shapes=[pltpu.VMEM((nb, BW), F32) for _ in range(5)],
        compiler_params=pltpu.CompilerParams(vmem_limit_bytes=VMEM_LIMIT_BYTES),
        name="gdn_step",
    )(p, s0, c0, conv_w, neg_a, dt_bias, norm_w)


def _ml_step_body(p_ref, c_ref, n_ref, m_ref, cv_ref, cw_ref, bi_ref, bf_ref, out_ref, co_ref, no_ref, mo_ref, cvo_ref,
                  q_s, k_s, i_s, f_s, m_s):
    nb = p_ref.shape[0]
    x = p_ref[:, 0:2 * BW]
    xa = x * cw_ref[CONV_W - 1:CONV_W, :]
    for j in range(CONV_W - 1):
        xa = xa + cv_ref[j] * cw_ref[j:j + 1, :]
    cvo_ref[0] = cv_ref[1]
    cvo_ref[1] = cv_ref[2]
    cvo_ref[2] = x
    qk = _silu(xa)
    q_s[...] = qk[:, 0:BW]
    k_s[...] = qk[:, BW:2 * BW] * (DH ** -0.5)
    ga = p_ref[:, 5 * BW:5 * BW + 128]
    i_s[...] = _dot_sel(ga + bi_ref[...], _expand_sel(0))
    f_s[...] = _dot_sel(-_softplus(-(ga + bf_ref[...])), _expand_sel(NH))
    m_s[...] = _dot_sel(m_ref[...], _expand_sel(0))

    def one(bi, carry):
        r = pl.ds(bi, 1)
        cm = c_ref[bi]
        n3 = n_ref[bi]
        qrow = q_s[r, :]
        krow = k_s[r, :]
        q3r = _head_stack(qrow)
        k3r = _head_stack(krow)
        q3 = _row_to_col(qrow, BW).reshape(NH, DH, 1)
        k3 = _row_to_col(krow, BW).reshape(NH, DH, 1)
        ip = _row_to_col(i_s[r, :], BW).reshape(NH, DH, 1)[:, 0:1, :]
        lf = _row_to_col(f_s[r, :], BW).reshape(NH, DH, 1)[:, 0:1, :]
        mm = _row_to_col(m_s[r, :], BW).reshape(NH, DH, 1)[:, 0:1, :]
        v3 = _head_stack(p_ref[r, 2 * BW:3 * BW])
        o3 = _head_stack(p_ref[r, 3 * BW:4 * BW])
        z3 = _head_stack(p_ref[r, 4 * BW:5 * BW])
        qkm = jnp.sum(q3 * k3, axis=1, keepdims=True)
        inter = lf + mm
        m_t = jnp.maximum(inter, ip)
        s = qkm * jnp.exp(ip - m_t)
        w_int = jnp.exp(inter - m_t)
        num = w_int * jnp.sum(cm * q3, axis=1, keepdims=True) + s * v3
        den = w_int * jnp.sum(q3r * n3, axis=-1, keepdims=True) + s
        hd = num / jnp.maximum(jnp.abs(den), jnp.exp(-m_t))
        dec = jnp.exp(inter - m_t)
        wgt = jnp.exp(ip - m_t)
        co_ref[bi] = dec * cm + (wgt * k3) * v3
        no_ref[bi] = dec * n3 + wgt * k3r
        mo_ref[bi] = jnp.broadcast_to(m_t, (NH, 1, 128))
        out_ref[bi] = hd * _sigmoid(o3) * _silu(z3)
        return carry

    lax.fori_loop(0, nb, one, 0)


def _ml_step(p, c0m, n0, m0, c0, conv_w, b_i, b_f):
    nb = p.shape[0]
    return pl.pallas_call(
        _ml_step_body,
        out_shape=[
            jax.ShapeDtypeStruct((nb, NH, 1, DH), F32),
            jax.ShapeDtypeStruct((nb, NH, DH, DH), F32),
            jax.ShapeDtypeStruct((nb, NH, 1, DH), F32),
            jax.ShapeDtypeStruct((nb, NH, 1, 128), F32),
            jax.ShapeDtypeStruct((CONV_W - 1, nb, 2 * BW), F32),
        ],
        scratch_shapes=[pltpu.VMEM((nb, BW), F32) for _ in range(5)],
        compiler_params=pltpu.CompilerParams(vmem_limit_bytes=VMEM_LIMIT_BYTES),
        name="mlstm_step",
    )(p, c0m, n0, m0, c0, conv_w, b_i, b_f)


def _pad_lanes(x, width):
    return jnp.pad(x, [(0, 0)] * (x.ndim - 1) + [(0, width - x.shape[-1])])


def _prep_layer(l, prm):
    w_in = prm['w_in'][l]
    seg = {}
    off = 0
    for name, width in (('lru_x', BW), ('lru_z', BW), ('gdn_q', BW), ('gdn_k', BW), ('gdn_v', BW), ('gdn_z', BW),
                        ('gdn_a', NH), ('gdn_b', NH), ('s5_u', BW), ('s5_z', BW),
                        ('ml_q', BW), ('ml_k', BW), ('ml_v', BW), ('ml_o', BW), ('ml_z', BW),
                        ('ml_i', NH), ('ml_f', NH), ('merge', NH * D_MODEL)):
        seg[name] = w_in[:, off:off + width]
        off += width
    w_mix = jnp.concatenate([
        seg['lru_x'], seg['lru_z'],
        seg['s5_u'], seg['s5_z'],
        seg['gdn_q'], seg['gdn_k'], seg['gdn_v'], seg['gdn_z'], _pad_lanes(jnp.concatenate([seg['gdn_a'], seg['gdn_b']], 1), 128),
        seg['ml_q'], seg['ml_k'], seg['ml_v'], seg['ml_o'], seg['ml_z'], _pad_lanes(jnp.concatenate([seg['ml_i'], seg['ml_f']], 1), 128),
    ], axis=1).astype(BF16)

    eye4 = jnp.eye(4, dtype=F32)
    blockdiag = lambda w: jnp.einsum('ncd,nm->ncmd', w, eye4).reshape(BW, BW)
    lru_wg = jnp.concatenate([blockdiag(prm['lru_w_a'][l]), blockdiag(prm['lru_w_x'][l])], axis=1).astype(BF16)
    lru_bg = jnp.concatenate([prm['lru_b_a'][l], prm['lru_b_x'][l]])[None, :]
    lru_sp = jax.nn.softplus(-prm['lru_lambda'][l])[None, :]

    lam_re, lam_im = prm['s5_lam_re'][l], prm['s5_lam_im'][l]
    dt = jnp.exp(prm['s5_log_dt'][l])[:, None]
    mag = jnp.exp(lam_re * dt)
    ang = lam_im * dt
    ar, ai = mag * jnp.cos(ang), mag * jnp.sin(ang)
    den = lam_re * lam_re + lam_im * lam_im
    cr = ((ar - 1.0) * lam_re + ai * lam_im) / den
    ci = (ai * lam_re - (ar - 1.0) * lam_im) / den
    b_re, b_im = prm['s5_b_re'][l], prm['s5_b_im'][l]
    bb_re = cr[..., None] * b_re - ci[..., None] * b_im
    bb_im = cr[..., None] * b_im + ci[..., None] * b_re
    eye_g = jnp.eye(S5_G, dtype=F32)
    in_dense = lambda bb: jnp.einsum('gnc,gh->gchn', bb, eye_g).reshape(BW, S5_W).astype(BF16)
    out_dense = lambda cc: jnp.einsum('gcn,gh->gnhc', cc, eye_g).reshape(S5_W, BW).astype(BF16)

    gate_row = lambda a, b: _pad_lanes(jnp.concatenate([a, b])[None, :], 128)
    zeros4 = jnp.zeros((NH,), F32)
    return dict(
        norm_w=prm['norm_w'][l][None, :],
        w_mix=w_mix,
        w_merge=seg['merge'].astype(BF16),
        w_branch=prm['w_branch'][l].astype(BF16),
        w_out=prm['w_out'][l].astype(BF16),
        lru_cw=prm['lru_conv_w'][l], lru_wg=lru_wg, lru_bg=lru_bg, lru_sp=lru_sp,
        s5_ar=ar.reshape(1, S5_W), s5_ai=ai.reshape(1, S5_W),
        s5_bre=in_dense(bb_re), s5_bim=in_dense(bb_im),
        s5_cre=out_dense(prm['s5_c_re'][l]), s5_cim=out_dense(prm['s5_c_im'][l]),
        s5_d=prm['s5_d'][l][None, :], s5_wglu=prm['s5_w_glu'][l].astype(BF16),
        gdn_cw=prm['gdn_conv_w'][l],
        gdn_nega=gate_row(-jnp.exp(prm['gdn_A_log'][l]), zeros4),
        gdn_dtb=gate_row(prm['gdn_dt_bias'][l], zeros4),
        gdn_nw=jnp.tile(prm['gdn_norm_w'][l], NH)[None, :],
        gdn_nw1=prm['gdn_norm_w'][l][None, :],
        ml_cw=prm['ml_conv_w'][l],
        ml_bi=gate_row(prm['ml_b_i'][l], zeros4),
        ml_bf=gate_row(zeros4, prm['ml_b_f'][l]),
    )


def _prompt_layer(x, pp, final_w, final, tm, tt, tg):
    b, t, _ = x.shape
    p_lru, p_s5, p_gdn, p_ml = _in_proj(x, pp['norm_w'], pp['w_mix'], tm)
    z = lambda *s: jnp.zeros(s, F32)
    out_a, lru_h, lru_c = _lru(p_lru.reshape(t, b, W_LRU), z(b, BW), z(CONV_W - 1, b, BW),
                               pp['lru_cw'], pp['lru_wg'], pp['lru_bg'], pp['lru_sp'], tt)
    out_c, s5_re, s5_im = _s5(p_s5.reshape(t, b, W_S5), z(b, S5_W), z(b, S5_W), pp['s5_ar'], pp['s5_ai'],
                              pp['s5_bre'], pp['s5_bim'], pp['s5_cre'], pp['s5_cim'], pp['s5_d'], pp['s5_wglu'], tt)
    out_b, gdn_s, gdn_c = _gdn(p_gdn, z(b, NH, DH, DH), z(b, CONV_W - 1, 3 * BW), pp['gdn_cw'], pp['gdn_nega'],
                               pp['gdn_dtb'], pp['gdn_nw'], b, tg)
    out_d, ml_c, ml_n, ml_m, ml_cv = _mlstm(p_ml, z(b, NH, DH, DH), z(b, 1, BW), z(b, 1, BW), z(b, CONV_W - 1, 2 * BW),
                                            pp['ml_cw'], pp['ml_bi'], pp['ml_bf'], b, tg)
    y = _merge(x, (out_a.reshape(t, b * BW), out_b, out_c.reshape(t, b * BW), out_d), pp['norm_w'], pp['w_merge'],
               pp['w_branch'], pp['w_out'], final_w, tm, final)
    st = dict(lru_h=lru_h, lru_conv=jnp.transpose(lru_c, (1, 0, 2)), gdn_S=gdn_s, gdn_conv=gdn_c,
              s5_re=s5_re.reshape(b, S5_G, S5_N), s5_im=s5_im.reshape(b, S5_G, S5_N),
              ml_C=ml_c, ml_n=ml_n.reshape(b, NH, DH), ml_m=ml_m.reshape(b, NH, DH)[:, :, 0], ml_conv=ml_cv)
    return y, st


def _sample_layer(x, st, pp, final_w, final):
    nb = x.shape[1]
    p_lru, p_s5, p_gdn, p_ml = _in_proj(x, pp['norm_w'], pp['w_mix'], nb)
    tmaj = lambda c: jnp.transpose(c, (1, 0, 2))
    out_a, lru_h, lru_c = _lru(p_lru.reshape(1, nb, W_LRU), st['lru_h'], tmaj(st['lru_conv']),
                               pp['lru_cw'], pp['lru_wg'], pp['lru_bg'], pp['lru_sp'], 1)
    out_c, s5_re, s5_im = _s5(p_s5.reshape(1, nb, W_S5), st['s5_re'].reshape(nb, S5_W), st['s5_im'].reshape(nb, S5_W),
                              pp['s5_ar'], pp['s5_ai'], pp['s5_bre'], pp['s5_bim'], pp['s5_cre'], pp['s5_cim'],
                              pp['s5_d'], pp['s5_wglu'], 1)
    out_b, gdn_s, gdn_c = _gdn_step(p_gdn, st['gdn_S'], tmaj(st['gdn_conv']), pp['gdn_cw'], pp['gdn_nega'],
                                    pp['gdn_dtb'], pp['gdn_nw1'])
    out_d, ml_c, ml_n, ml_m, ml_cv = _ml_step(p_ml, st['ml_C'], st['ml_n'].reshape(nb, NH, 1, DH),
                                              _pad_lanes(st['ml_m'], 128), tmaj(st['ml_conv']),
                                              pp['ml_cw'], pp['ml_bi'], pp['ml_bf'])
    y = _merge(x, (out_a.reshape(nb, BW), out_b.reshape(nb, BW), out_c.reshape(nb, BW), out_d.reshape(nb, BW)),
               pp['norm_w'], pp['w_merge'], pp['w_branch'], pp['w_out'], final_w, nb, final)
    new = dict(lru_h=lru_h, lru_conv=tmaj(lru_c), gdn_S=gdn_s, gdn_conv=tmaj(gdn_c),
               s5_re=s5_re.reshape(nb, S5_G, S5_N), s5_im=s5_im.reshape(nb, S5_G, S5_N),
               ml_C=ml_c, ml_n=ml_n.reshape(nb, NH, DH), ml_m=ml_m[:, :, 0, 0], ml_conv=tmaj(ml_cv))
    return y, new


STATE_NAMES = ('lru_h', 'lru_conv', 'gdn_S', 'gdn_conv', 's5_re', 's5_im', 'ml_C', 'ml_n', 'ml_m', 'ml_conv')


def kernel(x_prompt, x_sample, state_lru_h, state_lru_conv, state_gdn_S, state_gdn_conv, state_s5_re, state_s5_im, state_ml_C, state_ml_n, state_ml_m, state_ml_conv, norm_w, w_in, lru_conv_w, lru_w_a, lru_b_a, lru_w_x, lru_b_x, lru_lambda, gdn_conv_w, gdn_A_log, gdn_dt_bias, gdn_norm_w, s5_lam_re, s5_lam_im, s5_log_dt, s5_b_re, s5_b_im, s5_c_re, s5_c_im, s5_d, s5_w_glu, ml_conv_w, ml_b_i, ml_b_f, w_branch, w_out, final_norm_w):
    prm = dict(norm_w=norm_w, w_in=w_in, lru_conv_w=lru_conv_w, lru_w_a=lru_w_a, lru_b_a=lru_b_a, lru_w_x=lru_w_x,
               lru_b_x=lru_b_x, lru_lambda=lru_lambda, gdn_conv_w=gdn_conv_w, gdn_A_log=gdn_A_log,
               gdn_dt_bias=gdn_dt_bias, gdn_norm_w=gdn_norm_w, s5_lam_re=s5_lam_re, s5_lam_im=s5_lam_im,
               s5_log_dt=s5_log_dt, s5_b_re=s5_b_re, s5_b_im=s5_b_im, s5_c_re=s5_c_re, s5_c_im=s5_c_im, s5_d=s5_d,
               s5_w_glu=s5_w_glu, ml_conv_w=ml_conv_w, ml_b_i=ml_b_i, ml_b_f=ml_b_f, w_branch=w_branch, w_out=w_out)
    depth = w_in.shape[0]
    t = x_prompt.shape[1]
    nb = x_sample.shape[0]
    final_w = final_norm_w[None, :]
    st_in = dict(lru_h=state_lru_h, lru_conv=state_lru_conv, gdn_S=state_gdn_S, gdn_conv=state_gdn_conv,
                 s5_re=state_s5_re, s5_im=state_s5_im, ml_C=state_ml_C, ml_n=state_ml_n, ml_m=state_ml_m,
                 ml_conv=state_ml_conv)
    tm = min(512, t)
    tt = min(128, t)
    tg = min(256, t)

    xp = x_prompt
    xs = x_sample.reshape(1, nb, D_MODEL)
    new_p = {n: [] for n in STATE_NAMES}
    new_s = {n: [] for n in STATE_NAMES}
    for l in range(depth):
        pp = _prep_layer(l, prm)
        final = l == depth - 1
        xp, sp = _prompt_layer(xp, pp, final_w, final, tm, tt, tg)
        xs, ss = _sample_layer(xs, {n: st_in[n][l] for n in STATE_NAMES}, pp, final_w, final)
        for n in STATE_NAMES:
            new_p[n].append(sp[n])
            new_s[n].append(ss[n])
    np_ = {n: jnp.stack(v) for n, v in new_p.items()}
    ns_ = {n: jnp.stack(v) for n, v in new_s.items()}
    y_sample = xs.reshape(nb, 1, D_MODEL)
    return (xp, y_sample,
            np_['lru_h'], ns_['lru_h'], np_['lru_conv'], ns_['lru_conv'],
            np_['gdn_S'], ns_['gdn_S'], np_['gdn_conv'], ns_['gdn_conv'],
            np_['s5_re'], ns_['s5_re'], np_['s5_im'], ns_['s5_im'],
            np_['ml_C'], ns_['ml_C'], np_['ml_n'], ns_['ml_n'], np_['ml_m'], ns_['ml_m'],
            np_['ml_conv'], ns_['ml_conv'])
```

```python
import functools
import math

import jax
import jax.numpy as jnp
from jax import lax
from jax.experimental import pallas as pl
from jax.experimental.pallas import tpu as pltpu

F32 = jnp.float32
BF16 = jnp.bfloat16

D_MODEL = 1024
BW = 256
NH = 4
DH = 64
CONV_W = 4
CHUNK = 64
LRU_C = 8.0
S5_G = 16
S5_N = 64
S5_W = S5_G * S5_N
EPS = 1e-6

W_LRU = 2 * BW
W_S5 = 2 * BW
W_GDN = 4 * BW + 128
W_ML = 5 * BW + 128
W_MIX = (W_LRU, W_S5, W_GDN, W_ML)

VMEM_LIMIT_BYTES = 56 * 1024 * 1024


def _cparams(sem):
    return pltpu.CompilerParams(dimension_semantics=sem, vmem_limit_bytes=VMEM_LIMIT_BYTES)


def _sigmoid(x):
    return 1.0 / (1.0 + jnp.exp(-x))


def _silu(x):
    return x * _sigmoid(x)


def _softplus(x):
    return jnp.maximum(x, 0.0) + jnp.log(1.0 + jnp.exp(-jnp.abs(x)))


def _gelu_tanh(x):
    c = math.sqrt(2.0 / math.pi)
    return x * (0.5 * (1.0 + jnp.tanh(c * (x + 0.044715 * (x * x * x)))))


def _split3(x):
    hi = x.astype(BF16)
    r = x - hi.astype(F32)
    mid = r.astype(BF16)
    lo = (r - mid.astype(F32)).astype(BF16)
    return hi, mid, lo


def _split2(x):
    hi = x.astype(BF16)
    return hi, (x - hi.astype(F32)).astype(BF16)


_NN = (((1,), (0,)), ((), ()))
_NT = (((1,), (1,)), ((), ()))
_TN = (((0,), (0,)), ((), ()))


def _sel_dot(sel, x, dims=_NN):
    out = None
    for p in _split3(x):
        t = lax.dot_general(sel, p, dims, preferred_element_type=F32)
        out = t if out is None else out + t
    return out


def _dot_sel(x, sel, dims=_NN, split=_split3):
    out = None
    for p in split(x):
        t = lax.dot_general(p, sel, dims, preferred_element_type=F32)
        out = t if out is None else out + t
    return out


def _iota(shape, axis):
    return lax.broadcasted_iota(jnp.int32, shape, axis)


def _head_ones():
    return (_iota((BW, BW), 0) // DH == _iota((BW, BW), 1) // DH).astype(BF16)


def _expand_sel(offset):
    return (_iota((128, BW), 0) == _iota((128, BW), 1) // DH + offset).astype(BF16)


def _rms_rows(x, w_row):
    ms = jnp.mean(x * x, axis=-1, keepdims=True)
    return (x * lax.rsqrt(ms + EPS)) * w_row


def _inproj_body(x_ref, nw_ref, w_ref, *o_refs):
    nb, tm, _ = x_ref.shape
    xn = _rms_rows(x_ref[...].reshape(nb * tm, D_MODEL), nw_ref[...]).astype(BF16)
    off = 0
    for o_ref in o_refs:
        wd = o_ref.shape[-1] if len(o_ref.shape) == 3 else o_ref.shape[-1] // nb
        y = jnp.dot(xn, w_ref[:, off:off + wd], preferred_element_type=F32)
        if len(o_ref.shape) == 3:
            o_ref[...] = jnp.swapaxes(y.reshape(nb, tm, wd), 0, 1)
        else:
            for bi in range(nb):
                o_ref[:, bi * wd:(bi + 1) * wd] = y[bi * tm:(bi + 1) * tm]
        off += wd


def _in_proj(x, norm_w, w_mix, tm, scan_3d):
    b, t, _ = x.shape
    wtot = sum(W_MIX)
    specs, shapes = [], []
    for n, wd in enumerate(W_MIX):
        if scan_3d and n < 2:
            specs.append(pl.BlockSpec((tm, b, wd), lambda i: (i, 0, 0)))
            shapes.append(jax.ShapeDtypeStruct((t, b, wd), F32))
        else:
            specs.append(pl.BlockSpec((tm, b * wd), lambda i: (i, 0)))
            shapes.append(jax.ShapeDtypeStruct((t, b * wd), F32))
    return pl.pallas_call(
        _inproj_body,
        grid=(t // tm,),
        in_specs=[
            pl.BlockSpec((b, tm, D_MODEL), lambda i: (0, i, 0)),
            pl.BlockSpec((1, D_MODEL), lambda i: (0, 0)),
            pl.BlockSpec((D_MODEL, wtot), lambda i: (0, 0)),
        ],
        out_specs=specs,
        out_shape=shapes,
        compiler_params=_cparams(("arbitrary",)),
        name="in_proj",
    )(x, norm_w, w_mix)


def _merge_body(x_ref, a_ref, b_ref, c_ref, d_ref, nw_ref, wm_ref, wb_ref, wo_ref, fw_ref, y_ref, *, final):
    nb, tm, _ = x_ref.shape
    x = x_ref[...].reshape(nb * tm, D_MODEL)
    xn = _rms_rows(x, nw_ref[...]).astype(BF16)
    merged = None
    for n, r in enumerate((a_ref, b_ref, c_ref, d_ref)):
        if len(r.shape) == 3:
            br = jnp.swapaxes(r[...], 0, 1).reshape(nb * tm, BW)
        else:
            br = jnp.concatenate([r[:, bi * BW:(bi + 1) * BW] for bi in range(nb)], axis=0)
        gate = _sigmoid(jnp.dot(xn, wm_ref[:, n * D_MODEL:(n + 1) * D_MODEL], preferred_element_type=F32))
        proj = jnp.dot(br.astype(BF16), wb_ref[n], preferred_element_type=F32)
        merged = gate * proj if merged is None else merged + gate * proj
    y = x + jnp.dot(merged.astype(BF16), wo_ref[...], preferred_element_type=F32)
    if final:
        y = _rms_rows(y, fw_ref[...])
    y_ref[...] = y.reshape(nb, tm, D_MODEL)


def _merge(x, branches, norm_w, w_merge, w_branch, w_out, final_w, tm, final):
    b, t, _ = x.shape
    const2 = lambda i: (0, 0)
    br_specs = [pl.BlockSpec((tm,) + r.shape[1:], (lambda i: (i, 0, 0)) if r.ndim == 3 else (lambda i: (i, 0)))
                for r in branches]
    return pl.pallas_call(
        functools.partial(_merge_body, final=final),
        grid=(t // tm,),
        in_specs=[
            pl.BlockSpec((b, tm, D_MODEL), lambda i: (0, i, 0)),
            *br_specs,
            pl.BlockSpec((1, D_MODEL), const2),
            pl.BlockSpec((D_MODEL, NH * D_MODEL), const2),
            pl.BlockSpec((NH, BW, D_MODEL), lambda i: (0, 0, 0)),
            pl.BlockSpec((D_MODEL, D_MODEL), const2),
            pl.BlockSpec((1, D_MODEL), const2),
        ],
        out_specs=pl.BlockSpec((b, tm, D_MODEL), lambda i: (0, i, 0)),
        out_shape=jax.ShapeDtypeStruct((b, t, D_MODEL), F32),
        compiler_params=_cparams(("arbitrary",)),
        name="merge",
    )(x, *branches, norm_w, w_merge, w_branch, w_out, final_w)


def _lru_body(p_ref, h0_ref, c0_ref, cw_ref, wg_ref, bg_ref, sp_ref, out_ref, ht_ref, ct_ref,
              xp_s, a_s, b_s, h_s):
    tt, nb = p_ref.shape[0], p_ref.shape[1]

    @pl.when(pl.program_id(0) == 0)
    def _():
        h_s[...] = h0_ref[...]
        xp_s[0:CONV_W - 1] = c0_ref[...]

    xp_s[CONV_W - 1:CONV_W - 1 + tt] = p_ref[:, :, 0:BW]
    xa = None
    for j in range(CONV_W):
        term = xp_s[j:j + tt] * cw_ref[j:j + 1, :].reshape(1, 1, BW)
        xa = term if xa is None else xa + term
    tail = xp_s[tt:tt + CONV_W - 1]
    xp_s[0:CONV_W - 1] = tail
    ct_ref[...] = tail

    xa2 = xa.reshape(tt * nb, BW)
    gates = jnp.dot(xa2.astype(BF16), wg_ref[...], preferred_element_type=F32) + bg_ref[...]
    r = _sigmoid(gates[:, 0:BW])
    ig = _sigmoid(gates[:, BW:2 * BW])
    log_a = (-LRU_C * r) * sp_ref[...]
    a = jnp.exp(log_a)
    bb = jnp.sqrt(1.0 - jnp.exp(2.0 * log_a)) * (ig * xa2)
    a_s[...] = a.reshape(tt, nb, BW)
    b_s[...] = bb.reshape(tt, nb, BW)

    def step(t, h):
        h = a_s[t] * h + b_s[t]
        b_s[t] = h
        return h

    h = lax.fori_loop(0, tt, step, h_s[...], unroll=min(8, tt))
    h_s[...] = h
    ht_ref[...] = h
    out_ref[...] = b_s[...] * _silu(p_ref[:, :, BW:2 * BW])


def _lru(p, h0, c0, conv_w, w_gates, b_gates, sp_lam, tt):
    t, nb, _ = p.shape
    c2 = lambda i: (0, 0)
    c3 = lambda i: (0, 0, 0)
    return pl.pallas_call(
        _lru_body,
        grid=(t // tt,),
        in_specs=[
            pl.BlockSpec((tt, nb, W_LRU), lambda i: (i, 0, 0)),
            pl.BlockSpec((nb, BW), c2),
            pl.BlockSpec((CONV_W - 1, nb, BW), c3),
            pl.BlockSpec((CONV_W, BW), c2),
            pl.BlockSpec((BW, 2 * BW), c2),
            pl.BlockSpec((1, 2 * BW), c2),
            pl.BlockSpec((1, BW), c2),
        ],
        out_specs=[
            pl.BlockSpec((tt, nb, BW), lambda i: (i, 0, 0)),
            pl.BlockSpec((nb, BW), c2),
            pl.BlockSpec((CONV_W - 1, nb, BW), c3),
        ],
        out_shape=[
            jax.ShapeDtypeStruct((t, nb, BW), F32),
            jax.ShapeDtypeStruct((nb, BW), F32),
            jax.ShapeDtypeStruct((CONV_W - 1, nb, BW), F32),
        ],
        scratch_shapes=[
            pltpu.VMEM((tt + CONV_W - 1, nb, BW), F32),
            pltpu.VMEM((tt, nb, BW), F32),
            pltpu.VMEM((tt, nb, BW), F32),
            pltpu.VMEM((nb, BW), F32),
        ],
        compiler_params=_cparams(("arbitrary",)),
        name="rglru",
    )(p, h0, c0, conv_w, w_gates, b_gates, sp_lam)


def _s5_body(p_ref, hr0_ref, hi0_ref, ar_ref, ai_ref, bre_ref, bim_ref, cre_ref, cim_ref, d_ref, wglu_ref,
             out_ref, hrt_ref, hit_ref, xr_s, xi_s, hr_s, hi_s):
    tt, nb = p_ref.shape[0], p_ref.shape[1]

    @pl.when(pl.program_id(0) == 0)
    def _():
        hr_s[...] = hr0_ref[...]
        hi_s[...] = hi0_ref[...]

    u2 = p_ref[:, :, 0:BW].reshape(tt * nb, BW)
    ub = u2.astype(BF16)
    xr_s[...] = jnp.dot(ub, bre_ref[...], preferred_element_type=F32).reshape(tt, nb, S5_W)
    xi_s[...] = jnp.dot(ub, bim_ref[...], preferred_element_type=F32).reshape(tt, nb, S5_W)
    ar = jnp.broadcast_to(ar_ref[...], (nb, S5_W))
    ai = jnp.broadcast_to(ai_ref[...], (nb, S5_W))

    def step(t, c):
        hr, hi = c
        nr = ar * hr - ai * hi + xr_s[t]
        ni = ar * hi + ai * hr + xi_s[t]
        xr_s[t] = nr
        xi_s[t] = ni
        return nr, ni

    hr, hi = lax.fori_loop(0, tt, step, (hr_s[...], hi_s[...]), unroll=min(4, tt))
    hr_s[...] = hr
    hi_s[...] = hi
    hrt_ref[...] = hr
    hit_ref[...] = hi

    y = (jnp.dot(xr_s[...].reshape(tt * nb, S5_W).astype(BF16), cre_ref[...], preferred_element_type=F32)
         - jnp.dot(xi_s[...].reshape(tt * nb, S5_W).astype(BF16), cim_ref[...], preferred_element_type=F32)
         + d_ref[...] * u2)
    y = _gelu_tanh(y)
    y = y * _sigmoid(jnp.dot(y.astype(BF16), wglu_ref[...], preferred_element_type=F32))
    out_ref[...] = y.reshape(tt, nb, BW) * _silu(p_ref[:, :, BW:2 * BW])


def _s5(p, hr0, hi0, ar, ai, bre, bim, cre, cim, d, wglu, tt):
    t, nb, _ = p.shape
    c2 = lambda i: (0, 0)
    st = pl.BlockSpec((nb, S5_W), c2)
    row = pl.BlockSpec((1, S5_W), c2)
    return pl.pallas_call(
        _s5_body,
        grid=(t // tt,),
        in_specs=[
            pl.BlockSpec((tt, nb, W_S5), lambda i: (i, 0, 0)),
            st, st, row, row,
            pl.BlockSpec((BW, S5_W), c2), pl.BlockSpec((BW, S5_W), c2),
            pl.BlockSpec((S5_W, BW), c2), pl.BlockSpec((S5_W, BW), c2),
            pl.BlockSpec((1, BW), c2),
            pl.BlockSpec((BW, BW), c2),
        ],
        out_specs=[pl.BlockSpec((tt, nb, BW), lambda i: (i, 0, 0)), st, st],
        out_shape=[
            jax.ShapeDtypeStruct((t, nb, BW), F32),
            jax.ShapeDtypeStruct((nb, S5_W), F32),
            jax.ShapeDtypeStruct((nb, S5_W), F32),
        ],
        scratch_shapes=[
            pltpu.VMEM((tt, nb, S5_W), F32),
            pltpu.VMEM((tt, nb, S5_W), F32),
            pltpu.VMEM((nb, S5_W), F32),
            pltpu.VMEM((nb, S5_W), F32),
        ],
        compiler_params=_cparams(("arbitrary",)),
        name="s5",
    )(p, hr0, hi0, ar, ai, bre, bim, cre, cim, d, wglu)


def _conv_rows(xp_s, p_cols, c0_ref, cw_ref, tg, first):
    @pl.when(first)
    def _():
        xp_s[8 - (CONV_W - 1):8] = c0_ref[...]

    xp_s[8:8 + tg] = p_cols
    y = None
    for j in range(CONV_W):
        term = xp_s[8 - (CONV_W - 1) + j:8 - (CONV_W - 1) + j + tg] * cw_ref[j:j + 1, :]
        y = term if y is None else y + term
    tail = xp_s[8 + tg - (CONV_W - 1):8 + tg]
    xp_s[8 - (CONV_W - 1):8] = tail
    return y, tail


SE = NH * CHUNK


def _se_masks():
    r = _iota((SE, SE), 0)
    c = _iota((SE, SE), 1)
    same = (r // CHUNK) == (c // CHUNK)
    return same, same & ((r % CHUNK) >= (c % CHUNK))


def _cols_masks(n):
    i = _iota((n, BW), 0) % CHUNK
    j = _iota((n, BW), 1) % CHUNK
    return i >= j, i > j, i == j


def _stack_se(x_bf, same):
    return jnp.where(same, jnp.concatenate([x_bf] * NH, axis=0), jnp.zeros((), x_bf.dtype))


def _tile_rows(x):
    return jnp.concatenate([x] * NH, axis=0)


def _fold_se(x_se):
    out = x_se[0:CHUNK]
    for h in range(1, NH):
        out = out + x_se[h * CHUNK:(h + 1) * CHUNK]
    return out


def _load_cols(s_s, blocks_ref):
    for h in range(NH):
        s_s[:, h * DH:(h + 1) * DH] = blocks_ref[h]


def _store_cols(blocks_ref, s_s):
    for h in range(NH):
        blocks_ref[h] = s_s[:, h * DH:(h + 1) * DH]


def _last_rows(x, nc):
    return jnp.concatenate([jnp.broadcast_to(x[c * CHUNK + CHUNK - 1:(c + 1) * CHUNK], (CHUNK, BW)) for c in range(nc)], axis=0)


def _gdn_body(p_ref, s0_ref, c0_ref, cw_ref, nega_ref, dtb_ref, nw_ref, out_ref, st_ref, ct_ref,
              xp_s, o_s, s_s):
    tg = p_ref.shape[0]
    nc = tg // CHUNK
    first = pl.program_id(1) == 0

    @pl.when(first)
    def _():
        _load_cols(s_s, s0_ref)

    y, tail = _conv_rows(xp_s, p_ref[:, 0:3 * BW], c0_ref, cw_ref, tg, first)
    ct_ref[...] = tail
    qkv = _silu(y)
    ones_h = _head_ones()
    q = qkv[:, 0:BW]
    k = qkv[:, BW:2 * BW]
    q = q * lax.rsqrt(_dot_sel(q * q, ones_h, split=_split2) + EPS) * (DH ** -0.5)
    k = k * lax.rsqrt(_dot_sel(k * k, ones_h, split=_split2) + EPS)
    v = qkv[:, 2 * BW:3 * BW]
    ga = p_ref[:, 4 * BW:4 * BW + 128]
    lane = _iota((tg, 128), 1)
    g = nega_ref[...] * _softplus(ga + dtb_ref[...])
    gall = jnp.where(lane < NH, g, _sigmoid(ga))

    same, tri = _se_masks()
    tri_c, strict_c, _ = _cols_masks(tg)
    tri_bf = tri.astype(BF16)
    gcum = _sel_dot(tri_bf, gall)
    beta_x = _dot_sel(gall, _expand_sel(NH))
    gc_x = _dot_sel(gcum, _expand_sel(0))
    gl_x = _last_rows(gc_x, nc)
    egc = jnp.exp(gc_x)
    kb = k * beta_x
    k_b = k.astype(BF16)
    kb_b = kb.astype(BF16)
    q_b = q.astype(BF16)
    vb_b = (v * beta_x).astype(BF16)
    kbg_b = (kb * egc).astype(BF16)
    qd_b = (q * egc).astype(BF16)
    kd_b = (k * jnp.exp(gl_x - gc_x)).astype(BF16)

    cs = range(nc)
    rows = [slice(c * CHUNK, (c + 1) * CHUNK) for c in cs]
    g_x = _dot_sel(gall, _expand_sel(0))
    diff = _sel_dot(tri_bf, jnp.where(strict_c, g_x, 0.0))
    dec_all = jnp.where(tri_c, jnp.exp(diff), 0.0)
    dec_t = [_tile_rows(dec_all[rows[c]]) for c in cs]
    ndec_all = jnp.where(strict_c, -dec_all, 0.0)
    ndec_t = [_tile_rows(ndec_all[rows[c]]) for c in cs]

    k_se = [_stack_se(k_b[rows[c]], same) for c in cs]
    kk = [lax.dot_general(_stack_se(kb_b[rows[c]], same), k_se[c], _NT, preferred_element_type=F32) for c in cs]
    eye = (_iota((SE, SE), 0) == _iota((SE, SE), 1)).astype(F32)
    npow = [kk[c] * ndec_t[c] for c in cs]
    tinv = [eye + npow[c] for c in cs]
    nb = [n.astype(BF16) for n in npow]
    for _ in range(5):
        npow = [jnp.dot(n, n, preferred_element_type=F32) for n in nb]
        nb = [n.astype(BF16) for n in npow]
        tinv = [t + jnp.dot(t.astype(BF16), n, preferred_element_type=F32) for t, n in zip(tinv, nb)]
    t_bf = [t.astype(BF16) for t in tinv]
    u = [jnp.dot(t_bf[c], _stack_se(vb_b[rows[c]], same), preferred_element_type=F32) for c in cs]
    w = [jnp.dot(t_bf[c], _stack_se(kbg_b[rows[c]], same), preferred_element_type=F32).astype(BF16) for c in cs]
    attn = [(lax.dot_general(_stack_se(q_b[rows[c]], same), k_se[c], _NT, preferred_element_type=F32) * dec_t[c]).astype(BF16)
            for c in cs]

    s = s_s[...]
    for c in cs:
        sb = _stack_se(s.astype(BF16), same)
        v_new = u[c] - jnp.dot(w[c], sb, preferred_element_type=F32)
        vb = v_new.astype(BF16)
        o_se = (jnp.dot(_stack_se(qd_b[rows[c]], same), sb, preferred_element_type=F32)
                + jnp.dot(attn[c], vb, preferred_element_type=F32))
        upd = lax.dot_general(_stack_se(kd_b[rows[c]], same), vb, _TN, preferred_element_type=F32)
        s = s * jnp.exp(gl_x[c * CHUNK:c * CHUNK + 1]) + _fold_se(upd)
        o_s[rows[c], :] = _fold_se(o_se)
    s_s[...] = s
    _store_cols(st_ref, s_s)

    o = o_s[...]
    ms = _dot_sel(o * o, ones_h, split=_split2) * (1.0 / DH)
    out_ref[...] = (o * lax.rsqrt(ms + EPS)) * nw_ref[...] * _silu(p_ref[:, 3 * BW:4 * BW])


def _gdn(p, s0, c0, conv_w, neg_a, dt_bias, norm_w, b, tg):
    assert tg == SE
    t = p.shape[0]
    c2 = lambda bi, i: (0, 0)
    return pl.pallas_call(
        _gdn_body,
        grid=(b, t // tg),
        in_specs=[
            pl.BlockSpec((tg, W_GDN), lambda bi, i: (i, bi)),
            pl.BlockSpec((None, NH, DH, DH), lambda bi, i: (bi, 0, 0, 0)),
            pl.BlockSpec((None, CONV_W - 1, 3 * BW), lambda bi, i: (bi, 0, 0)),
            pl.BlockSpec((CONV_W, 3 * BW), c2),
            pl.BlockSpec((1, 128), c2),
            pl.BlockSpec((1, 128), c2),
            pl.BlockSpec((1, BW), c2),
        ],
        out_specs=[
            pl.BlockSpec((tg, BW), lambda bi, i: (i, bi)),
            pl.BlockSpec((None, NH, DH, DH), lambda bi, i: (bi, 0, 0, 0)),
            pl.BlockSpec((None, CONV_W - 1, 3 * BW), lambda bi, i: (bi, 0, 0)),
        ],
        out_shape=[
            jax.ShapeDtypeStruct((t, b * BW), F32),
            jax.ShapeDtypeStruct((b, NH, DH, DH), F32),
            jax.ShapeDtypeStruct((b, CONV_W - 1, 3 * BW), F32),
        ],
        scratch_shapes=[
            pltpu.VMEM((tg + 8, 3 * BW), F32),
            pltpu.VMEM((tg, BW), F32),
            pltpu.VMEM((DH, BW), F32),
        ],
        compiler_params=_cparams(("arbitrary", "arbitrary")),
        name="gdn",
    )(p, s0, c0, conv_w, neg_a, dt_bias, norm_w)


def _cummax_rows(x):
    row = _iota(x.shape, 0)
    sh = 1
    while sh < x.shape[0]:
        x = jnp.maximum(x, jnp.where(row >= sh, pltpu.roll(x, sh, 0), -jnp.inf))
        sh *= 2
    return x


def _ml_body(p_ref, c0m_ref, n0_ref, m0_ref, c0_ref, cw_ref, bi_ref, bf_ref, out_ref, ct_m_ref, nt_ref, mt_ref, cvt_ref,
             xp_s, h_s, cm_s):
    tg = p_ref.shape[0]
    nc = tg // CHUNK
    first = pl.program_id(1) == 0

    @pl.when(first)
    def _():
        _load_cols(cm_s, c0m_ref)
        nt_ref[...] = n0_ref[...]
        mt_ref[...] = m0_ref[...]

    y, tail = _conv_rows(xp_s, p_ref[:, 0:2 * BW], c0_ref, cw_ref, tg, first)
    cvt_ref[...] = tail
    qk = _silu(y)
    q = qk[:, 0:BW]
    k = qk[:, BW:2 * BW] * (DH ** -0.5)
    ga = p_ref[:, 5 * BW:5 * BW + 128]
    lane = _iota((tg, 128), 1)
    gall = jnp.where(lane < NH, ga + bi_ref[...], -_softplus(-(ga + bf_ref[...])))

    same, tri = _se_masks()
    tri_c, strict_c, eye_c = _cols_masks(tg)
    tri_bf = tri.astype(BF16)
    ones_h = _head_ones()
    bcum = _sel_dot(tri_bf, gall)
    i_x = _dot_sel(gall, _expand_sel(0))
    lf_x = _dot_sel(gall, _expand_sel(NH))
    b_x = _dot_sel(bcum, _expand_sel(NH))
    r_x = i_x - b_x
    e_x = _last_rows(b_x, nc) - b_x + i_x
    q_b = q.astype(BF16)
    k_b = k.astype(BF16)
    v_b = p_ref[:, 2 * BW:3 * BW].astype(BF16)

    cs = range(nc)
    rows = [slice(c * CHUNK, (c + 1) * CHUNK) for c in cs]
    dl_all = (_sel_dot(tri_bf, jnp.where(strict_c, lf_x, 0.0)) + _sel_dot(same.astype(BF16), jnp.where(eye_c, i_x, 0.0)))
    dl_all = jnp.where(tri_c, dl_all, -jnp.inf)
    dlog = [dl_all[rows[c]] for c in cs]
    dmax = [b_x[rows[c]] + _cummax_rows(r_x[rows[c]]) for c in cs]
    q_se = [_stack_se(q_b[rows[c]], same) for c in cs]
    v_se = [_stack_se(v_b[rows[c]], same) for c in cs]
    qkm = [_fold_se(lax.dot_general(q_se[c], _stack_se(k_b[rows[c]], same), _NT, preferred_element_type=F32)) for c in cs]

    cm = cm_s[...]
    nrow = nt_ref[...]
    mrow = mt_ref[...]
    for c in cs:
        bx = b_x[rows[c]]
        inter = bx + mrow
        m_t = jnp.maximum(inter, dmax[c])
        s = qkm[c] * jnp.exp(dlog[c] - m_t)
        w_int = jnp.exp(inter - m_t)
        num = (w_int * _fold_se(jnp.dot(q_se[c], _stack_se(cm.astype(BF16), same), preferred_element_type=F32))
               + _fold_se(jnp.dot(_stack_se(s.astype(BF16), same), v_se[c], preferred_element_type=F32)))
        den = w_int * _dot_sel(q[rows[c]] * nrow, ones_h) + _dot_sel(s, ones_h)
        h_s[rows[c], :] = num / jnp.maximum(jnp.abs(den), jnp.exp(-m_t))
        blast = bx[CHUNK - 1:CHUNK]
        ex = e_x[rows[c]]
        m_new = jnp.maximum(blast + mrow, jnp.max(ex, axis=0, keepdims=True))
        dec = jnp.exp(blast + mrow - m_new)
        wk = jnp.exp(ex - m_new) * k[rows[c]]
        cm = dec * cm + _fold_se(lax.dot_general(_stack_se(wk.astype(BF16), same), v_se[c], _TN, preferred_element_type=F32))
        nrow = dec * nrow + jnp.sum(wk, axis=0, keepdims=True)
        mrow = m_new
    cm_s[...] = cm
    _store_cols(ct_m_ref, cm_s)
    nt_ref[...] = nrow
    mt_ref[...] = mrow
    out_ref[...] = h_s[...] * _sigmoid(p_ref[:, 3 * BW:4 * BW]) * _silu(p_ref[:, 4 * BW:5 * BW])


def _mlstm(p, c0m, n0, m0, c0, conv_w, b_i, b_f, b, tg):
    assert tg == SE
    t = p.shape[0]
    c2 = lambda bi, i: (0, 0)
    per_b3 = lambda bi, i: (bi, 0, 0)
    per_b4 = lambda bi, i: (bi, 0, 0, 0)
    return pl.pallas_call(
        _ml_body,
        grid=(b, t // tg),
        in_specs=[
            pl.BlockSpec((tg, W_ML), lambda bi, i: (i, bi)),
            pl.BlockSpec((None, NH, DH, DH), per_b4),
            pl.BlockSpec((None, 1, BW), per_b3),
            pl.BlockSpec((None, 1, BW), per_b3),
            pl.BlockSpec((None, CONV_W - 1, 2 * BW), per_b3),
            pl.BlockSpec((CONV_W, 2 * BW), c2),
            pl.BlockSpec((1, 128), c2),
            pl.BlockSpec((1, 128), c2),
        ],
        out_specs=[
            pl.BlockSpec((tg, BW), lambda bi, i: (i, bi)),
            pl.BlockSpec((None, NH, DH, DH), per_b4),
            pl.BlockSpec((None, 1, BW), per_b3),
            pl.BlockSpec((None, 1, BW), per_b3),
            pl.BlockSpec((None, CONV_W - 1, 2 * BW), per_b3),
        ],
        out_shape=[
            jax.ShapeDtypeStruct((t, b * BW), F32),
            jax.ShapeDtypeStruct((b, NH, DH, DH), F32),
            jax.ShapeDtypeStruct((b, 1, BW), F32),
            jax.ShapeDtypeStruct((b, 1, BW), F32),
            jax.ShapeDtypeStruct((b, CONV_W - 1, 2 * BW), F32),
        ],
        scratch_shapes=[
            pltpu.VMEM((tg + 8, 2 * BW), F32),
            pltpu.VMEM((tg, BW), F32),
            pltpu.VMEM((DH, BW), F32),
        ],
        compiler_params=_cparams(("arbitrary", "arbitrary")),
        name="mlstm",
    )(p, c0m, n0, m0, c0, conv_w, b_i, b_f)


STEP_TILE = 8


def _head_stack(row):
    return jnp.stack([row[:, h * DH:(h + 1) * DH] for h in range(NH)], axis=0)


def _transpose_rows(x):
    eye = (_iota((BW, BW), 0) == _iota((BW, BW), 1)).astype(BF16)
    return _sel_dot(eye, x, _NT)


def _gdn_step_body(p_ref, s_ref, c_ref, cw_ref, nega_ref, dtb_ref, nw_ref, out_ref, so_ref, co_ref, rows_s):
    nb = p_ref.shape[0]
    x = p_ref[:, 0:3 * BW]
    xa = x * cw_ref[CONV_W - 1:CONV_W, :]
    for j in range(CONV_W - 1):
        xa = xa + c_ref[j] * cw_ref[j:j + 1, :]
    co_ref[0] = c_ref[1]
    co_ref[1] = c_ref[2]
    co_ref[2] = x
    qkv = _silu(xa)
    ones_h = _head_ones()
    q = qkv[:, 0:BW]
    k = qkv[:, BW:2 * BW]
    q = q * lax.rsqrt(_dot_sel(q * q, ones_h) + EPS) * (DH ** -0.5)
    k = k * lax.rsqrt(_dot_sel(k * k, ones_h) + EPS)
    v = qkv[:, 2 * BW:3 * BW]
    z = p_ref[:, 3 * BW:4 * BW]
    ga = p_ref[:, 4 * BW:4 * BW + 128]
    g = nega_ref[...] * _softplus(ga + dtb_ref[...])
    eg = jnp.exp(_dot_sel(g, _expand_sel(0)))
    be = _dot_sel(_sigmoid(ga), _expand_sel(NH))
    qk = _dot_sel(q * k, ones_h)
    rows_s[0] = k
    rows_s[1] = q * eg
    rows_s[2] = k * be * eg
    rows_s[3] = eg
    rows_s[4] = v * be
    rows_s[5] = qk
    rows_s[6] = _silu(z)
    nw3 = nw_ref[...].reshape(1, 1, DH)

    def tile(t, carry):
        t0 = pl.multiple_of(t * STEP_TILE, STEP_TILE)
        rr = pl.ds(t0, STEP_TILE)
        kt, qet, wt, egt = (_transpose_rows(rows_s[n, rr, :]) for n in range(4))
        vb, qkx, zs = (rows_s[n, rr, :] for n in range(4, 7))
        for j in range(STEP_TILE):
            r = slice(j, j + 1)
            col = lambda m: m[:, r].reshape(NH, DH, 1)
            s = s_ref[t0 + j]
            v_new = _head_stack(vb[r, :]) - jnp.sum(s * col(wt), axis=1, keepdims=True)
            o = jnp.sum(s * col(qet), axis=1, keepdims=True) + _head_stack(qkx[r, :]) * v_new
            so_ref[t0 + j] = s * col(egt) + col(kt) * v_new
            ms = jnp.mean(o * o, axis=-1, keepdims=True)
            out_ref[t0 + j] = (o * lax.rsqrt(ms + EPS)) * nw3 * _head_stack(zs[r, :])
        return carry

    lax.fori_loop(0, nb // STEP_TILE, tile, 0)


def _gdn_step(p, s0, c0, conv_w, neg_a, dt_bias, norm_w):
    nb = p.shape[0]
    return pl.pallas_call(
        _gdn_step_body,
        out_shape=[
            jax.ShapeDtypeStruct((nb, NH, 1, DH), F32),
            jax.ShapeDtypeStruct((nb, NH, DH, DH), F32),
            jax.ShapeDtypeStruct((CONV_W - 1, nb, 3 * BW), F32),
        ],
        scratch_shapes=[pltpu.VMEM((7, nb, BW), F32)],
        compiler_params=pltpu.CompilerParams(vmem_limit_bytes=VMEM_LIMIT_BYTES),
        name="gdn_step",
    )(p, s0, c0, conv_w, neg_a, dt_bias, norm_w)


def _ml_step_body(p_ref, c_ref, n_ref, m_ref, cv_ref, cw_ref, bi_ref, bf_ref, out_ref, co_ref, no_ref, mo_ref, cvo_ref,
                  rows_s):
    nb = p_ref.shape[0]
    x = p_ref[:, 0:2 * BW]
    xa = x * cw_ref[CONV_W - 1:CONV_W, :]
    for j in range(CONV_W - 1):
        xa = xa + cv_ref[j] * cw_ref[j:j + 1, :]
    cvo_ref[0] = cv_ref[1]
    cvo_ref[1] = cv_ref[2]
    cvo_ref[2] = x
    qk = _silu(xa)
    q = qk[:, 0:BW]
    k = qk[:, BW:2 * BW] * (DH ** -0.5)
    v = p_ref[:, 2 * BW:3 * BW]
    ga = p_ref[:, 5 * BW:5 * BW + 128]
    ones_h = _head_ones()
    ip = _dot_sel(ga + bi_ref[...], _expand_sel(0))
    lf = _dot_sel(-_softplus(-(ga + bf_ref[...])), _expand_sel(NH))
    mm = _dot_sel(m_ref[...], _expand_sel(0))
    inter = lf + mm
    m_t = jnp.maximum(inter, ip)
    w_int = jnp.exp(inter - m_t)
    wgt = jnp.exp(ip - m_t)
    s = _dot_sel(q * k, ones_h) * wgt
    gate = _sigmoid(p_ref[:, 3 * BW:4 * BW]) * _silu(p_ref[:, 4 * BW:5 * BW])
    sv = s * v
    wk = wgt * k
    floor = jnp.exp(-m_t)
    mo_ref[...] = m_t
    for n, a in enumerate((q * w_int, w_int, wk, q, sv, s, floor, v, gate)):
        rows_s[n] = a

    def tile(t, carry):
        t0 = pl.multiple_of(t * STEP_TILE, STEP_TILE)
        rr = pl.ds(t0, STEP_TILE)
        qwt, dect, wkt = (_transpose_rows(rows_s[n, rr, :]) for n in range(3))
        wi, wkr, qr, svr, sr, fl, vr, gt = (rows_s[n, rr, :] for n in range(1, 9))
        for j in range(STEP_TILE):
            r = slice(j, j + 1)
            col = lambda m: m[:, r].reshape(NH, DH, 1)
            cm = c_ref[t0 + j]
            n3 = n_ref[t0 + j]
            num = jnp.sum(cm * col(qwt), axis=1, keepdims=True) + _head_stack(svr[r, :])
            qn = jnp.sum(_head_stack(qr[r, :]) * n3, axis=-1, keepdims=True)
            den = _head_stack(wi[r, :])[:, :, 0:1] * qn + _head_stack(sr[r, :])[:, :, 0:1]
            hd = num / jnp.maximum(jnp.abs(den), _head_stack(fl[r, :])[:, :, 0:1])
            co_ref[t0 + j] = col(dect) * cm + col(wkt) * _head_stack(vr[r, :])
            no_ref[t0 + j] = _head_stack(wi[r, :]) * n3 + _head_stack(wkr[r, :])
            out_ref[t0 + j] = hd * _head_stack(gt[r, :])
        return carry

    lax.fori_loop(0, nb // STEP_TILE, tile, 0)


def _ml_step(p, c0m, n0, m0, c0, conv_w, b_i, b_f):
    nb = p.shape[0]
    return pl.pallas_call(
        _ml_step_body,
        out_shape=[
            jax.ShapeDtypeStruct((nb, NH, 1, DH), F32),
            jax.ShapeDtypeStruct((nb, NH, DH, DH), F32),
            jax.ShapeDtypeStruct((nb, NH, 1, DH), F32),
            jax.ShapeDtypeStruct((nb, BW), F32),
            jax.ShapeDtypeStruct((CONV_W - 1, nb, 2 * BW), F32),
        ],
        scratch_shapes=[pltpu.VMEM((9, nb, BW), F32)],
        compiler_params=pltpu.CompilerParams(vmem_limit_bytes=VMEM_LIMIT_BYTES),
        name="mlstm_step",
    )(p, c0m, n0, m0, c0, conv_w, b_i, b_f)


def _pad_lanes(x, width):
    return jnp.pad(x, [(0, 0)] * (x.ndim - 1) + [(0, width - x.shape[-1])])


def _prep_layer(l, prm):
    w_in = prm['w_in'][l]
    seg = {}
    off = 0
    for name, width in (('lru_x', BW), ('lru_z', BW), ('gdn_q', BW), ('gdn_k', BW), ('gdn_v', BW), ('gdn_z', BW),
                        ('gdn_a', NH), ('gdn_b', NH), ('s5_u', BW), ('s5_z', BW),
                        ('ml_q', BW), ('ml_k', BW), ('ml_v', BW), ('ml_o', BW), ('ml_z', BW),
                        ('ml_i', NH), ('ml_f', NH), ('merge', NH * D_MODEL)):
        seg[name] = w_in[:, off:off + width]
        off += width
    w_mix = jnp.concatenate([
        seg['lru_x'], seg['lru_z'],
        seg['s5_u'], seg['s5_z'],
        seg['gdn_q'], seg['gdn_k'], seg['gdn_v'], seg['gdn_z'], _pad_lanes(jnp.concatenate([seg['gdn_a'], seg['gdn_b']], 1), 128),
        seg['ml_q'], seg['ml_k'], seg['ml_v'], seg['ml_o'], seg['ml_z'], _pad_lanes(jnp.concatenate([seg['ml_i'], seg['ml_f']], 1), 128),
    ], axis=1).astype(BF16)

    eye4 = jnp.eye(4, dtype=F32)
    blockdiag = lambda w: jnp.einsum('ncd,nm->ncmd', w, eye4).reshape(BW, BW)
    lru_wg = jnp.concatenate([blockdiag(prm['lru_w_a'][l]), blockdiag(prm['lru_w_x'][l])], axis=1).astype(BF16)
    lru_bg = jnp.concatenate([prm['lru_b_a'][l], prm['lru_b_x'][l]])[None, :]
    lru_sp = jax.nn.softplus(-prm['lru_lambda'][l])[None, :]

    lam_re, lam_im = prm['s5_lam_re'][l], prm['s5_lam_im'][l]
    dt = jnp.exp(prm['s5_log_dt'][l])[:, None]
    mag = jnp.exp(lam_re * dt)
    ang = lam_im * dt
    ar, ai = mag * jnp.cos(ang), mag * jnp.sin(ang)
    den = lam_re * lam_re + lam_im * lam_im
    cr = ((ar - 1.0) * lam_re + ai * lam_im) / den
    ci = (ai * lam_re - (ar - 1.0) * lam_im) / den
    b_re, b_im = prm['s5_b_re'][l], prm['s5_b_im'][l]
    bb_re = cr[..., None] * b_re - ci[..., None] * b_im
    bb_im = cr[..., None] * b_im + ci[..., None] * b_re
    eye_g = jnp.eye(S5_G, dtype=F32)
    in_dense = lambda bb: jnp.einsum('gnc,gh->gchn', bb, eye_g).reshape(BW, S5_W).astype(BF16)
    out_dense = lambda cc: jnp.einsum('gcn,gh->gnhc', cc, eye_g).reshape(S5_W, BW).astype(BF16)

    gate_row = lambda a, b: _pad_lanes(jnp.concatenate([a, b])[None, :], 128)
    zeros4 = jnp.zeros((NH,), F32)
    return dict(
        norm_w=prm['norm_w'][l][None, :],
        w_mix=w_mix,
        w_merge=seg['merge'].astype(BF16),
        w_branch=prm['w_branch'][l].astype(BF16),
        w_out=prm['w_out'][l].astype(BF16),
        lru_cw=prm['lru_conv_w'][l], lru_wg=lru_wg, lru_bg=lru_bg, lru_sp=lru_sp,
        s5_ar=ar.reshape(1, S5_W), s5_ai=ai.reshape(1, S5_W),
        s5_bre=in_dense(bb_re), s5_bim=in_dense(bb_im),
        s5_cre=out_dense(prm['s5_c_re'][l]), s5_cim=out_dense(prm['s5_c_im'][l]),
        s5_d=prm['s5_d'][l][None, :], s5_wglu=prm['s5_w_glu'][l].astype(BF16),
        gdn_cw=prm['gdn_conv_w'][l],
        gdn_nega=gate_row(-jnp.exp(prm['gdn_A_log'][l]), zeros4),
        gdn_dtb=gate_row(prm['gdn_dt_bias'][l], zeros4),
        gdn_nw=jnp.tile(prm['gdn_norm_w'][l], NH)[None, :],
        gdn_nw1=prm['gdn_norm_w'][l][None, :],
        ml_cw=prm['ml_conv_w'][l],
        ml_bi=gate_row(prm['ml_b_i'][l], zeros4),
        ml_bf=gate_row(zeros4, prm['ml_b_f'][l]),
    )


def _prompt_layer(x, pp, final_w, final, tm, tt, tg):
    b, t, _ = x.shape
    p_lru, p_s5, p_gdn, p_ml = _in_proj(x, pp['norm_w'], pp['w_mix'], tm, True)
    z = lambda *s: jnp.zeros(s, F32)
    out_a, lru_h, lru_c = _lru(p_lru, z(b, BW), z(CONV_W - 1, b, BW),
                               pp['lru_cw'], pp['lru_wg'], pp['lru_bg'], pp['lru_sp'], tt)
    out_c, s5_re, s5_im = _s5(p_s5, z(b, S5_W), z(b, S5_W), pp['s5_ar'], pp['s5_ai'],
                              pp['s5_bre'], pp['s5_bim'], pp['s5_cre'], pp['s5_cim'], pp['s5_d'], pp['s5_wglu'], tt)
    out_b, gdn_s, gdn_c = _gdn(p_gdn, z(b, NH, DH, DH), z(b, CONV_W - 1, 3 * BW), pp['gdn_cw'], pp['gdn_nega'],
                               pp['gdn_dtb'], pp['gdn_nw'], b, tg)
    out_d, ml_c, ml_n, ml_m, ml_cv = _mlstm(p_ml, z(b, NH, DH, DH), z(b, 1, BW), z(b, 1, BW), z(b, CONV_W - 1, 2 * BW),
                                            pp['ml_cw'], pp['ml_bi'], pp['ml_bf'], b, tg)
    y = _merge(x, (out_a, out_b, out_c, out_d), pp['norm_w'], pp['w_merge'],
               pp['w_branch'], pp['w_out'], final_w, tm, final)
    st = dict(lru_h=lru_h, lru_conv=jnp.transpose(lru_c, (1, 0, 2)), gdn_S=gdn_s, gdn_conv=gdn_c,
              s5_re=s5_re.reshape(b, S5_G, S5_N), s5_im=s5_im.reshape(b, S5_G, S5_N),
              ml_C=ml_c, ml_n=ml_n.reshape(b, NH, DH), ml_m=ml_m.reshape(b, NH, DH)[:, :, 0], ml_conv=ml_cv)
    return y, st


def _sample_layer(x, st, pp, final_w, final):
    nb = x.shape[1]
    p_lru, p_s5, p_gdn, p_ml = _in_proj(x, pp['norm_w'], pp['w_mix'], nb, False)
    tmaj = lambda c: jnp.transpose(c, (1, 0, 2))
    out_a, lru_h, lru_c = _lru(p_lru.reshape(1, nb, W_LRU), st['lru_h'], tmaj(st['lru_conv']),
                               pp['lru_cw'], pp['lru_wg'], pp['lru_bg'], pp['lru_sp'], 1)
    out_c, s5_re, s5_im = _s5(p_s5.reshape(1, nb, W_S5), st['s5_re'].reshape(nb, S5_W), st['s5_im'].reshape(nb, S5_W),
                              pp['s5_ar'], pp['s5_ai'], pp['s5_bre'], pp['s5_bim'], pp['s5_cre'], pp['s5_cim'],
                              pp['s5_d'], pp['s5_wglu'], 1)
    out_b, gdn_s, gdn_c = _gdn_step(p_gdn, st['gdn_S'], tmaj(st['gdn_conv']), pp['gdn_cw'], pp['gdn_nega'],
                                    pp['gdn_dtb'], pp['gdn_nw1'])
    out_d, ml_c, ml_n, ml_m, ml_cv = _ml_step(p_ml, st['ml_C'], st['ml_n'].reshape(nb, NH, 1, DH),
                                              _pad_lanes(st['ml_m'], 128), tmaj(st['ml_conv']),
                                              pp['ml_cw'], pp['ml_bi'], pp['ml_bf'])
    y = _merge(x, (out_a.reshape(nb, BW), out_b.reshape(nb, BW), out_c.reshape(nb, BW), out_d.reshape(nb, BW)),
               pp['norm_w'], pp['w_merge'], pp['w_branch'], pp['w_out'], final_w, nb, final)
    new = dict(lru_h=lru_h, lru_conv=tmaj(lru_c), gdn_S=gdn_s, gdn_conv=tmaj(gdn_c),
               s5_re=s5_re.reshape(nb, S5_G, S5_N), s5_im=s5_im.reshape(nb, S5_G, S5_N),
               ml_C=ml_c, ml_n=ml_n.reshape(nb, NH, DH), ml_m=ml_m.reshape(nb, NH, DH)[:, :, 0], ml_conv=tmaj(ml_cv))
    return y, new


STATE_NAMES = ('lru_h', 'lru_conv', 'gdn_S', 'gdn_conv', 's5_re', 's5_im', 'ml_C', 'ml_n', 'ml_m', 'ml_conv')

TM_ROWS = 64
TT_SCAN = 128


def kernel(x_prompt, x_sample, state_lru_h, state_lru_conv, state_gdn_S, state_gdn_conv, state_s5_re, state_s5_im, state_ml_C, state_ml_n, state_ml_m, state_ml_conv, norm_w, w_in, lru_conv_w, lru_w_a, lru_b_a, lru_w_x, lru_b_x, lru_lambda, gdn_conv_w, gdn_A_log, gdn_dt_bias, gdn_norm_w, s5_lam_re, s5_lam_im, s5_log_dt, s5_b_re, s5_b_im, s5_c_re, s5_c_im, s5_d, s5_w_glu, ml_conv_w, ml_b_i, ml_b_f, w_branch, w_out, final_norm_w):
    prm = dict(norm_w=norm_w, w_in=w_in, lru_conv_w=lru_conv_w, lru_w_a=lru_w_a, lru_b_a=lru_b_a, lru_w_x=lru_w_x,
               lru_b_x=lru_b_x, lru_lambda=lru_lambda, gdn_conv_w=gdn_conv_w, gdn_A_log=gdn_A_log,
               gdn_dt_bias=gdn_dt_bias, gdn_norm_w=gdn_norm_w, s5_lam_re=s5_lam_re, s5_lam_im=s5_lam_im,
               s5_log_dt=s5_log_dt, s5_b_re=s5_b_re, s5_b_im=s5_b_im, s5_c_re=s5_c_re, s5_c_im=s5_c_im, s5_d=s5_d,
               s5_w_glu=s5_w_glu, ml_conv_w=ml_conv_w, ml_b_i=ml_b_i, ml_b_f=ml_b_f, w_branch=w_branch, w_out=w_out)
    depth = w_in.shape[0]
    t = x_prompt.shape[1]
    nb = x_sample.shape[0]
    final_w = final_norm_w[None, :]
    st_in = dict(lru_h=state_lru_h, lru_conv=state_lru_conv, gdn_S=state_gdn_S, gdn_conv=state_gdn_conv,
                 s5_re=state_s5_re, s5_im=state_s5_im, ml_C=state_ml_C, ml_n=state_ml_n, ml_m=state_ml_m,
                 ml_conv=state_ml_conv)
    tm = min(TM_ROWS, t)
    tt = min(TT_SCAN, t)
    tg = min(SE, t)

    xp = x_prompt
    xs = x_sample.reshape(1, nb, D_MODEL)
    new_p = {n: [] for n in STATE_NAMES}
    new_s = {n: [] for n in STATE_NAMES}
    for l in range(depth):
        pp = _prep_layer(l, prm)
        final = l == depth - 1
        xp, sp = _prompt_layer(xp, pp, final_w, final, tm, tt, tg)
        xs, ss = _sample_layer(xs, {n: st_in[n][l] for n in STATE_NAMES}, pp, final_w, final)
        for n in STATE_NAMES:
            new_p[n].append(sp[n])
            new_s[n].append(ss[n])
    np_ = {n: jnp.stack(v) for n, v in new_p.items()}
    ns_ = {n: jnp.stack(v) for n, v in new_s.items()}
    y_sample = xs.reshape(nb, 1, D_MODEL)
    return (xp, y_sample,
            np_['lru_h'], ns_['lru_h'], np_['lru_conv'], ns_['lru_conv'],
            np_['gdn_S'], ns_['gdn_S'], np_['gdn_conv'], ns_['gdn_conv'],
            np_['s5_re'], ns_['s5_re'], np_['s5_im'], ns_['s5_im'],
            np_['ml_C'], ns_['ml_C'], np_['ml_n'], ns_['ml_n'], np_['ml_m'], ns_['ml_m'],
            np_['ml_conv'], ns_['ml_conv'])
```

```python
import functools
import math

import jax
import jax.numpy as jnp
from jax import lax
from jax.experimental import pallas as pl
from jax.experimental.pallas import tpu as pltpu

F32 = jnp.float32
BF16 = jnp.bfloat16

D_MODEL = 1024
BW = 256
NH = 4
DH = 64
CONV_W = 4
CHUNK = 64
LRU_C = 8.0
S5_G = 16
S5_N = 64
S5_W = S5_G * S5_N
EPS = 1e-6

W_LRU = 2 * BW
W_S5 = 2 * BW
W_GDN = 4 * BW + 128
W_ML = 5 * BW + 128
W_MIX = (W_LRU, W_S5, W_GDN, W_ML)

VMEM_LIMIT_BYTES = 56 * 1024 * 1024


def _cparams(sem):
    return pltpu.CompilerParams(dimension_semantics=sem, vmem_limit_bytes=VMEM_LIMIT_BYTES)


def _sigmoid(x):
    return 1.0 / (1.0 + jnp.exp(-x))


def _silu(x):
    return x * _sigmoid(x)


def _softplus(x):
    return jnp.maximum(x, 0.0) + jnp.log(1.0 + jnp.exp(-jnp.abs(x)))


def _gelu_tanh(x):
    c = math.sqrt(2.0 / math.pi)
    return x * (0.5 * (1.0 + jnp.tanh(c * (x + 0.044715 * (x * x * x)))))


def _split3(x):
    hi = x.astype(BF16)
    r = x - hi.astype(F32)
    mid = r.astype(BF16)
    lo = (r - mid.astype(F32)).astype(BF16)
    return hi, mid, lo


def _split2(x):
    hi = x.astype(BF16)
    return hi, (x - hi.astype(F32)).astype(BF16)


_NN = (((1,), (0,)), ((), ()))
_NT = (((1,), (1,)), ((), ()))
_TN = (((0,), (0,)), ((), ()))


def _sel_dot(sel, x, dims=_NN):
    out = None
    for p in _split3(x):
        t = lax.dot_general(sel, p, dims, preferred_element_type=F32)
        out = t if out is None else out + t
    return out


def _dot_sel(x, sel, dims=_NN, split=_split3):
    out = None
    for p in split(x):
        t = lax.dot_general(p, sel, dims, preferred_element_type=F32)
        out = t if out is None else out + t
    return out


def _iota(shape, axis):
    return lax.broadcasted_iota(jnp.int32, shape, axis)


def _head_ones():
    return (_iota((BW, BW), 0) // DH == _iota((BW, BW), 1) // DH).astype(BF16)


def _expand_sel(offset):
    return (_iota((128, BW), 0) == _iota((128, BW), 1) // DH + offset).astype(BF16)


def _rms_rows(x, w_row):
    ms = jnp.mean(x * x, axis=-1, keepdims=True)
    return (x * lax.rsqrt(ms + EPS)) * w_row


def _inproj_body(x_ref, nw_ref, w_ref, *o_refs):
    nb, tm, _ = x_ref.shape
    xn = _rms_rows(x_ref[...].reshape(nb * tm, D_MODEL), nw_ref[...]).astype(BF16)
    off = 0
    for o_ref in o_refs:
        wd = o_ref.shape[-1] if len(o_ref.shape) == 3 else o_ref.shape[-1] // nb
        y = jnp.dot(xn, w_ref[:, off:off + wd], preferred_element_type=F32)
        if len(o_ref.shape) == 3:
            o_ref[...] = jnp.swapaxes(y.reshape(nb, tm, wd), 0, 1)
        else:
            for bi in range(nb):
                o_ref[:, bi * wd:(bi + 1) * wd] = y[bi * tm:(bi + 1) * tm]
        off += wd


def _in_proj(x, norm_w, w_mix, tm, scan_3d):
    b, t, _ = x.shape
    wtot = sum(W_MIX)
    specs, shapes = [], []
    for n, wd in enumerate(W_MIX):
        if scan_3d and n < 2:
            specs.append(pl.BlockSpec((tm, b, wd), lambda i: (i, 0, 0)))
            shapes.append(jax.ShapeDtypeStruct((t, b, wd), F32))
        else:
            specs.append(pl.BlockSpec((tm, b * wd), lambda i: (i, 0)))
            shapes.append(jax.ShapeDtypeStruct((t, b * wd), F32))
    return pl.pallas_call(
        _inproj_body,
        grid=(t // tm,),
        in_specs=[
            pl.BlockSpec((b, tm, D_MODEL), lambda i: (0, i, 0)),
            pl.BlockSpec((1, D_MODEL), lambda i: (0, 0)),
            pl.BlockSpec((D_MODEL, wtot), lambda i: (0, 0)),
        ],
        out_specs=specs,
        out_shape=shapes,
        compiler_params=_cparams(("arbitrary",)),
        name="in_proj",
    )(x, norm_w, w_mix)


def _merge_body(x_ref, a_ref, b_ref, c_ref, d_ref, nw_ref, wm_ref, wb_ref, wo_ref, fw_ref, y_ref, *, final):
    nb, tm, _ = x_ref.shape
    x = x_ref[...].reshape(nb * tm, D_MODEL)
    xn = _rms_rows(x, nw_ref[...]).astype(BF16)
    merged = None
    for n, r in enumerate((a_ref, b_ref, c_ref, d_ref)):
        if len(r.shape) == 3:
            br = jnp.swapaxes(r[...], 0, 1).reshape(nb * tm, BW)
        else:
            br = jnp.concatenate([r[:, bi * BW:(bi + 1) * BW] for bi in range(nb)], axis=0)
        gate = _sigmoid(jnp.dot(xn, wm_ref[:, n * D_MODEL:(n + 1) * D_MODEL], preferred_element_type=F32))
        proj = jnp.dot(br.astype(BF16), wb_ref[n], preferred_element_type=F32)
        merged = gate * proj if merged is None else merged + gate * proj
    y = x + jnp.dot(merged.astype(BF16), wo_ref[...], preferred_element_type=F32)
    if final:
        y = _rms_rows(y, fw_ref[...])
    y_ref[...] = y.reshape(nb, tm, D_MODEL)


def _merge(x, branches, norm_w, w_merge, w_branch, w_out, final_w, tm, final):
    b, t, _ = x.shape
    const2 = lambda i: (0, 0)
    br_specs = [pl.BlockSpec((tm,) + r.shape[1:], (lambda i: (i, 0, 0)) if r.ndim == 3 else (lambda i: (i, 0)))
                for r in branches]
    return pl.pallas_call(
        functools.partial(_merge_body, final=final),
        grid=(t // tm,),
        in_specs=[
            pl.BlockSpec((b, tm, D_MODEL), lambda i: (0, i, 0)),
            *br_specs,
            pl.BlockSpec((1, D_MODEL), const2),
            pl.BlockSpec((D_MODEL, NH * D_MODEL), const2),
            pl.BlockSpec((NH, BW, D_MODEL), lambda i: (0, 0, 0)),
            pl.BlockSpec((D_MODEL, D_MODEL), const2),
            pl.BlockSpec((1, D_MODEL), const2),
        ],
        out_specs=pl.BlockSpec((b, tm, D_MODEL), lambda i: (0, i, 0)),
        out_shape=jax.ShapeDtypeStruct((b, t, D_MODEL), F32),
        compiler_params=_cparams(("arbitrary",)),
        name="merge",
    )(x, *branches, norm_w, w_merge, w_branch, w_out, final_w)


def _lru_body(p_ref, h0_ref, c0_ref, cw_ref, wg_ref, bg_ref, sp_ref, out_ref, ht_ref, ct_ref,
              xp_s, a_s, b_s, h_s):
    tt, nb = p_ref.shape[0], p_ref.shape[1]

    @pl.when(pl.program_id(0) == 0)
    def _():
        h_s[...] = h0_ref[...]
        xp_s[0:CONV_W - 1] = c0_ref[...]

    xp_s[CONV_W - 1:CONV_W - 1 + tt] = p_ref[:, :, 0:BW]
    xa = None
    for j in range(CONV_W):
        term = xp_s[j:j + tt] * cw_ref[j:j + 1, :].reshape(1, 1, BW)
        xa = term if xa is None else xa + term
    tail = xp_s[tt:tt + CONV_W - 1]
    xp_s[0:CONV_W - 1] = tail
    ct_ref[...] = tail

    xa2 = xa.reshape(tt * nb, BW)
    gates = jnp.dot(xa2.astype(BF16), wg_ref[...], preferred_element_type=F32) + bg_ref[...]
    r = _sigmoid(gates[:, 0:BW])
    ig = _sigmoid(gates[:, BW:2 * BW])
    log_a = (-LRU_C * r) * sp_ref[...]
    a = jnp.exp(log_a)
    bb = jnp.sqrt(1.0 - jnp.exp(2.0 * log_a)) * (ig * xa2)
    a_s[...] = a.reshape(tt, nb, BW)
    b_s[...] = bb.reshape(tt, nb, BW)

    def step(t, h):
        h = a_s[t] * h + b_s[t]
        b_s[t] = h
        return h

    h = lax.fori_loop(0, tt, step, h_s[...], unroll=min(8, tt))
    h_s[...] = h
    ht_ref[...] = h
    out_ref[...] = b_s[...] * _silu(p_ref[:, :, BW:2 * BW])


def _lru(p, h0, c0, conv_w, w_gates, b_gates, sp_lam, tt):
    t, nb, _ = p.shape
    c2 = lambda i: (0, 0)
    c3 = lambda i: (0, 0, 0)
    return pl.pallas_call(
        _lru_body,
        grid=(t // tt,),
        in_specs=[
            pl.BlockSpec((tt, nb, W_LRU), lambda i: (i, 0, 0)),
            pl.BlockSpec((nb, BW), c2),
            pl.BlockSpec((CONV_W - 1, nb, BW), c3),
            pl.BlockSpec((CONV_W, BW), c2),
            pl.BlockSpec((BW, 2 * BW), c2),
            pl.BlockSpec((1, 2 * BW), c2),
            pl.BlockSpec((1, BW), c2),
        ],
        out_specs=[
            pl.BlockSpec((tt, nb, BW), lambda i: (i, 0, 0)),
            pl.BlockSpec((nb, BW), c2),
            pl.BlockSpec((CONV_W - 1, nb, BW), c3),
        ],
        out_shape=[
            jax.ShapeDtypeStruct((t, nb, BW), F32),
            jax.ShapeDtypeStruct((nb, BW), F32),
            jax.ShapeDtypeStruct((CONV_W - 1, nb, BW), F32),
        ],
        scratch_shapes=[
            pltpu.VMEM((tt + CONV_W - 1, nb, BW), F32),
            pltpu.VMEM((tt, nb, BW), F32),
            pltpu.VMEM((tt, nb, BW), F32),
            pltpu.VMEM((nb, BW), F32),
        ],
        compiler_params=_cparams(("arbitrary",)),
        name="rglru",
    )(p, h0, c0, conv_w, w_gates, b_gates, sp_lam)


def _s5_body(p_ref, hr0_ref, hi0_ref, ar_ref, ai_ref, bre_ref, bim_ref, cre_ref, cim_ref, d_ref, wglu_ref,
             out_ref, hrt_ref, hit_ref, xr_s, xi_s, hr_s, hi_s):
    tt, nb = p_ref.shape[0], p_ref.shape[1]

    @pl.when(pl.program_id(0) == 0)
    def _():
        hr_s[...] = hr0_ref[...]
        hi_s[...] = hi0_ref[...]

    u2 = p_ref[:, :, 0:BW].reshape(tt * nb, BW)
    ub = u2.astype(BF16)
    xr_s[...] = jnp.dot(ub, bre_ref[...], preferred_element_type=F32).reshape(tt, nb, S5_W)
    xi_s[...] = jnp.dot(ub, bim_ref[...], preferred_element_type=F32).reshape(tt, nb, S5_W)
    ar = jnp.broadcast_to(ar_ref[...], (nb, S5_W))
    ai = jnp.broadcast_to(ai_ref[...], (nb, S5_W))

    def step(t, c):
        hr, hi = c
        nr = ar * hr - ai * hi + xr_s[t]
        ni = ar * hi + ai * hr + xi_s[t]
        xr_s[t] = nr
        xi_s[t] = ni
        return nr, ni

    hr, hi = lax.fori_loop(0, tt, step, (hr_s[...], hi_s[...]), unroll=min(4, tt))
    hr_s[...] = hr
    hi_s[...] = hi
    hrt_ref[...] = hr
    hit_ref[...] = hi

    y = (jnp.dot(xr_s[...].reshape(tt * nb, S5_W).astype(BF16), cre_ref[...], preferred_element_type=F32)
         - jnp.dot(xi_s[...].reshape(tt * nb, S5_W).astype(BF16), cim_ref[...], preferred_element_type=F32)
         + d_ref[...] * u2)
    y = _gelu_tanh(y)
    y = y * _sigmoid(jnp.dot(y.astype(BF16), wglu_ref[...], preferred_element_type=F32))
    out_ref[...] = y.reshape(tt, nb, BW) * _silu(p_ref[:, :, BW:2 * BW])


def _s5(p, hr0, hi0, ar, ai, bre, bim, cre, cim, d, wglu, tt):
    t, nb, _ = p.shape
    c2 = lambda i: (0, 0)
    st = pl.BlockSpec((nb, S5_W), c2)
    row = pl.BlockSpec((1, S5_W), c2)
    return pl.pallas_call(
        _s5_body,
        grid=(t // tt,),
        in_specs=[
            pl.BlockSpec((tt, nb, W_S5), lambda i: (i, 0, 0)),
            st, st, row, row,
            pl.BlockSpec((BW, S5_W), c2), pl.BlockSpec((BW, S5_W), c2),
            pl.BlockSpec((S5_W, BW), c2), pl.BlockSpec((S5_W, BW), c2),
            pl.BlockSpec((1, BW), c2),
            pl.BlockSpec((BW, BW), c2),
        ],
        out_specs=[pl.BlockSpec((tt, nb, BW), lambda i: (i, 0, 0)), st, st],
        out_shape=[
            jax.ShapeDtypeStruct((t, nb, BW), F32),
            jax.ShapeDtypeStruct((nb, S5_W), F32),
            jax.ShapeDtypeStruct((nb, S5_W), F32),
        ],
        scratch_shapes=[
            pltpu.VMEM((tt, nb, S5_W), F32),
            pltpu.VMEM((tt, nb, S5_W), F32),
            pltpu.VMEM((nb, S5_W), F32),
            pltpu.VMEM((nb, S5_W), F32),
        ],
        compiler_params=_cparams(("arbitrary",)),
        name="s5",
    )(p, hr0, hi0, ar, ai, bre, bim, cre, cim, d, wglu)


def _conv_rows(xp_s, p_cols, cw_ref, tg):
    xp_s[8:8 + tg] = p_cols
    y = None
    for j in range(CONV_W):
        term = xp_s[8 - (CONV_W - 1) + j:8 - (CONV_W - 1) + j + tg] * cw_ref[j:j + 1, :]
        y = term if y is None else y + term
    tail = xp_s[8 + tg - (CONV_W - 1):8 + tg]
    xp_s[8 - (CONV_W - 1):8] = tail
    return y, tail


SE = NH * CHUNK


def _se_masks():
    r = _iota((SE, SE), 0)
    c = _iota((SE, SE), 1)
    same = (r // CHUNK) == (c // CHUNK)
    return same, same & ((r % CHUNK) >= (c % CHUNK))


def _cols_masks(n):
    i = _iota((n, BW), 0) % CHUNK
    j = _iota((n, BW), 1) % CHUNK
    return i >= j, i > j, i == j


def _stack_se(x_bf, same):
    return jnp.where(same, jnp.concatenate([x_bf] * NH, axis=0), jnp.zeros((), x_bf.dtype))


def _tile_rows(x):
    return jnp.concatenate([x] * NH, axis=0)


def _fold_se(x_se):
    out = x_se[0:CHUNK]
    for h in range(1, NH):
        out = out + x_se[h * CHUNK:(h + 1) * CHUNK]
    return out


def _load_cols(s_s, blocks_ref):
    for h in range(NH):
        s_s[:, h * DH:(h + 1) * DH] = blocks_ref[h]


def _store_cols(blocks_ref, s_s):
    for h in range(NH):
        blocks_ref[h] = s_s[:, h * DH:(h + 1) * DH]


def _last_rows(x, nc):
    return jnp.concatenate([jnp.broadcast_to(x[c * CHUNK + CHUNK - 1:(c + 1) * CHUNK], (CHUNK, BW)) for c in range(nc)], axis=0)


def _gdn_init(s0_ref, c0_ref, xp_s, s_s):
    _load_cols(s_s, s0_ref)
    xp_s[8 - (CONV_W - 1):8] = c0_ref[...]


def _gdn_main(p_ref, cw_ref, nega_ref, dtb_ref, nw_ref, out_ref, st_ref, ct_ref, xp_s, o_s, s_s):
    tg = p_ref.shape[0]
    nc = tg // CHUNK
    y, tail = _conv_rows(xp_s, p_ref[:, 0:3 * BW], cw_ref, tg)
    ct_ref[...] = tail
    qkv = _silu(y)
    ones_h = _head_ones()
    q = qkv[:, 0:BW]
    k = qkv[:, BW:2 * BW]
    q = q * lax.rsqrt(_dot_sel(q * q, ones_h, split=_split2) + EPS) * (DH ** -0.5)
    k = k * lax.rsqrt(_dot_sel(k * k, ones_h, split=_split2) + EPS)
    v = qkv[:, 2 * BW:3 * BW]
    yield
    ga = p_ref[:, 4 * BW:4 * BW + 128]
    lane = _iota((tg, 128), 1)
    g = nega_ref[...] * _softplus(ga + dtb_ref[...])
    gall = jnp.where(lane < NH, g, _sigmoid(ga))

    same, tri = _se_masks()
    tri_c, strict_c, _ = _cols_masks(tg)
    tri_bf = tri.astype(BF16)
    gcum = _sel_dot(tri_bf, gall)
    beta_x = _dot_sel(gall, _expand_sel(NH))
    gc_x = _dot_sel(gcum, _expand_sel(0))
    gl_x = _last_rows(gc_x, nc)
    egc = jnp.exp(gc_x)
    kb = k * beta_x
    k_b = k.astype(BF16)
    kb_b = kb.astype(BF16)
    q_b = q.astype(BF16)
    vb_b = (v * beta_x).astype(BF16)
    kbg_b = (kb * egc).astype(BF16)
    qd_b = (q * egc).astype(BF16)
    kd_b = (k * jnp.exp(gl_x - gc_x)).astype(BF16)
    yield

    cs = range(nc)
    rows = [slice(c * CHUNK, (c + 1) * CHUNK) for c in cs]
    g_x = _dot_sel(gall, _expand_sel(0))
    diff = _sel_dot(tri_bf, jnp.where(strict_c, g_x, 0.0))
    dec_all = jnp.where(tri_c, jnp.exp(diff), 0.0)
    dec_t = [_tile_rows(dec_all[rows[c]]) for c in cs]
    ndec_all = jnp.where(strict_c, -dec_all, 0.0)
    ndec_t = [_tile_rows(ndec_all[rows[c]]) for c in cs]

    dot = functools.partial(jnp.dot, preferred_element_type=F32)
    cat = functools.partial(jnp.concatenate, axis=0)
    k_se = [_stack_se(k_b[rows[c]], same) for c in cs]
    kq = [lax.dot_general(cat([_stack_se(kb_b[rows[c]], same), _stack_se(q_b[rows[c]], same)]), k_se[c], _NT,
                          preferred_element_type=F32) for c in cs]
    attn = [(kq[c][SE:] * dec_t[c]).astype(BF16) for c in cs]
    eye = (_iota((SE, SE), 0) == _iota((SE, SE), 1)).astype(F32)
    npow = [kq[c][:SE] * ndec_t[c] for c in cs]
    tinv = [eye + npow[c] for c in cs]
    nb = [n.astype(BF16) for n in npow]
    yield
    nb = [dot(n, n).astype(BF16) for n in nb]
    for _ in range(4):
        yield
        both = [dot(cat([t.astype(BF16), n]), n) for t, n in zip(tinv, nb)]
        tinv = [t + b2[:SE] for t, b2 in zip(tinv, both)]
        nb = [b2[SE:].astype(BF16) for b2 in both]
    yield
    tinv = [t + dot(t.astype(BF16), n) for t, n in zip(tinv, nb)]
    t_bf = [t.astype(BF16) for t in tinv]
    u = [dot(t_bf[c], _stack_se(vb_b[rows[c]], same)) for c in cs]
    w = [dot(t_bf[c], _stack_se(kbg_b[rows[c]], same)).astype(BF16) for c in cs]

    kd_t = [_stack_se(kd_b[rows[c]], same).T for c in cs]
    s = s_s[...]
    yield _READY
    for c in cs:
        sb = _stack_se(s.astype(BF16), same)
        ws = dot(cat([w[c], _stack_se(qd_b[rows[c]], same)]), sb)
        vb = (u[c] - ws[:SE]).astype(BF16)
        av = dot(cat([attn[c], kd_t[c]]), vb)
        s = s * jnp.exp(gl_x[c * CHUNK:c * CHUNK + 1]) + _fold_se(av[SE:])
        o_s[rows[c], :] = _fold_se(ws[SE:] + av[:SE])
        yield
    s_s[...] = s
    _store_cols(st_ref, s_s)

    o = o_s[...]
    ms = _dot_sel(o * o, ones_h, split=_split2) * (1.0 / DH)
    out_ref[...] = (o * lax.rsqrt(ms + EPS)) * nw_ref[...] * _silu(p_ref[:, 3 * BW:4 * BW])


def _cummax_rows(x):
    row = _iota(x.shape, 0)
    sh = 1
    while sh < x.shape[0]:
        x = jnp.maximum(x, jnp.where(row >= sh, pltpu.roll(x, sh, 0), -jnp.inf))
        sh *= 2
    return x


def _ml_init(c0m_ref, n0_ref, m0_ref, c0_ref, nt_ref, mt_ref, xp_s, cm_s):
    _load_cols(cm_s, c0m_ref)
    nt_ref[...] = n0_ref[...]
    mt_ref[...] = m0_ref[...]
    xp_s[8 - (CONV_W - 1):8] = c0_ref[...]


def _ml_main(p_ref, cw_ref, bi_ref, bf_ref, out_ref, ct_m_ref, nt_ref, mt_ref, cvt_ref, xp_s, h_s, cm_s):
    tg = p_ref.shape[0]
    nc = tg // CHUNK
    y, tail = _conv_rows(xp_s, p_ref[:, 0:2 * BW], cw_ref, tg)
    cvt_ref[...] = tail
    qk = _silu(y)
    q = qk[:, 0:BW]
    k = qk[:, BW:2 * BW] * (DH ** -0.5)
    yield
    ga = p_ref[:, 5 * BW:5 * BW + 128]
    lane = _iota((tg, 128), 1)
    gall = jnp.where(lane < NH, ga + bi_ref[...], -_softplus(-(ga + bf_ref[...])))

    same, tri = _se_masks()
    tri_c, strict_c, eye_c = _cols_masks(tg)
    tri_bf = tri.astype(BF16)
    ones_h = _head_ones()
    bcum = _sel_dot(tri_bf, gall)
    i_x = _dot_sel(gall, _expand_sel(0))
    lf_x = _dot_sel(gall, _expand_sel(NH))
    b_x = _dot_sel(bcum, _expand_sel(NH))
    r_x = i_x - b_x
    e_x = _last_rows(b_x, nc) - b_x + i_x
    q_b = q.astype(BF16)
    k_b = k.astype(BF16)
    v_b = p_ref[:, 2 * BW:3 * BW].astype(BF16)
    yield

    cs = range(nc)
    rows = [slice(c * CHUNK, (c + 1) * CHUNK) for c in cs]
    dl_all = (_sel_dot(tri_bf, jnp.where(strict_c, lf_x, 0.0)) + _sel_dot(same.astype(BF16), jnp.where(eye_c, i_x, 0.0)))
    dl_all = jnp.where(tri_c, dl_all, -jnp.inf)
    dlog = [dl_all[rows[c]] for c in cs]
    yield
    dmax = [b_x[rows[c]] + _cummax_rows(r_x[rows[c]]) for c in cs]
    yield
    q_se = [_stack_se(q_b[rows[c]], same) for c in cs]
    v_se = [_stack_se(v_b[rows[c]], same) for c in cs]
    qkm = [_fold_se(lax.dot_general(q_se[c], _stack_se(k_b[rows[c]], same), _NT, preferred_element_type=F32)) for c in cs]

    cm = cm_s[...]
    nrow = nt_ref[...]
    mrow = mt_ref[...]
    yield _READY
    for c in cs:
        bx = b_x[rows[c]]
        inter = bx + mrow
        m_t = jnp.maximum(inter, dmax[c])
        s = qkm[c] * jnp.exp(dlog[c] - m_t)
        w_int = jnp.exp(inter - m_t)
        num = (w_int * _fold_se(jnp.dot(q_se[c], _stack_se(cm.astype(BF16), same), preferred_element_type=F32))
               + _fold_se(jnp.dot(_stack_se(s.astype(BF16), same), v_se[c], preferred_element_type=F32)))
        den = w_int * _dot_sel(q[rows[c]] * nrow, ones_h) + _dot_sel(s, ones_h)
        h_s[rows[c], :] = num / jnp.maximum(jnp.abs(den), jnp.exp(-m_t))
        blast = bx[CHUNK - 1:CHUNK]
        ex = e_x[rows[c]]
        m_new = jnp.maximum(blast + mrow, jnp.max(ex, axis=0, keepdims=True))
        dec = jnp.exp(blast + mrow - m_new)
        wk = jnp.exp(ex - m_new) * k[rows[c]]
        cm = dec * cm + _fold_se(lax.dot_general(_stack_se(wk.astype(BF16), same), v_se[c], _TN, preferred_element_type=F32))
        nrow = dec * nrow + jnp.sum(wk, axis=0, keepdims=True)
        mrow = m_new
        yield
    cm_s[...] = cm
    _store_cols(ct_m_ref, cm_s)
    nt_ref[...] = nrow
    mt_ref[...] = mrow
    out_ref[...] = h_s[...] * _sigmoid(p_ref[:, 3 * BW:4 * BW]) * _silu(p_ref[:, 4 * BW:5 * BW])


_READY, _DONE = object(), object()


def _chunked_body(pg_ref, s0_ref, gc0_ref, gcw_ref, nega_ref, dtb_ref, gnw_ref,
                  pm_ref, c0m_ref, n0_ref, m0_ref, mc0_ref, mcw_ref, bi_ref, bf_ref,
                  og_ref, st_ref, gct_ref, om_ref, ctm_ref, nt_ref, mt_ref, mct_ref,
                  gxp_s, go_s, gs_s, mxp_s, mh_s, mcm_s):
    @pl.when(pl.program_id(1) == 0)
    def _():
        _gdn_init(s0_ref, gc0_ref, gxp_s, gs_s)
        _ml_init(c0m_ref, n0_ref, m0_ref, mc0_ref, nt_ref, mt_ref, mxp_s, mcm_s)

    live = [_gdn_main(pg_ref, gcw_ref, nega_ref, dtb_ref, gnw_ref, og_ref, st_ref, gct_ref, gxp_s, go_s, gs_s),
            _ml_main(pm_ref, mcw_ref, bi_ref, bf_ref, om_ref, ctm_ref, nt_ref, mt_ref, mct_ref, mxp_s, mh_s, mcm_s)]
    pre = list(live)
    while pre:
        pre = [g for g in pre if next(g) is not _READY]
    while live:
        live = [g for g in live if next(g, _DONE) is not _DONE]


def _chunked(pg, s0, gc0, gcw, neg_a, dt_bias, gnw, pm, c0m, n0, m0, mc0, mcw, b_i, b_f, b, tg):
    assert tg == SE
    t = pg.shape[0]
    c2 = lambda bi, i: (0, 0)
    per_b3 = lambda bi, i: (bi, 0, 0)
    per_b4 = lambda bi, i: (bi, 0, 0, 0)
    tile = lambda wd: pl.BlockSpec((tg, wd), lambda bi, i: (i, bi))
    st4 = pl.BlockSpec((None, NH, DH, DH), per_b4)
    row = pl.BlockSpec((None, 1, BW), per_b3)
    gconv = pl.BlockSpec((None, CONV_W - 1, 3 * BW), per_b3)
    mconv = pl.BlockSpec((None, CONV_W - 1, 2 * BW), per_b3)
    gate = pl.BlockSpec((1, 128), c2)
    return pl.pallas_call(
        _chunked_body,
        grid=(b, t // tg),
        in_specs=[
            tile(W_GDN), st4, gconv, pl.BlockSpec((CONV_W, 3 * BW), c2), gate, gate, pl.BlockSpec((1, BW), c2),
            tile(W_ML), st4, row, row, mconv, pl.BlockSpec((CONV_W, 2 * BW), c2), gate, gate,
        ],
        out_specs=[tile(BW), st4, gconv, tile(BW), st4, row, row, mconv],
        out_shape=[
            jax.ShapeDtypeStruct((t, b * BW), F32),
            jax.ShapeDtypeStruct((b, NH, DH, DH), F32),
            jax.ShapeDtypeStruct((b, CONV_W - 1, 3 * BW), F32),
            jax.ShapeDtypeStruct((t, b * BW), F32),
            jax.ShapeDtypeStruct((b, NH, DH, DH), F32),
            jax.ShapeDtypeStruct((b, 1, BW), F32),
            jax.ShapeDtypeStruct((b, 1, BW), F32),
            jax.ShapeDtypeStruct((b, CONV_W - 1, 2 * BW), F32),
        ],
        scratch_shapes=[
            pltpu.VMEM((tg + 8, 3 * BW), F32), pltpu.VMEM((tg, BW), F32), pltpu.VMEM((DH, BW), F32),
            pltpu.VMEM((tg + 8, 2 * BW), F32), pltpu.VMEM((tg, BW), F32), pltpu.VMEM((DH, BW), F32),
        ],
        compiler_params=_cparams(("arbitrary", "arbitrary")),
        name="chunked",
    )(pg, s0, gc0, gcw, neg_a, dt_bias, gnw, pm, c0m, n0, m0, mc0, mcw, b_i, b_f)


STEP_TILE = 8


def _head_stack(row):
    return jnp.stack([row[:, h * DH:(h + 1) * DH] for h in range(NH)], axis=0)


def _transpose_rows(x):
    eye = (_iota((BW, BW), 0) == _iota((BW, BW), 1)).astype(BF16)
    return _sel_dot(eye, x, _NT)


def _gdn_step_body(p_ref, s_ref, c_ref, cw_ref, nega_ref, dtb_ref, nw_ref, out_ref, so_ref, co_ref, rows_s):
    nb = p_ref.shape[0]
    x = p_ref[:, 0:3 * BW]
    xa = x * cw_ref[CONV_W - 1:CONV_W, :]
    for j in range(CONV_W - 1):
        xa = xa + c_ref[j] * cw_ref[j:j + 1, :]
    co_ref[0] = c_ref[1]
    co_ref[1] = c_ref[2]
    co_ref[2] = x
    qkv = _silu(xa)
    ones_h = _head_ones()
    q = qkv[:, 0:BW]
    k = qkv[:, BW:2 * BW]
    q = q * lax.rsqrt(_dot_sel(q * q, ones_h) + EPS) * (DH ** -0.5)
    k = k * lax.rsqrt(_dot_sel(k * k, ones_h) + EPS)
    v = qkv[:, 2 * BW:3 * BW]
    z = p_ref[:, 3 * BW:4 * BW]
    ga = p_ref[:, 4 * BW:4 * BW + 128]
    g = nega_ref[...] * _softplus(ga + dtb_ref[...])
    eg = jnp.exp(_dot_sel(g, _expand_sel(0)))
    be = _dot_sel(_sigmoid(ga), _expand_sel(NH))
    qk = _dot_sel(q * k, ones_h)
    rows_s[0] = k
    rows_s[1] = q * eg
    rows_s[2] = k * be * eg
    rows_s[3] = eg
    rows_s[4] = v * be
    rows_s[5] = qk
    rows_s[6] = _silu(z)
    nw3 = nw_ref[...].reshape(1, 1, DH)

    def tile(t, carry):
        t0 = pl.multiple_of(t * STEP_TILE, STEP_TILE)
        rr = pl.ds(t0, STEP_TILE)
        kt, qet, wt, egt = (_transpose_rows(rows_s[n, rr, :]) for n in range(4))
        vb, qkx, zs = (rows_s[n, rr, :] for n in range(4, 7))
        for j in range(STEP_TILE):
            r = slice(j, j + 1)
            col = lambda m: m[:, r].reshape(NH, DH, 1)
            s = s_ref[t0 + j]
            v_new = _head_stack(vb[r, :]) - jnp.sum(s * col(wt), axis=1, keepdims=True)
            o = jnp.sum(s * col(qet), axis=1, keepdims=True) + _head_stack(qkx[r, :]) * v_new
            so_ref[t0 + j] = s * col(egt) + col(kt) * v_new
            ms = jnp.mean(o * o, axis=-1, keepdims=True)
            out_ref[t0 + j] = (o * lax.rsqrt(ms + EPS)) * nw3 * _head_stack(zs[r, :])
        return carry

    lax.fori_loop(0, nb // STEP_TILE, tile, 0)


def _full(shape):
    nd = len(shape)
    return pl.BlockSpec(shape, lambda i: (0,) * nd)


def _layer_block(nb, l):
    return pl.BlockSpec((None, nb, NH, DH, DH), lambda i: (l, 0, 0, 0, 0))


def _gdn_step(p, s_all, s_acc, l, c0, conv_w, neg_a, dt_bias, norm_w):
    nb = p.shape[0]
    ins = (p, s_all, c0, conv_w, neg_a, dt_bias, norm_w)
    in_specs = [_full(a.shape) for a in ins]
    in_specs[1] = _layer_block(nb, l)
    return pl.pallas_call(
        lambda *refs: _gdn_step_body(*refs[:7], *refs[8:]),
        grid=(1,),
        in_specs=in_specs + [pl.BlockSpec(memory_space=pl.ANY)],
        out_specs=[_full((nb, NH, 1, DH)), _layer_block(nb, l), _full((CONV_W - 1, nb, 3 * BW))],
        out_shape=[
            jax.ShapeDtypeStruct((nb, NH, 1, DH), F32),
            jax.ShapeDtypeStruct(s_acc.shape, F32),
            jax.ShapeDtypeStruct((CONV_W - 1, nb, 3 * BW), F32),
        ],
        input_output_aliases={7: 1},
        scratch_shapes=[pltpu.VMEM((7, nb, BW), F32)],
        compiler_params=_cparams(("arbitrary",)),
        name="gdn_step",
    )(*ins, s_acc)


def _ml_step_body(p_ref, c_ref, n_ref, m_ref, cv_ref, cw_ref, bi_ref, bf_ref, out_ref, co_ref, no_ref, mo_ref, cvo_ref,
                  rows_s):
    nb = p_ref.shape[0]
    x = p_ref[:, 0:2 * BW]
    xa = x * cw_ref[CONV_W - 1:CONV_W, :]
    for j in range(CONV_W - 1):
        xa = xa + cv_ref[j] * cw_ref[j:j + 1, :]
    cvo_ref[0] = cv_ref[1]
    cvo_ref[1] = cv_ref[2]
    cvo_ref[2] = x
    qk = _silu(xa)
    q = qk[:, 0:BW]
    k = qk[:, BW:2 * BW] * (DH ** -0.5)
    v = p_ref[:, 2 * BW:3 * BW]
    ga = p_ref[:, 5 * BW:5 * BW + 128]
    ones_h = _head_ones()
    ip = _dot_sel(ga + bi_ref[...], _expand_sel(0))
    lf = _dot_sel(-_softplus(-(ga + bf_ref[...])), _expand_sel(NH))
    mm = _dot_sel(m_ref[...], _expand_sel(0))
    inter = lf + mm
    m_t = jnp.maximum(inter, ip)
    w_int = jnp.exp(inter - m_t)
    wgt = jnp.exp(ip - m_t)
    s = _dot_sel(q * k, ones_h) * wgt
    gate = _sigmoid(p_ref[:, 3 * BW:4 * BW]) * _silu(p_ref[:, 4 * BW:5 * BW])
    sv = s * v
    wk = wgt * k
    floor = jnp.exp(-m_t)
    mo_ref[...] = m_t
    for n, a in enumerate((q * w_int, w_int, wk, q, sv, s, floor, v, gate)):
        rows_s[n] = a

    def tile(t, carry):
        t0 = pl.multiple_of(t * STEP_TILE, STEP_TILE)
        rr = pl.ds(t0, STEP_TILE)
        qwt, dect, wkt = (_transpose_rows(rows_s[n, rr, :]) for n in range(3))
        wi, wkr, qr, svr, sr, fl, vr, gt = (rows_s[n, rr, :] for n in range(1, 9))
        for j in range(STEP_TILE):
            r = slice(j, j + 1)
            col = lambda m: m[:, r].reshape(NH, DH, 1)
            cm = c_ref[t0 + j]
            n3 = n_ref[t0 + j]
            num = jnp.sum(cm * col(qwt), axis=1, keepdims=True) + _head_stack(svr[r, :])
            qn = jnp.sum(_head_stack(qr[r, :]) * n3, axis=-1, keepdims=True)
            den = _head_stack(wi[r, :])[:, :, 0:1] * qn + _head_stack(sr[r, :])[:, :, 0:1]
            hd = num / jnp.maximum(jnp.abs(den), _head_stack(fl[r, :])[:, :, 0:1])
            co_ref[t0 + j] = col(dect) * cm + col(wkt) * _head_stack(vr[r, :])
            no_ref[t0 + j] = _head_stack(wi[r, :]) * n3 + _head_stack(wkr[r, :])
            out_ref[t0 + j] = hd * _head_stack(gt[r, :])
        return carry

    lax.fori_loop(0, nb // STEP_TILE, tile, 0)


def _ml_step(p, c_all, c_acc, l, n0, m0, c0, conv_w, b_i, b_f):
    nb = p.shape[0]
    ins = (p, c_all, n0, m0, c0, conv_w, b_i, b_f)
    in_specs = [_full(a.shape) for a in ins]
    in_specs[1] = _layer_block(nb, l)
    return pl.pallas_call(
        lambda *refs: _ml_step_body(*refs[:8], *refs[9:]),
        grid=(1,),
        in_specs=in_specs + [pl.BlockSpec(memory_space=pl.ANY)],
        out_specs=[_full((nb, NH, 1, DH)), _layer_block(nb, l), _full((nb, NH, 1, DH)), _full((nb, BW)),
                   _full((CONV_W - 1, nb, 2 * BW))],
        out_shape=[
            jax.ShapeDtypeStruct((nb, NH, 1, DH), F32),
            jax.ShapeDtypeStruct(c_acc.shape, F32),
            jax.ShapeDtypeStruct((nb, NH, 1, DH), F32),
            jax.ShapeDtypeStruct((nb, BW), F32),
            jax.ShapeDtypeStruct((CONV_W - 1, nb, 2 * BW), F32),
        ],
        input_output_aliases={8: 1},
        scratch_shapes=[pltpu.VMEM((9, nb, BW), F32)],
        compiler_params=_cparams(("arbitrary",)),
        name="mlstm_step",
    )(*ins, c_acc)


def _pad_lanes(x, width):
    return jnp.pad(x, [(0, 0)] * (x.ndim - 1) + [(0, width - x.shape[-1])])


def _prep_layer(l, prm, w_in_bf):
    w_in = w_in_bf[l]
    seg = {}
    off = 0
    for name, width in (('lru_x', BW), ('lru_z', BW), ('gdn_q', BW), ('gdn_k', BW), ('gdn_v', BW), ('gdn_z', BW),
                        ('gdn_a', NH), ('gdn_b', NH), ('s5_u', BW), ('s5_z', BW),
                        ('ml_q', BW), ('ml_k', BW), ('ml_v', BW), ('ml_o', BW), ('ml_z', BW),
                        ('ml_i', NH), ('ml_f', NH), ('merge', NH * D_MODEL)):
        seg[name] = w_in[:, off:off + width]
        off += width
    w_mix = jnp.concatenate([
        seg['lru_x'], seg['lru_z'],
        seg['s5_u'], seg['s5_z'],
        seg['gdn_q'], seg['gdn_k'], seg['gdn_v'], seg['gdn_z'], _pad_lanes(jnp.concatenate([seg['gdn_a'], seg['gdn_b']], 1), 128),
        seg['ml_q'], seg['ml_k'], seg['ml_v'], seg['ml_o'], seg['ml_z'], _pad_lanes(jnp.concatenate([seg['ml_i'], seg['ml_f']], 1), 128),
    ], axis=1)

    eye4 = jnp.eye(4, dtype=F32)
    blockdiag = lambda w: jnp.einsum('ncd,nm->ncmd', w, eye4).reshape(BW, BW)
    lru_wg = jnp.concatenate([blockdiag(prm['lru_w_a'][l]), blockdiag(prm['lru_w_x'][l])], axis=1).astype(BF16)
    lru_bg = jnp.concatenate([prm['lru_b_a'][l], prm['lru_b_x'][l]])[None, :]
    lru_sp = jax.nn.softplus(-prm['lru_lambda'][l])[None, :]

    lam_re, lam_im = prm['s5_lam_re'][l], prm['s5_lam_im'][l]
    dt = jnp.exp(prm['s5_log_dt'][l])[:, None]
    mag = jnp.exp(lam_re * dt)
    ang = lam_im * dt
    ar, ai = mag * jnp.cos(ang), mag * jnp.sin(ang)
    den = lam_re * lam_re + lam_im * lam_im
    cr = ((ar - 1.0) * lam_re + ai * lam_im) / den
    ci = (ai * lam_re - (ar - 1.0) * lam_im) / den
    b_re, b_im = prm['s5_b_re'][l], prm['s5_b_im'][l]
    bb_re = cr[..., None] * b_re - ci[..., None] * b_im
    bb_im = cr[..., None] * b_im + ci[..., None] * b_re
    eye_g = jnp.eye(S5_G, dtype=F32)
    in_dense = lambda bb: jnp.einsum('gnc,gh->gchn', bb, eye_g).reshape(BW, S5_W).astype(BF16)
    out_dense = lambda cc: jnp.einsum('gcn,gh->gnhc', cc, eye_g).reshape(S5_W, BW).astype(BF16)

    gate_row = lambda a, b: _pad_lanes(jnp.concatenate([a, b])[None, :], 128)
    zeros4 = jnp.zeros((NH,), F32)
    return dict(
        norm_w=prm['norm_w'][l][None, :],
        w_mix=w_mix,
        w_merge=seg['merge'],
        w_branch=prm['w_branch'][l].astype(BF16),
        w_out=prm['w_out'][l].astype(BF16),
        lru_cw=prm['lru_conv_w'][l], lru_wg=lru_wg, lru_bg=lru_bg, lru_sp=lru_sp,
        s5_ar=ar.reshape(1, S5_W), s5_ai=ai.reshape(1, S5_W),
        s5_bre=in_dense(bb_re), s5_bim=in_dense(bb_im),
        s5_cre=out_dense(prm['s5_c_re'][l]), s5_cim=out_dense(prm['s5_c_im'][l]),
        s5_d=prm['s5_d'][l][None, :], s5_wglu=prm['s5_w_glu'][l].astype(BF16),
        gdn_cw=prm['gdn_conv_w'][l],
        gdn_nega=gate_row(-jnp.exp(prm['gdn_A_log'][l]), zeros4),
        gdn_dtb=gate_row(prm['gdn_dt_bias'][l], zeros4),
        gdn_nw=jnp.tile(prm['gdn_norm_w'][l], NH)[None, :],
        gdn_nw1=prm['gdn_norm_w'][l][None, :],
        ml_cw=prm['ml_conv_w'][l],
        ml_bi=gate_row(prm['ml_b_i'][l], zeros4),
        ml_bf=gate_row(zeros4, prm['ml_b_f'][l]),
    )


def _prompt_layer(x, pp, final_w, final, tm, tt, tg):
    b, t, _ = x.shape
    p_lru, p_s5, p_gdn, p_ml = _in_proj(x, pp['norm_w'], pp['w_mix'], tm, True)
    z = lambda *s: jnp.zeros(s, F32)
    out_a, lru_h, lru_c = _lru(p_lru, z(b, BW), z(CONV_W - 1, b, BW),
                               pp['lru_cw'], pp['lru_wg'], pp['lru_bg'], pp['lru_sp'], tt)
    out_c, s5_re, s5_im = _s5(p_s5, z(b, S5_W), z(b, S5_W), pp['s5_ar'], pp['s5_ai'],
                              pp['s5_bre'], pp['s5_bim'], pp['s5_cre'], pp['s5_cim'], pp['s5_d'], pp['s5_wglu'], tt)
    out_b, gdn_s, gdn_c, out_d, ml_c, ml_n, ml_m, ml_cv = _chunked(
        p_gdn, z(b, NH, DH, DH), z(b, CONV_W - 1, 3 * BW), pp['gdn_cw'], pp['gdn_nega'], pp['gdn_dtb'], pp['gdn_nw'],
        p_ml, z(b, NH, DH, DH), z(b, 1, BW), z(b, 1, BW), z(b, CONV_W - 1, 2 * BW), pp['ml_cw'], pp['ml_bi'], pp['ml_bf'],
        b, tg)
    y = _merge(x, (out_a, out_b, out_c, out_d), pp['norm_w'], pp['w_merge'],
               pp['w_branch'], pp['w_out'], final_w, tm, final)
    st = dict(lru_h=lru_h, lru_conv=jnp.transpose(lru_c, (1, 0, 2)), gdn_S=gdn_s, gdn_conv=gdn_c,
              s5_re=s5_re.reshape(b, S5_G, S5_N), s5_im=s5_im.reshape(b, S5_G, S5_N),
              ml_C=ml_c, ml_n=ml_n.reshape(b, NH, DH), ml_m=ml_m.reshape(b, NH, DH)[:, :, 0], ml_conv=ml_cv)
    return y, st


BIG_STATES = ('gdn_S', 'ml_C')


def _sample_layer(x, st, big_in, big_acc, l, pp, final_w, final):
    nb = x.shape[1]
    p_lru, p_s5, p_gdn, p_ml = _in_proj(x, pp['norm_w'], pp['w_mix'], nb, False)
    tmaj = lambda c: jnp.transpose(c, (1, 0, 2))
    out_a, lru_h, lru_c = _lru(p_lru.reshape(1, nb, W_LRU), st['lru_h'], tmaj(st['lru_conv']),
                               pp['lru_cw'], pp['lru_wg'], pp['lru_bg'], pp['lru_sp'], 1)
    out_c, s5_re, s5_im = _s5(p_s5.reshape(1, nb, W_S5), st['s5_re'].reshape(nb, S5_W), st['s5_im'].reshape(nb, S5_W),
                              pp['s5_ar'], pp['s5_ai'], pp['s5_bre'], pp['s5_bim'], pp['s5_cre'], pp['s5_cim'],
                              pp['s5_d'], pp['s5_wglu'], 1)
    out_b, gdn_s, gdn_c = _gdn_step(p_gdn, big_in['gdn_S'], big_acc['gdn_S'], l, tmaj(st['gdn_conv']), pp['gdn_cw'],
                                    pp['gdn_nega'], pp['gdn_dtb'], pp['gdn_nw1'])
    out_d, ml_c, ml_n, ml_m, ml_cv = _ml_step(p_ml, big_in['ml_C'], big_acc['ml_C'], l,
                                              st['ml_n'].reshape(nb, NH, 1, DH), _pad_lanes(st['ml_m'], 128),
                                              tmaj(st['ml_conv']), pp['ml_cw'], pp['ml_bi'], pp['ml_bf'])
    y = _merge(x, (out_a.reshape(nb, BW), out_b.reshape(nb, BW), out_c.reshape(nb, BW), out_d.reshape(nb, BW)),
               pp['norm_w'], pp['w_merge'], pp['w_branch'], pp['w_out'], final_w, nb, final)
    new = dict(lru_h=lru_h, lru_conv=tmaj(lru_c), gdn_conv=tmaj(gdn_c),
               s5_re=s5_re.reshape(nb, S5_G, S5_N), s5_im=s5_im.reshape(nb, S5_G, S5_N),
               ml_n=ml_n.reshape(nb, NH, DH), ml_m=ml_m.reshape(nb, NH, DH)[:, :, 0], ml_conv=tmaj(ml_cv))
    return y, new, dict(gdn_S=gdn_s, ml_C=ml_c)


STATE_NAMES = ('lru_h', 'lru_conv', 'gdn_S', 'gdn_conv', 's5_re', 's5_im', 'ml_C', 'ml_n', 'ml_m', 'ml_conv')

TM_ROWS = 64
TT_SCAN = 128


def kernel(x_prompt, x_sample, state_lru_h, state_lru_conv, state_gdn_S, state_gdn_conv, state_s5_re, state_s5_im, state_ml_C, state_ml_n, state_ml_m, state_ml_conv, norm_w, w_in, lru_conv_w, lru_w_a, lru_b_a, lru_w_x, lru_b_x, lru_lambda, gdn_conv_w, gdn_A_log, gdn_dt_bias, gdn_norm_w, s5_lam_re, s5_lam_im, s5_log_dt, s5_b_re, s5_b_im, s5_c_re, s5_c_im, s5_d, s5_w_glu, ml_conv_w, ml_b_i, ml_b_f, w_branch, w_out, final_norm_w):
    prm = dict(norm_w=norm_w, w_in=w_in, lru_conv_w=lru_conv_w, lru_w_a=lru_w_a, lru_b_a=lru_b_a, lru_w_x=lru_w_x,
               lru_b_x=lru_b_x, lru_lambda=lru_lambda, gdn_conv_w=gdn_conv_w, gdn_A_log=gdn_A_log,
               gdn_dt_bias=gdn_dt_bias, gdn_norm_w=gdn_norm_w, s5_lam_re=s5_lam_re, s5_lam_im=s5_lam_im,
               s5_log_dt=s5_log_dt, s5_b_re=s5_b_re, s5_b_im=s5_b_im, s5_c_re=s5_c_re, s5_c_im=s5_c_im, s5_d=s5_d,
               s5_w_glu=s5_w_glu, ml_conv_w=ml_conv_w, ml_b_i=ml_b_i, ml_b_f=ml_b_f, w_branch=w_branch, w_out=w_out)
    depth = w_in.shape[0]
    t = x_prompt.shape[1]
    nb = x_sample.shape[0]
    final_w = final_norm_w[None, :]
    st_in = dict(lru_h=state_lru_h, lru_conv=state_lru_conv, gdn_S=state_gdn_S, gdn_conv=state_gdn_conv,
                 s5_re=state_s5_re, s5_im=state_s5_im, ml_C=state_ml_C, ml_n=state_ml_n, ml_m=state_ml_m,
                 ml_conv=state_ml_conv)
    tm = min(TM_ROWS, t)
    tt = min(TT_SCAN, t)
    tg = min(SE, t)

    xp = x_prompt
    xs = x_sample.reshape(1, nb, D_MODEL)
    small = [n for n in STATE_NAMES if n not in BIG_STATES]
    new_p = {n: [] for n in STATE_NAMES}
    new_s = {n: [] for n in small}
    big_in = {n: st_in[n] for n in BIG_STATES}
    big_acc = {n: jnp.zeros(st_in[n].shape, F32) for n in BIG_STATES}
    w_in_bf = w_in.astype(BF16)
    for l in range(depth):
        pp = _prep_layer(l, prm, w_in_bf)
        final = l == depth - 1
        xp, sp = _prompt_layer(xp, pp, final_w, final, tm, tt, tg)
        xs, ss, big_acc = _sample_layer(xs, {n: st_in[n][l] for n in small}, big_in, big_acc, l, pp, final_w, final)
        for n in STATE_NAMES:
            new_p[n].append(sp[n])
        for n in small:
            new_s[n].append(ss[n])
    np_ = {n: jnp.stack(v) for n, v in new_p.items()}
    ns_ = {n: jnp.stack(v) for n, v in new_s.items()}
    ns_.update(big_acc)
    y_sample = xs.reshape(nb, 1, D_MODEL)
    return (xp, y_sample,
            np_['lru_h'], ns_['lru_h'], np_['lru_conv'], ns_['lru_conv'],
            np_['gdn_S'], ns_['gdn_S'], np_['gdn_conv'], ns_['gdn_conv'],
            np_['s5_re'], ns_['s5_re'], np_['s5_im'], ns_['s5_im'],
            np_['ml_C'], ns_['ml_C'], np_['ml_n'], ns_['ml_n'], np_['ml_m'], ns_['ml_m'],
            np_['ml_conv'], ns_['ml_conv'])
```

```python
import functools
import math

import jax
import jax.numpy as jnp
from jax import lax
from jax.experimental import pallas as pl
from jax.experimental.pallas import tpu as pltpu

F32 = jnp.float32
BF16 = jnp.bfloat16

D_MODEL = 1024
BW = 256
NH = 4
DH = 64
CONV_W = 4
CHUNK = 64
LRU_C = 8.0
S5_G = 16
S5_N = 64
S5_W = S5_G * S5_N
EPS = 1e-6

W_LRU = 2 * BW
W_S5 = 2 * BW
W_GDN = 4 * BW + 128
W_ML = 5 * BW + 128
W_MIX = (W_LRU, W_S5, W_GDN, W_ML)

VMEM_LIMIT_BYTES = 56 * 1024 * 1024


def _cparams(sem):
    return pltpu.CompilerParams(dimension_semantics=sem, vmem_limit_bytes=VMEM_LIMIT_BYTES)


def _sigmoid(x):
    return 1.0 / (1.0 + jnp.exp(-x))


def _silu(x):
    return x * _sigmoid(x)


def _softplus(x):
    return jnp.maximum(x, 0.0) + jnp.log(1.0 + jnp.exp(-jnp.abs(x)))


def _gelu_tanh(x):
    c = math.sqrt(2.0 / math.pi)
    return x * (0.5 * (1.0 + jnp.tanh(c * (x + 0.044715 * (x * x * x)))))


def _split3(x):
    hi = x.astype(BF16)
    r = x - hi.astype(F32)
    mid = r.astype(BF16)
    lo = (r - mid.astype(F32)).astype(BF16)
    return hi, mid, lo


def _split2(x):
    hi = x.astype(BF16)
    return hi, (x - hi.astype(F32)).astype(BF16)


_NN = (((1,), (0,)), ((), ()))
_NT = (((1,), (1,)), ((), ()))
_TN = (((0,), (0,)), ((), ()))


def _sel_dot(sel, x, dims=_NN):
    out = None
    for p in _split3(x):
        t = lax.dot_general(sel, p, dims, preferred_element_type=F32)
        out = t if out is None else out + t
    return out


def _dot_sel(x, sel, dims=_NN, split=_split3):
    out = None
    for p in split(x):
        t = lax.dot_general(p, sel, dims, preferred_element_type=F32)
        out = t if out is None else out + t
    return out


def _iota(shape, axis):
    return lax.broadcasted_iota(jnp.int32, shape, axis)


def _head_ones():
    return (_iota((BW, BW), 0) // DH == _iota((BW, BW), 1) // DH).astype(BF16)


def _expand_sel(offset):
    return (_iota((128, BW), 0) == _iota((128, BW), 1) // DH + offset).astype(BF16)


def _rms_rows(x, w_row):
    ms = jnp.mean(x * x, axis=-1, keepdims=True)
    return (x * lax.rsqrt(ms + EPS)) * w_row


def _inproj_body(x_ref, nw_ref, w_ref, *o_refs):
    nb, tm, _ = x_ref.shape
    xn = _rms_rows(x_ref[...].reshape(nb * tm, D_MODEL), nw_ref[...]).astype(BF16)
    off = 0
    for o_ref in o_refs:
        wd = o_ref.shape[-1] if len(o_ref.shape) == 3 else o_ref.shape[-1] // nb
        y = jnp.dot(xn, w_ref[:, off:off + wd], preferred_element_type=F32)
        if len(o_ref.shape) == 3:
            o_ref[...] = jnp.swapaxes(y.reshape(nb, tm, wd), 0, 1)
        else:
            for bi in range(nb):
                o_ref[:, bi * wd:(bi + 1) * wd] = y[bi * tm:(bi + 1) * tm]
        off += wd


W_PACK = NH * D_MODEL


def _in_proj(x, norm_w, w_pack, l, tm, scan_3d):
    b, t, _ = x.shape
    assert sum(W_MIX) <= W_PACK
    specs, shapes = [], []
    for n, wd in enumerate(W_MIX):
        if scan_3d and n < 2:
            specs.append(pl.BlockSpec((tm, b, wd), lambda i: (i, 0, 0)))
            shapes.append(jax.ShapeDtypeStruct((t, b, wd), F32))
        else:
            specs.append(pl.BlockSpec((tm, b * wd), lambda i: (i, 0)))
            shapes.append(jax.ShapeDtypeStruct((t, b * wd), F32))
    return pl.pallas_call(
        _inproj_body,
        grid=(t // tm,),
        in_specs=[
            pl.BlockSpec((b, tm, D_MODEL), lambda i: (0, i, 0)),
            pl.BlockSpec((1, D_MODEL), lambda i: (0, 0)),
            pl.BlockSpec((None, D_MODEL, W_PACK), lambda i: (l, 0, 1)),
        ],
        out_specs=specs,
        out_shape=shapes,
        compiler_params=_cparams(("arbitrary",)),
        name="in_proj",
    )(x, norm_w, w_pack)


def _merge_body(x_ref, a_ref, b_ref, c_ref, d_ref, nw_ref, wm_ref, wb_ref, wo_ref, fw_ref, y_ref, *, final):
    nb, tm, _ = x_ref.shape
    x = x_ref[...].reshape(nb * tm, D_MODEL)
    xn = _rms_rows(x, nw_ref[...]).astype(BF16)
    merged = None
    for n, r in enumerate((a_ref, b_ref, c_ref, d_ref)):
        if len(r.shape) == 3:
            br = jnp.swapaxes(r[...], 0, 1).reshape(nb * tm, BW)
        else:
            br = jnp.concatenate([r[:, bi * BW:(bi + 1) * BW] for bi in range(nb)], axis=0)
        gate = _sigmoid(jnp.dot(xn, wm_ref[:, n * D_MODEL:(n + 1) * D_MODEL], preferred_element_type=F32))
        proj = jnp.dot(br.astype(BF16), wb_ref[n], preferred_element_type=F32)
        merged = gate * proj if merged is None else merged + gate * proj
    y = x + jnp.dot(merged.astype(BF16), wo_ref[...], preferred_element_type=F32)
    if final:
        y = _rms_rows(y, fw_ref[...])
    y_ref[...] = y.reshape(nb, tm, D_MODEL)


def _merge(x, branches, norm_w, w_pack, l, w_branch, w_out, final_w, tm, final):
    b, t, _ = x.shape
    const2 = lambda i: (0, 0)
    br_specs = [pl.BlockSpec((tm,) + r.shape[1:], (lambda i: (i, 0, 0)) if r.ndim == 3 else (lambda i: (i, 0)))
                for r in branches]
    return pl.pallas_call(
        functools.partial(_merge_body, final=final),
        grid=(t // tm,),
        in_specs=[
            pl.BlockSpec((b, tm, D_MODEL), lambda i: (0, i, 0)),
            *br_specs,
            pl.BlockSpec((1, D_MODEL), const2),
            pl.BlockSpec((None, D_MODEL, W_PACK), lambda i: (l, 0, 0)),
            pl.BlockSpec((NH, BW, D_MODEL), lambda i: (0, 0, 0)),
            pl.BlockSpec((D_MODEL, D_MODEL), const2),
            pl.BlockSpec((1, D_MODEL), const2),
        ],
        out_specs=pl.BlockSpec((b, tm, D_MODEL), lambda i: (0, i, 0)),
        out_shape=jax.ShapeDtypeStruct((b, t, D_MODEL), F32),
        compiler_params=_cparams(("arbitrary",)),
        name="merge",
    )(x, *branches, norm_w, w_pack, w_branch, w_out, final_w)


def _lru_body(p_ref, h0_ref, c0_ref, cw_ref, wg_ref, bg_ref, sp_ref, out_ref, ht_ref, ct_ref,
              xp_s, a_s, b_s, h_s):
    tt, nb = p_ref.shape[0], p_ref.shape[1]

    @pl.when(pl.program_id(0) == 0)
    def _():
        h_s[...] = h0_ref[...]
        xp_s[0:CONV_W - 1] = c0_ref[...]

    xp_s[CONV_W - 1:CONV_W - 1 + tt] = p_ref[:, :, 0:BW]
    xa = None
    for j in range(CONV_W):
        term = xp_s[j:j + tt] * cw_ref[j:j + 1, :].reshape(1, 1, BW)
        xa = term if xa is None else xa + term
    tail = xp_s[tt:tt + CONV_W - 1]
    xp_s[0:CONV_W - 1] = tail
    ct_ref[...] = tail

    xa2 = xa.reshape(tt * nb, BW)
    gates = jnp.dot(xa2.astype(BF16), wg_ref[...], preferred_element_type=F32) + bg_ref[...]
    r = _sigmoid(gates[:, 0:BW])
    ig = _sigmoid(gates[:, BW:2 * BW])
    log_a = (-LRU_C * r) * sp_ref[...]
    a = jnp.exp(log_a)
    bb = jnp.sqrt(1.0 - jnp.exp(2.0 * log_a)) * (ig * xa2)
    a_s[...] = a.reshape(tt, nb, BW)
    b_s[...] = bb.reshape(tt, nb, BW)

    def step(t, h):
        h = a_s[t] * h + b_s[t]
        b_s[t] = h
        return h

    h = lax.fori_loop(0, tt, step, h_s[...], unroll=min(8, tt))
    h_s[...] = h
    ht_ref[...] = h
    out_ref[...] = b_s[...] * _silu(p_ref[:, :, BW:2 * BW])


def _lru(p, h0, c0, conv_w, w_gates, b_gates, sp_lam, tt):
    t, nb, _ = p.shape
    c2 = lambda i: (0, 0)
    c3 = lambda i: (0, 0, 0)
    return pl.pallas_call(
        _lru_body,
        grid=(t // tt,),
        in_specs=[
            pl.BlockSpec((tt, nb, W_LRU), lambda i: (i, 0, 0)),
            pl.BlockSpec((nb, BW), c2),
            pl.BlockSpec((CONV_W - 1, nb, BW), c3),
            pl.BlockSpec((CONV_W, BW), c2),
            pl.BlockSpec((BW, 2 * BW), c2),
            pl.BlockSpec((1, 2 * BW), c2),
            pl.BlockSpec((1, BW), c2),
        ],
        out_specs=[
            pl.BlockSpec((tt, nb, BW), lambda i: (i, 0, 0)),
            pl.BlockSpec((nb, BW), c2),
            pl.BlockSpec((CONV_W - 1, nb, BW), c3),
        ],
        out_shape=[
            jax.ShapeDtypeStruct((t, nb, BW), F32),
            jax.ShapeDtypeStruct((nb, BW), F32),
            jax.ShapeDtypeStruct((CONV_W - 1, nb, BW), F32),
        ],
        scratch_shapes=[
            pltpu.VMEM((tt + CONV_W - 1, nb, BW), F32),
            pltpu.VMEM((tt, nb, BW), F32),
            pltpu.VMEM((tt, nb, BW), F32),
            pltpu.VMEM((nb, BW), F32),
        ],
        compiler_params=_cparams(("arbitrary",)),
        name="rglru",
    )(p, h0, c0, conv_w, w_gates, b_gates, sp_lam)


def _s5_body(p_ref, hr0_ref, hi0_ref, ar_ref, ai_ref, bre_ref, bim_ref, cre_ref, cim_ref, d_ref, wglu_ref,
             out_ref, hrt_ref, hit_ref, xr_s, xi_s, hr_s, hi_s):
    tt, nb = p_ref.shape[0], p_ref.shape[1]

    @pl.when(pl.program_id(0) == 0)
    def _():
        hr_s[...] = hr0_ref[...]
        hi_s[...] = hi0_ref[...]

    u2 = p_ref[:, :, 0:BW].reshape(tt * nb, BW)
    ub = u2.astype(BF16)
    xr_s[...] = jnp.dot(ub, bre_ref[...], preferred_element_type=F32).reshape(tt, nb, S5_W)
    xi_s[...] = jnp.dot(ub, bim_ref[...], preferred_element_type=F32).reshape(tt, nb, S5_W)
    ar = jnp.broadcast_to(ar_ref[...], (nb, S5_W))
    ai = jnp.broadcast_to(ai_ref[...], (nb, S5_W))

    def step(t, c):
        hr, hi = c
        nr = ar * hr - ai * hi + xr_s[t]
        ni = ar * hi + ai * hr + xi_s[t]
        xr_s[t] = nr
        xi_s[t] = ni
        return nr, ni

    hr, hi = lax.fori_loop(0, tt, step, (hr_s[...], hi_s[...]), unroll=min(4, tt))
    hr_s[...] = hr
    hi_s[...] = hi
    hrt_ref[...] = hr
    hit_ref[...] = hi

    y = (jnp.dot(xr_s[...].reshape(tt * nb, S5_W).astype(BF16), cre_ref[...], preferred_element_type=F32)
         - jnp.dot(xi_s[...].reshape(tt * nb, S5_W).astype(BF16), cim_ref[...], preferred_element_type=F32)
         + d_ref[...] * u2)
    y = _gelu_tanh(y)
    y = y * _sigmoid(jnp.dot(y.astype(BF16), wglu_ref[...], preferred_element_type=F32))
    out_ref[...] = y.reshape(tt, nb, BW) * _silu(p_ref[:, :, BW:2 * BW])


def _s5(p, hr0, hi0, ar, ai, bre, bim, cre, cim, d, wglu, tt):
    t, nb, _ = p.shape
    c2 = lambda i: (0, 0)
    st = pl.BlockSpec((nb, S5_W), c2)
    row = pl.BlockSpec((1, S5_W), c2)
    return pl.pallas_call(
        _s5_body,
        grid=(t // tt,),
        in_specs=[
            pl.BlockSpec((tt, nb, W_S5), lambda i: (i, 0, 0)),
            st, st, row, row,
            pl.BlockSpec((BW, S5_W), c2), pl.BlockSpec((BW, S5_W), c2),
            pl.BlockSpec((S5_W, BW), c2), pl.BlockSpec((S5_W, BW), c2),
            pl.BlockSpec((1, BW), c2),
            pl.BlockSpec((BW, BW), c2),
        ],
        out_specs=[pl.BlockSpec((tt, nb, BW), lambda i: (i, 0, 0)), st, st],
        out_shape=[
            jax.ShapeDtypeStruct((t, nb, BW), F32),
            jax.ShapeDtypeStruct((nb, S5_W), F32),
            jax.ShapeDtypeStruct((nb, S5_W), F32),
        ],
        scratch_shapes=[
            pltpu.VMEM((tt, nb, S5_W), F32),
            pltpu.VMEM((tt, nb, S5_W), F32),
            pltpu.VMEM((nb, S5_W), F32),
            pltpu.VMEM((nb, S5_W), F32),
        ],
        compiler_params=_cparams(("arbitrary",)),
        name="s5",
    )(p, hr0, hi0, ar, ai, bre, bim, cre, cim, d, wglu)


def _conv_rows(xp_s, p_cols, cw_ref, tg):
    xp_s[8:8 + tg] = p_cols
    y = None
    for j in range(CONV_W):
        term = xp_s[8 - (CONV_W - 1) + j:8 - (CONV_W - 1) + j + tg] * cw_ref[j:j + 1, :]
        y = term if y is None else y + term
    tail = xp_s[8 + tg - (CONV_W - 1):8 + tg]
    xp_s[8 - (CONV_W - 1):8] = tail
    return y, tail


SE = NH * CHUNK


def _se_masks():
    r = _iota((SE, SE), 0)
    c = _iota((SE, SE), 1)
    same = (r // CHUNK) == (c // CHUNK)
    return same, same & ((r % CHUNK) >= (c % CHUNK))


def _cols_masks(n):
    i = _iota((n, BW), 0) % CHUNK
    j = _iota((n, BW), 1) % CHUNK
    return i >= j, i > j, i == j


def _stack_se(x_bf, same):
    return jnp.where(same, jnp.concatenate([x_bf] * NH, axis=0), jnp.zeros((), x_bf.dtype))


def _tile_rows(x):
    return jnp.concatenate([x] * NH, axis=0)


def _fold_se(x_se):
    out = x_se[0:CHUNK]
    for h in range(1, NH):
        out = out + x_se[h * CHUNK:(h + 1) * CHUNK]
    return out


def _load_cols(s_s, blocks_ref):
    for h in range(NH):
        s_s[:, h * DH:(h + 1) * DH] = blocks_ref[h]


def _store_cols(blocks_ref, s_s):
    for h in range(NH):
        blocks_ref[h] = s_s[:, h * DH:(h + 1) * DH]


def _last_rows(x, nc):
    return jnp.concatenate([jnp.broadcast_to(x[c * CHUNK + CHUNK - 1:(c + 1) * CHUNK], (CHUNK, BW)) for c in range(nc)], axis=0)


def _gdn_init(s0_ref, c0_ref, xp_s, s_s):
    _load_cols(s_s, s0_ref)
    xp_s[8 - (CONV_W - 1):8] = c0_ref[...]


def _gdn_main(p_ref, cw_ref, nega_ref, dtb_ref, nw_ref, out_ref, st_ref, ct_ref, xp_s, o_s, s_s):
    tg = p_ref.shape[0]
    nc = tg // CHUNK
    y, tail = _conv_rows(xp_s, p_ref[:, 0:3 * BW], cw_ref, tg)
    ct_ref[...] = tail
    qkv = _silu(y)
    ones_h = _head_ones()
    q = qkv[:, 0:BW]
    k = qkv[:, BW:2 * BW]
    q = q * lax.rsqrt(_dot_sel(q * q, ones_h, split=_split2) + EPS) * (DH ** -0.5)
    k = k * lax.rsqrt(_dot_sel(k * k, ones_h, split=_split2) + EPS)
    v = qkv[:, 2 * BW:3 * BW]
    yield
    ga = p_ref[:, 4 * BW:4 * BW + 128]
    lane = _iota((tg, 128), 1)
    g = nega_ref[...] * _softplus(ga + dtb_ref[...])
    gall = jnp.where(lane < NH, g, _sigmoid(ga))

    same, tri = _se_masks()
    tri_c, strict_c, _ = _cols_masks(tg)
    tri_bf = tri.astype(BF16)
    gcum = _sel_dot(tri_bf, gall)
    beta_x = _dot_sel(gall, _expand_sel(NH))
    gc_x = _dot_sel(gcum, _expand_sel(0))
    gl_x = _last_rows(gc_x, nc)
    egc = jnp.exp(gc_x)
    kb = k * beta_x
    k_b = k.astype(BF16)
    kb_b = kb.astype(BF16)
    q_b = q.astype(BF16)
    vb_b = (v * beta_x).astype(BF16)
    kbg_b = (kb * egc).astype(BF16)
    qd_b = (q * egc).astype(BF16)
    kd_b = (k * jnp.exp(gl_x - gc_x)).astype(BF16)
    yield

    cs = range(nc)
    rows = [slice(c * CHUNK, (c + 1) * CHUNK) for c in cs]
    g_x = _dot_sel(gall, _expand_sel(0))
    diff = _sel_dot(tri_bf, jnp.where(strict_c, g_x, 0.0))
    dec_all = jnp.where(tri_c, jnp.exp(diff), 0.0)
    dec_t = [_tile_rows(dec_all[rows[c]]) for c in cs]
    ndec_all = jnp.where(strict_c, -dec_all, 0.0)
    ndec_t = [_tile_rows(ndec_all[rows[c]]) for c in cs]

    dot = functools.partial(jnp.dot, preferred_element_type=F32)
    cat = functools.partial(jnp.concatenate, axis=0)
    k_se = [_stack_se(k_b[rows[c]], same) for c in cs]
    kq = [lax.dot_general(cat([_stack_se(kb_b[rows[c]], same), _stack_se(q_b[rows[c]], same)]), k_se[c], _NT,
                          preferred_element_type=F32) for c in cs]
    attn = [(kq[c][SE:] * dec_t[c]).astype(BF16) for c in cs]
    eye = (_iota((SE, SE), 0) == _iota((SE, SE), 1)).astype(F32)
    npow = [kq[c][:SE] * ndec_t[c] for c in cs]
    tinv = [eye + npow[c] for c in cs]
    nb = [n.astype(BF16) for n in npow]
    yield
    nb = [dot(n, n).astype(BF16) for n in nb]
    for _ in range(4):
        yield
        both = [dot(cat([t.astype(BF16), n]), n) for t, n in zip(tinv, nb)]
        tinv = [t + b2[:SE] for t, b2 in zip(tinv, both)]
        nb = [b2[SE:].astype(BF16) for b2 in both]
    yield
    tinv = [t + dot(t.astype(BF16), n) for t, n in zip(tinv, nb)]
    t_bf = [t.astype(BF16) for t in tinv]
    u = [dot(t_bf[c], _stack_se(vb_b[rows[c]], same)) for c in cs]
    w = [dot(t_bf[c], _stack_se(kbg_b[rows[c]], same)).astype(BF16) for c in cs]

    kd_t = [_stack_se(kd_b[rows[c]], same).T for c in cs]
    s = s_s[...]
    yield _READY
    for c in cs:
        sb = _stack_se(s.astype(BF16), same)
        ws = dot(cat([w[c], _stack_se(qd_b[rows[c]], same)]), sb)
        vb = (u[c] - ws[:SE]).astype(BF16)
        av = dot(cat([attn[c], kd_t[c]]), vb)
        s = s * jnp.exp(gl_x[c * CHUNK:c * CHUNK + 1]) + _fold_se(av[SE:])
        o_s[rows[c], :] = _fold_se(ws[SE:] + av[:SE])
        yield
    s_s[...] = s
    _store_cols(st_ref, s_s)

    o = o_s[...]
    ms = _dot_sel(o * o, ones_h, split=_split2) * (1.0 / DH)
    out_ref[...] = (o * lax.rsqrt(ms + EPS)) * nw_ref[...] * _silu(p_ref[:, 3 * BW:4 * BW])


def _cummax_rows(x):
    row = _iota(x.shape, 0)
    sh = 1
    while sh < x.shape[0]:
        x = jnp.maximum(x, jnp.where(row >= sh, pltpu.roll(x, sh, 0), -jnp.inf))
        sh *= 2
    return x


def _ml_init(c0m_ref, n0_ref, m0_ref, c0_ref, nt_ref, mt_ref, xp_s, cm_s):
    _load_cols(cm_s, c0m_ref)
    nt_ref[...] = n0_ref[...]
    mt_ref[...] = m0_ref[...]
    xp_s[8 - (CONV_W - 1):8] = c0_ref[...]


def _ml_main(p_ref, cw_ref, bi_ref, bf_ref, out_ref, ct_m_ref, nt_ref, mt_ref, cvt_ref, xp_s, h_s, cm_s):
    tg = p_ref.shape[0]
    nc = tg // CHUNK
    y, tail = _conv_rows(xp_s, p_ref[:, 0:2 * BW], cw_ref, tg)
    cvt_ref[...] = tail
    qk = _silu(y)
    q = qk[:, 0:BW]
    k = qk[:, BW:2 * BW] * (DH ** -0.5)
    yield
    ga = p_ref[:, 5 * BW:5 * BW + 128]
    lane = _iota((tg, 128), 1)
    gall = jnp.where(lane < NH, ga + bi_ref[...], -_softplus(-(ga + bf_ref[...])))

    same, tri = _se_masks()
    tri_c, strict_c, eye_c = _cols_masks(tg)
    tri_bf = tri.astype(BF16)
    ones_h = _head_ones()
    bcum = _sel_dot(tri_bf, gall)
    i_x = _dot_sel(gall, _expand_sel(0))
    lf_x = _dot_sel(gall, _expand_sel(NH))
    b_x = _dot_sel(bcum, _expand_sel(NH))
    r_x = i_x - b_x
    e_x = _last_rows(b_x, nc) - b_x + i_x
    q_b = q.astype(BF16)
    k_b = k.astype(BF16)
    v_b = p_ref[:, 2 * BW:3 * BW].astype(BF16)
    yield

    cs = range(nc)
    rows = [slice(c * CHUNK, (c + 1) * CHUNK) for c in cs]
    dl_all = (_sel_dot(tri_bf, jnp.where(strict_c, lf_x, 0.0)) + _sel_dot(same.astype(BF16), jnp.where(eye_c, i_x, 0.0)))
    dl_all = jnp.where(tri_c, dl_all, -jnp.inf)
    dlog = [dl_all[rows[c]] for c in cs]
    yield
    dmax = [b_x[rows[c]] + _cummax_rows(r_x[rows[c]]) for c in cs]
    yield
    q_se = [_stack_se(q_b[rows[c]], same) for c in cs]
    v_se = [_stack_se(v_b[rows[c]], same) for c in cs]
    qkm = [_fold_se(lax.dot_general(q_se[c], _stack_se(k_b[rows[c]], same), _NT, preferred_element_type=F32)) for c in cs]

    cm = cm_s[...]
    nrow = nt_ref[...]
    mrow = mt_ref[...]
    yield _READY
    for c in cs:
        bx = b_x[rows[c]]
        inter = bx + mrow
        m_t = jnp.maximum(inter, dmax[c])
        s = qkm[c] * jnp.exp(dlog[c] - m_t)
        w_int = jnp.exp(inter - m_t)
        num = (w_int * _fold_se(jnp.dot(q_se[c], _stack_se(cm.astype(BF16), same), preferred_element_type=F32))
               + _fold_se(jnp.dot(_stack_se(s.astype(BF16), same), v_se[c], preferred_element_type=F32)))
        den = w_int * _dot_sel(q[rows[c]] * nrow, ones_h) + _dot_sel(s, ones_h)
        h_s[rows[c], :] = num / jnp.maximum(jnp.abs(den), jnp.exp(-m_t))
        blast = bx[CHUNK - 1:CHUNK]
        ex = e_x[rows[c]]
        m_new = jnp.maximum(blast + mrow, jnp.max(ex, axis=0, keepdims=True))
        dec = jnp.exp(blast + mrow - m_new)
        wk = jnp.exp(ex - m_new) * k[rows[c]]
        cm = dec * cm + _fold_se(lax.dot_general(_stack_se(wk.astype(BF16), same), v_se[c], _TN, preferred_element_type=F32))
        nrow = dec * nrow + jnp.sum(wk, axis=0, keepdims=True)
        mrow = m_new
        yield
    cm_s[...] = cm
    _store_cols(ct_m_ref, cm_s)
    nt_ref[...] = nrow
    mt_ref[...] = mrow
    out_ref[...] = h_s[...] * _sigmoid(p_ref[:, 3 * BW:4 * BW]) * _silu(p_ref[:, 4 * BW:5 * BW])


_READY, _DONE = object(), object()


def _chunked_body(pg_ref, s0_ref, gc0_ref, gcw_ref, nega_ref, dtb_ref, gnw_ref,
                  pm_ref, c0m_ref, n0_ref, m0_ref, mc0_ref, mcw_ref, bi_ref, bf_ref,
                  og_ref, st_ref, gct_ref, om_ref, ctm_ref, nt_ref, mt_ref, mct_ref,
                  gxp_s, go_s, gs_s, mxp_s, mh_s, mcm_s):
    @pl.when(pl.program_id(1) == 0)
    def _():
        _gdn_init(s0_ref, gc0_ref, gxp_s, gs_s)
        _ml_init(c0m_ref, n0_ref, m0_ref, mc0_ref, nt_ref, mt_ref, mxp_s, mcm_s)

    live = [_gdn_main(pg_ref, gcw_ref, nega_ref, dtb_ref, gnw_ref, og_ref, st_ref, gct_ref, gxp_s, go_s, gs_s),
            _ml_main(pm_ref, mcw_ref, bi_ref, bf_ref, om_ref, ctm_ref, nt_ref, mt_ref, mct_ref, mxp_s, mh_s, mcm_s)]
    pre = list(live)
    while pre:
        pre = [g for g in pre if next(g) is not _READY]
    while live:
        live = [g for g in live if next(g, _DONE) is not _DONE]


def _chunked(pg, s0, gc0, gcw, neg_a, dt_bias, gnw, pm, c0m, n0, m0, mc0, mcw, b_i, b_f, b, tg):
    assert tg == SE
    t = pg.shape[0]
    c2 = lambda bi, i: (0, 0)
    per_b3 = lambda bi, i: (bi, 0, 0)
    per_b4 = lambda bi, i: (bi, 0, 0, 0)
    tile = lambda wd: pl.BlockSpec((tg, wd), lambda bi, i: (i, bi))
    st4 = pl.BlockSpec((None, NH, DH, DH), per_b4)
    row = pl.BlockSpec((None, 1, BW), per_b3)
    gconv = pl.BlockSpec((None, CONV_W - 1, 3 * BW), per_b3)
    mconv = pl.BlockSpec((None, CONV_W - 1, 2 * BW), per_b3)
    gate = pl.BlockSpec((1, 128), c2)
    return pl.pallas_call(
        _chunked_body,
        grid=(b, t // tg),
        in_specs=[
            tile(W_GDN), st4, gconv, pl.BlockSpec((CONV_W, 3 * BW), c2), gate, gate, pl.BlockSpec((1, BW), c2),
            tile(W_ML), st4, row, row, mconv, pl.BlockSpec((CONV_W, 2 * BW), c2), gate, gate,
        ],
        out_specs=[tile(BW), st4, gconv, tile(BW), st4, row, row, mconv],
        out_shape=[
            jax.ShapeDtypeStruct((t, b * BW), F32),
            jax.ShapeDtypeStruct((b, NH, DH, DH), F32),
            jax.ShapeDtypeStruct((b, CONV_W - 1, 3 * BW), F32),
            jax.ShapeDtypeStruct((t, b * BW), F32),
            jax.ShapeDtypeStruct((b, NH, DH, DH), F32),
            jax.ShapeDtypeStruct((b, 1, BW), F32),
            jax.ShapeDtypeStruct((b, 1, BW), F32),
            jax.ShapeDtypeStruct((b, CONV_W - 1, 2 * BW), F32),
        ],
        scratch_shapes=[
            pltpu.VMEM((tg + 8, 3 * BW), F32), pltpu.VMEM((tg, BW), F32), pltpu.VMEM((DH, BW), F32),
            pltpu.VMEM((tg + 8, 2 * BW), F32), pltpu.VMEM((tg, BW), F32), pltpu.VMEM((DH, BW), F32),
        ],
        compiler_params=_cparams(("arbitrary", "arbitrary")),
        name="chunked",
    )(pg, s0, gc0, gcw, neg_a, dt_bias, gnw, pm, c0m, n0, m0, mc0, mcw, b_i, b_f)


def _transpose_rows(x):
    eye = (_iota((BW, BW), 0) == _iota((BW, BW), 1)).astype(BF16)
    return _sel_dot(eye, x, _NT)


def _full(shape):
    nd = len(shape)
    return pl.BlockSpec(shape, lambda i: (0,) * nd)


def _head_rows(ref, h):
    return ref[pl.ds(pl.multiple_of(h * DH, DH), DH), :]


def _gdn_step_body(p_ref, s_ref, c_ref, cw_ref, nega_ref, dtb_ref, nw_ref, out_ref, so_ref, co_ref,
                   kt_s, qet_s, wt_s, egt_s, vbt_s, qkt_s, ot_s):
    h = pl.program_id(0)

    @pl.when(h == 0)
    def _():
        x = p_ref[:, 0:3 * BW]
        xa = x * cw_ref[CONV_W - 1:CONV_W, :]
        for j in range(CONV_W - 1):
            xa = xa + c_ref[j] * cw_ref[j:j + 1, :]
        co_ref[0] = c_ref[1]
        co_ref[1] = c_ref[2]
        co_ref[2] = x
        qkv = _silu(xa)
        ones_h = _head_ones()
        q = qkv[:, 0:BW]
        k = qkv[:, BW:2 * BW]
        q = q * lax.rsqrt(_dot_sel(q * q, ones_h) + EPS) * (DH ** -0.5)
        k = k * lax.rsqrt(_dot_sel(k * k, ones_h) + EPS)
        v = qkv[:, 2 * BW:3 * BW]
        ga = p_ref[:, 4 * BW:4 * BW + 128]
        g = nega_ref[...] * _softplus(ga + dtb_ref[...])
        eg = jnp.exp(_dot_sel(g, _expand_sel(0)))
        be = _dot_sel(_sigmoid(ga), _expand_sel(NH))
        kt_s[...] = _transpose_rows(k)
        qet_s[...] = _transpose_rows(q * eg)
        wt_s[...] = _transpose_rows(k * be * eg)
        egt_s[...] = _transpose_rows(eg)
        vbt_s[...] = _transpose_rows(v * be)
        qkt_s[...] = _transpose_rows(_dot_sel(q * k, ones_h))

    r0 = pl.multiple_of(h * DH, DH)

    def reduce_k(kk, acc):
        aw, aq = acc
        s = s_ref[kk]
        return aw + s * wt_s[pl.ds(r0 + kk, 1), :], aq + s * qet_s[pl.ds(r0 + kk, 1), :]

    nbl = s_ref.shape[-1]
    zero = jnp.zeros((DH, nbl), F32)
    aw, aq = lax.fori_loop(0, DH, reduce_k, (zero, zero), unroll=8)
    v_new = _head_rows(vbt_s, h) - aw
    ot_s[pl.ds(r0, DH), :] = aq + qkt_s[pl.ds(r0, 1), :] * v_new

    def update_k(kk, carry):
        so_ref[kk] = s_ref[kk] * egt_s[pl.ds(r0 + kk, 1), :] + kt_s[pl.ds(r0 + kk, 1), :] * v_new
        return carry

    lax.fori_loop(0, DH, update_k, 0, unroll=8)

    @pl.when(h == NH - 1)
    def _():
        o = ot_s[...].T
        ms = _dot_sel(o * o, _head_ones()) * (1.0 / DH)
        out_ref[...] = (o * lax.rsqrt(ms + EPS)) * nw_ref[...] * _silu(p_ref[:, 3 * BW:4 * BW])


def _lanes_block(nb, l):
    return pl.BlockSpec((None, None, DH, DH, nb), lambda h: (l, h, 0, 0, 0))


def _gdn_step(p, s_all, s_acc, l, c0, conv_w, neg_a, dt_bias, norm_w):
    nb = p.shape[0]
    ins = (p, s_all, c0, conv_w, neg_a, dt_bias, norm_w)
    in_specs = [_full(a.shape) for a in ins]
    in_specs[1] = _lanes_block(nb, l)
    if s_acc is None:
        extra, specs_x, alias, body = (), [], {}, _gdn_step_body
    else:
        extra, specs_x, alias = (s_acc,), [pl.BlockSpec(memory_space=pl.ANY)], {7: 1}
        body = lambda *refs: _gdn_step_body(*refs[:7], *refs[8:])
    return pl.pallas_call(
        body,
        grid=(NH,),
        in_specs=in_specs + specs_x,
        out_specs=[_full((nb, BW)), _lanes_block(nb, l), _full((CONV_W - 1, nb, 3 * BW))],
        out_shape=[
            jax.ShapeDtypeStruct((nb, BW), F32),
            jax.ShapeDtypeStruct(s_all.shape, F32),
            jax.ShapeDtypeStruct((CONV_W - 1, nb, 3 * BW), F32),
        ],
        input_output_aliases=alias,
        scratch_shapes=[pltpu.VMEM((BW, nb), F32) for _ in range(7)],
        compiler_params=_cparams(("arbitrary",)),
        name="gdn_step",
    )(*ins, *extra)


def _ml_step_body(p_ref, c_ref, n_ref, m_ref, cv_ref, cw_ref, bi_ref, bf_ref, out_ref, co_ref, no_ref, mo_ref, cvo_ref,
                  qwt_s, qt_s, wit_s, wkt_s, vt_s, st_s, flt_s, nt_s, ht_s, nto_s):
    h = pl.program_id(0)

    @pl.when(h == 0)
    def _():
        x = p_ref[:, 0:2 * BW]
        xa = x * cw_ref[CONV_W - 1:CONV_W, :]
        for j in range(CONV_W - 1):
            xa = xa + cv_ref[j] * cw_ref[j:j + 1, :]
        cvo_ref[0] = cv_ref[1]
        cvo_ref[1] = cv_ref[2]
        cvo_ref[2] = x
        qk = _silu(xa)
        q = qk[:, 0:BW]
        k = qk[:, BW:2 * BW] * (DH ** -0.5)
        ga = p_ref[:, 5 * BW:5 * BW + 128]
        ip = _dot_sel(ga + bi_ref[...], _expand_sel(0))
        lf = _dot_sel(-_softplus(-(ga + bf_ref[...])), _expand_sel(NH))
        mm = _dot_sel(m_ref[...], _expand_sel(0))
        inter = lf + mm
        m_t = jnp.maximum(inter, ip)
        w_int = jnp.exp(inter - m_t)
        wgt = jnp.exp(ip - m_t)
        mo_ref[...] = m_t
        qwt_s[...] = _transpose_rows(q * w_int)
        qt_s[...] = _transpose_rows(q)
        wit_s[...] = _transpose_rows(w_int)
        wkt_s[...] = _transpose_rows(wgt * k)
        vt_s[...] = _transpose_rows(p_ref[:, 2 * BW:3 * BW])
        st_s[...] = _transpose_rows(_dot_sel(q * k, _head_ones()) * wgt)
        flt_s[...] = _transpose_rows(jnp.exp(-m_t))
        nt_s[...] = _transpose_rows(n_ref[...])

    r0 = pl.multiple_of(h * DH, DH)
    v_h = _head_rows(vt_s, h)
    w_row = wit_s[pl.ds(r0, 1), :]

    def one_d(d, acc):
        c = c_ref[d]
        co_ref[d] = c * w_row + wkt_s[pl.ds(r0 + d, 1), :] * v_h
        return acc + c * qwt_s[pl.ds(r0 + d, 1), :]

    nbl = c_ref.shape[-1]
    num = lax.fori_loop(0, DH, one_d, jnp.zeros((DH, nbl), F32), unroll=8)
    s_row = st_s[pl.ds(r0, 1), :]
    n_h = _head_rows(nt_s, h)
    qn = jnp.sum(_head_rows(qt_s, h) * n_h, axis=0, keepdims=True)
    den = w_row * qn + s_row
    ht_s[pl.ds(r0, DH), :] = (num + s_row * v_h) / jnp.maximum(jnp.abs(den), flt_s[pl.ds(r0, 1), :])
    nto_s[pl.ds(r0, DH), :] = w_row * n_h + _head_rows(wkt_s, h)

    @pl.when(h == NH - 1)
    def _():
        out_ref[...] = ht_s[...].T * _sigmoid(p_ref[:, 3 * BW:4 * BW]) * _silu(p_ref[:, 4 * BW:5 * BW])
        no_ref[...] = nto_s[...].T


def _ml_step(p, c_all, c_acc, l, n0, m0, c0, conv_w, b_i, b_f):
    nb = p.shape[0]
    ins = (p, c_all, n0, m0, c0, conv_w, b_i, b_f)
    in_specs = [_full(a.shape) for a in ins]
    in_specs[1] = _lanes_block(nb, l)
    if c_acc is None:
        extra, specs_x, alias, body = (), [], {}, _ml_step_body
    else:
        extra, specs_x, alias = (c_acc,), [pl.BlockSpec(memory_space=pl.ANY)], {8: 1}
        body = lambda *refs: _ml_step_body(*refs[:8], *refs[9:])
    return pl.pallas_call(
        body,
        grid=(NH,),
        in_specs=in_specs + specs_x,
        out_specs=[_full((nb, BW)), _lanes_block(nb, l), _full((nb, BW)), _full((nb, BW)),
                   _full((CONV_W - 1, nb, 2 * BW))],
        out_shape=[
            jax.ShapeDtypeStruct((nb, BW), F32),
            jax.ShapeDtypeStruct(c_all.shape, F32),
            jax.ShapeDtypeStruct((nb, BW), F32),
            jax.ShapeDtypeStruct((nb, BW), F32),
            jax.ShapeDtypeStruct((CONV_W - 1, nb, 2 * BW), F32),
        ],
        input_output_aliases=alias,
        scratch_shapes=[pltpu.VMEM((BW, nb), F32) for _ in range(10)],
        compiler_params=_cparams(("arbitrary",)),
        name="mlstm_step",
    )(*ins, *extra)


def _pad_lanes(x, width):
    return jnp.pad(x, [(0, 0)] * (x.ndim - 1) + [(0, width - x.shape[-1])])


def _pack_w_in(w_in):
    seg = {}
    off = 0
    for name, width in (('lru_x', BW), ('lru_z', BW), ('gdn_q', BW), ('gdn_k', BW), ('gdn_v', BW), ('gdn_z', BW),
                        ('gdn_a', NH), ('gdn_b', NH), ('s5_u', BW), ('s5_z', BW),
                        ('ml_q', BW), ('ml_k', BW), ('ml_v', BW), ('ml_o', BW), ('ml_z', BW),
                        ('ml_i', NH), ('ml_f', NH), ('merge', NH * D_MODEL)):
        seg[name] = w_in[:, :, off:off + width]
        off += width
    zeros = lambda n: jnp.zeros(w_in.shape[:2] + (n,), w_in.dtype)
    return jnp.concatenate([
        seg['merge'],
        seg['lru_x'], seg['lru_z'],
        seg['s5_u'], seg['s5_z'],
        seg['gdn_q'], seg['gdn_k'], seg['gdn_v'], seg['gdn_z'], seg['gdn_a'], seg['gdn_b'], zeros(128 - 2 * NH),
        seg['ml_q'], seg['ml_k'], seg['ml_v'], seg['ml_o'], seg['ml_z'], seg['ml_i'], seg['ml_f'], zeros(128 - 2 * NH),
        zeros(W_PACK - sum(W_MIX)),
    ], axis=2).astype(BF16)


def _prep_layer(l, prm):
    eye4 = jnp.eye(4, dtype=F32)
    blockdiag = lambda w: jnp.einsum('ncd,nm->ncmd', w, eye4).reshape(BW, BW)
    lru_wg = jnp.concatenate([blockdiag(prm['lru_w_a'][l]), blockdiag(prm['lru_w_x'][l])], axis=1).astype(BF16)
    lru_bg = jnp.concatenate([prm['lru_b_a'][l], prm['lru_b_x'][l]])[None, :]
    lru_sp = jax.nn.softplus(-prm['lru_lambda'][l])[None, :]

    lam_re, lam_im = prm['s5_lam_re'][l], prm['s5_lam_im'][l]
    dt = jnp.exp(prm['s5_log_dt'][l])[:, None]
    mag = jnp.exp(lam_re * dt)
    ang = lam_im * dt
    ar, ai = mag * jnp.cos(ang), mag * jnp.sin(ang)
    den = lam_re * lam_re + lam_im * lam_im
    cr = ((ar - 1.0) * lam_re + ai * lam_im) / den
    ci = (ai * lam_re - (ar - 1.0) * lam_im) / den
    b_re, b_im = prm['s5_b_re'][l], prm['s5_b_im'][l]
    bb_re = cr[..., None] * b_re - ci[..., None] * b_im
    bb_im = cr[..., None] * b_im + ci[..., None] * b_re
    eye_g = jnp.eye(S5_G, dtype=F32)
    in_dense = lambda bb: jnp.einsum('gnc,gh->gchn', bb, eye_g).reshape(BW, S5_W).astype(BF16)
    out_dense = lambda cc: jnp.einsum('gcn,gh->gnhc', cc, eye_g).reshape(S5_W, BW).astype(BF16)

    gate_row = lambda a, b: _pad_lanes(jnp.concatenate([a, b])[None, :], 128)
    zeros4 = jnp.zeros((NH,), F32)
    return dict(
        norm_w=prm['norm_w'][l][None, :],
        w_branch=prm['w_branch'][l].astype(BF16),
        w_out=prm['w_out'][l].astype(BF16),
        lru_cw=prm['lru_conv_w'][l], lru_wg=lru_wg, lru_bg=lru_bg, lru_sp=lru_sp,
        s5_ar=ar.reshape(1, S5_W), s5_ai=ai.reshape(1, S5_W),
        s5_bre=in_dense(bb_re), s5_bim=in_dense(bb_im),
        s5_cre=out_dense(prm['s5_c_re'][l]), s5_cim=out_dense(prm['s5_c_im'][l]),
        s5_d=prm['s5_d'][l][None, :], s5_wglu=prm['s5_w_glu'][l].astype(BF16),
        gdn_cw=prm['gdn_conv_w'][l],
        gdn_nega=gate_row(-jnp.exp(prm['gdn_A_log'][l]), zeros4),
        gdn_dtb=gate_row(prm['gdn_dt_bias'][l], zeros4),
        gdn_nw=jnp.tile(prm['gdn_norm_w'][l], NH)[None, :],
        ml_cw=prm['ml_conv_w'][l],
        ml_bi=gate_row(prm['ml_b_i'][l], zeros4),
        ml_bf=gate_row(zeros4, prm['ml_b_f'][l]),
    )


def _prompt_layer(x, pp, w_pack, l, final_w, final, tm, tt, tg):
    b, t, _ = x.shape
    p_lru, p_s5, p_gdn, p_ml = _in_proj(x, pp['norm_w'], w_pack, l, tm, True)
    z = lambda *s: jnp.zeros(s, F32)
    out_a, lru_h, lru_c = _lru(p_lru, z(b, BW), z(CONV_W - 1, b, BW),
                               pp['lru_cw'], pp['lru_wg'], pp['lru_bg'], pp['lru_sp'], tt)
    out_c, s5_re, s5_im = _s5(p_s5, z(b, S5_W), z(b, S5_W), pp['s5_ar'], pp['s5_ai'],
                              pp['s5_bre'], pp['s5_bim'], pp['s5_cre'], pp['s5_cim'], pp['s5_d'], pp['s5_wglu'], tt)
    out_b, gdn_s, gdn_c, out_d, ml_c, ml_n, ml_m, ml_cv = _chunked(
        p_gdn, z(b, NH, DH, DH), z(b, CONV_W - 1, 3 * BW), pp['gdn_cw'], pp['gdn_nega'], pp['gdn_dtb'], pp['gdn_nw'],
        p_ml, z(b, NH, DH, DH), z(b, 1, BW), z(b, 1, BW), z(b, CONV_W - 1, 2 * BW), pp['ml_cw'], pp['ml_bi'], pp['ml_bf'],
        b, tg)
    y = _merge(x, (out_a, out_b, out_c, out_d), pp['norm_w'], w_pack, l,
               pp['w_branch'], pp['w_out'], final_w, tm, final)
    st = dict(lru_h=lru_h, lru_conv=jnp.transpose(lru_c, (1, 0, 2)), gdn_S=gdn_s, gdn_conv=gdn_c,
              s5_re=s5_re.reshape(b, S5_G, S5_N), s5_im=s5_im.reshape(b, S5_G, S5_N),
              ml_C=ml_c, ml_n=ml_n.reshape(b, NH, DH), ml_m=ml_m.reshape(b, NH, DH)[:, :, 0], ml_conv=ml_cv)
    return y, st


BIG_STATES = ('gdn_S', 'ml_C')
TO_LANES = (0, 2, 3, 4, 1)
FROM_LANES = (0, 4, 1, 2, 3)


def _sample_layer(x, st, big_in, big_acc, l, pp, w_pack, final_w, final):
    nb = x.shape[1]
    p_lru, p_s5, p_gdn, p_ml = _in_proj(x, pp['norm_w'], w_pack, l, nb, False)
    tmaj = lambda c: jnp.transpose(c, (1, 0, 2))
    out_a, lru_h, lru_c = _lru(p_lru.reshape(1, nb, W_LRU), st['lru_h'], tmaj(st['lru_conv']),
                               pp['lru_cw'], pp['lru_wg'], pp['lru_bg'], pp['lru_sp'], 1)
    out_c, s5_re, s5_im = _s5(p_s5.reshape(1, nb, W_S5), st['s5_re'].reshape(nb, S5_W), st['s5_im'].reshape(nb, S5_W),
                              pp['s5_ar'], pp['s5_ai'], pp['s5_bre'], pp['s5_bim'], pp['s5_cre'], pp['s5_cim'],
                              pp['s5_d'], pp['s5_wglu'], 1)
    out_b, gdn_s, gdn_c = _gdn_step(p_gdn, big_in['gdn_S'], big_acc['gdn_S'], l, tmaj(st['gdn_conv']), pp['gdn_cw'],
                                    pp['gdn_nega'], pp['gdn_dtb'], pp['gdn_nw'])
    out_d, ml_c, ml_n, ml_m, ml_cv = _ml_step(p_ml, big_in['ml_C'], big_acc['ml_C'], l,
                                              st['ml_n'].reshape(nb, BW), _pad_lanes(st['ml_m'], 128),
                                              tmaj(st['ml_conv']), pp['ml_cw'], pp['ml_bi'], pp['ml_bf'])
    y = _merge(x, (out_a.reshape(nb, BW), out_b, out_c.reshape(nb, BW), out_d),
               pp['norm_w'], w_pack, l, pp['w_branch'], pp['w_out'], final_w, nb, final)
    new = dict(lru_h=lru_h, lru_conv=tmaj(lru_c), gdn_conv=tmaj(gdn_c),
               s5_re=s5_re.reshape(nb, S5_G, S5_N), s5_im=s5_im.reshape(nb, S5_G, S5_N),
               ml_n=ml_n.reshape(nb, NH, DH), ml_m=ml_m.reshape(nb, NH, DH)[:, :, 0], ml_conv=tmaj(ml_cv))
    return y, new, dict(gdn_S=gdn_s, ml_C=ml_c)


STATE_NAMES = ('lru_h', 'lru_conv', 'gdn_S', 'gdn_conv', 's5_re', 's5_im', 'ml_C', 'ml_n', 'ml_m', 'ml_conv')

TM_ROWS = 64
TT_SCAN = 128


def kernel(x_prompt, x_sample, state_lru_h, state_lru_conv, state_gdn_S, state_gdn_conv, state_s5_re, state_s5_im, state_ml_C, state_ml_n, state_ml_m, state_ml_conv, norm_w, w_in, lru_conv_w, lru_w_a, lru_b_a, lru_w_x, lru_b_x, lru_lambda, gdn_conv_w, gdn_A_log, gdn_dt_bias, gdn_norm_w, s5_lam_re, s5_lam_im, s5_log_dt, s5_b_re, s5_b_im, s5_c_re, s5_c_im, s5_d, s5_w_glu, ml_conv_w, ml_b_i, ml_b_f, w_branch, w_out, final_norm_w):
    prm = dict(norm_w=norm_w, w_in=w_in, lru_conv_w=lru_conv_w, lru_w_a=lru_w_a, lru_b_a=lru_b_a, lru_w_x=lru_w_x,
               lru_b_x=lru_b_x, lru_lambda=lru_lambda, gdn_conv_w=gdn_conv_w, gdn_A_log=gdn_A_log,
               gdn_dt_bias=gdn_dt_bias, gdn_norm_w=gdn_norm_w, s5_lam_re=s5_lam_re, s5_lam_im=s5_lam_im,
               s5_log_dt=s5_log_dt, s5_b_re=s5_b_re, s5_b_im=s5_b_im, s5_c_re=s5_c_re, s5_c_im=s5_c_im, s5_d=s5_d,
               s5_w_glu=s5_w_glu, ml_conv_w=ml_conv_w, ml_b_i=ml_b_i, ml_b_f=ml_b_f, w_branch=w_branch, w_out=w_out)
    depth = w_in.shape[0]
    t = x_prompt.shape[1]
    nb = x_sample.shape[0]
    final_w = final_norm_w[None, :]
    st_in = dict(lru_h=state_lru_h, lru_conv=state_lru_conv, gdn_S=state_gdn_S, gdn_conv=state_gdn_conv,
                 s5_re=state_s5_re, s5_im=state_s5_im, ml_C=state_ml_C, ml_n=state_ml_n, ml_m=state_ml_m,
                 ml_conv=state_ml_conv)
    tm = min(TM_ROWS, t)
    tt = min(TT_SCAN, t)
    tg = min(SE, t)

    xp = x_prompt
    xs = x_sample.reshape(1, nb, D_MODEL)
    small = [n for n in STATE_NAMES if n not in BIG_STATES]
    new_p = {n: [] for n in STATE_NAMES}
    new_s = {n: [] for n in small}
    big_in = {n: jnp.transpose(st_in[n], TO_LANES) for n in BIG_STATES}
    big_acc = {n: None for n in BIG_STATES}
    w_pack = _pack_w_in(w_in)
    for l in range(depth):
        pp = _prep_layer(l, prm)
        final = l == depth - 1
        xp, sp = _prompt_layer(xp, pp, w_pack, l, final_w, final, tm, tt, tg)
        xs, ss, big_acc = _sample_layer(xs, {n: st_in[n][l] for n in small}, big_in, big_acc, l, pp, w_pack,
                                        final_w, final)
        for n in STATE_NAMES:
            new_p[n].append(sp[n])
        for n in small:
            new_s[n].append(ss[n])
    np_ = {n: jnp.stack(v) for n, v in new_p.items()}
    ns_ = {n: jnp.stack(v) for n, v in new_s.items()}
    ns_.update({n: jnp.transpose(big_acc[n], FROM_LANES) for n in BIG_STATES})
    y_sample = xs.reshape(nb, 1, D_MODEL)
    return (xp, y_sample,
            np_['lru_h'], ns_['lru_h'], np_['lru_conv'], ns_['lru_conv'],
            np_['gdn_S'], ns_['gdn_S'], np_['gdn_conv'], ns_['gdn_conv'],
            np_['s5_re'], ns_['s5_re'], np_['s5_im'], ns_['s5_im'],
            np_['ml_C'], ns_['ml_C'], np_['ml_n'], ns_['ml_n'], np_['ml_m'], ns_['ml_m'],
            np_['ml_conv'], ns_['ml_conv'])
```

```python
import functools
import math

import jax
import jax.numpy as jnp
from jax import lax
from jax.experimental import pallas as pl
from jax.experimental.pallas import tpu as pltpu

F32 = jnp.float32
BF16 = jnp.bfloat16

D_MODEL = 1024
BW = 256
NH = 4
DH = 64
CONV_W = 4
CHUNK = 64
LRU_C = 8.0
S5_G = 16
S5_N = 64
S5_W = S5_G * S5_N
EPS = 1e-6

W_LRU = 2 * BW
W_S5 = 2 * BW
W_GDN = 4 * BW + 128
W_ML = 5 * BW + 128
W_MIX = (W_LRU, W_S5, W_GDN, W_ML)

VMEM_LIMIT_BYTES = 56 * 1024 * 1024


def _cparams(sem):
    return pltpu.CompilerParams(dimension_semantics=sem, vmem_limit_bytes=VMEM_LIMIT_BYTES)


def _sigmoid(x):
    return 1.0 / (1.0 + jnp.exp(-x))


def _silu(x):
    return x * _sigmoid(x)


def _softplus(x):
    return jnp.maximum(x, 0.0) + jnp.log(1.0 + jnp.exp(-jnp.abs(x)))


def _gelu_tanh(x):
    c = math.sqrt(2.0 / math.pi)
    return x * (0.5 * (1.0 + jnp.tanh(c * (x + 0.044715 * (x * x * x)))))


def _split3(x):
    hi = x.astype(BF16)
    r = x - hi.astype(F32)
    mid = r.astype(BF16)
    lo = (r - mid.astype(F32)).astype(BF16)
    return hi, mid, lo


def _split2(x):
    hi = x.astype(BF16)
    return hi, (x - hi.astype(F32)).astype(BF16)


_NN = (((1,), (0,)), ((), ()))
_NT = (((1,), (1,)), ((), ()))
_TN = (((0,), (0,)), ((), ()))


def _sel_dot(sel, x, dims=_NN, split=_split3):
    out = None
    for p in split(x):
        t = lax.dot_general(sel, p, dims, preferred_element_type=F32)
        out = t if out is None else out + t
    return out


def _dot_sel(x, sel, dims=_NN, split=_split3):
    out = None
    for p in split(x):
        t = lax.dot_general(p, sel, dims, preferred_element_type=F32)
        out = t if out is None else out + t
    return out


def _iota(shape, axis):
    return lax.broadcasted_iota(jnp.int32, shape, axis)


def _head_ones():
    return (_iota((BW, BW), 0) // DH == _iota((BW, BW), 1) // DH).astype(BF16)


def _expand_sel(offset):
    return (_iota((128, BW), 0) == _iota((128, BW), 1) // DH + offset).astype(BF16)


def _rms_rows(x, w_row):
    ms = jnp.mean(x * x, axis=-1, keepdims=True)
    return (x * lax.rsqrt(ms + EPS)) * w_row


def _inproj_body(x_ref, nw_ref, w_ref, *o_refs):
    nb, tm, _ = x_ref.shape
    xn = _rms_rows(x_ref[...].reshape(nb * tm, D_MODEL), nw_ref[...]).astype(BF16)
    off = 0
    for o_ref in o_refs:
        wd = o_ref.shape[-1] if len(o_ref.shape) == 3 else o_ref.shape[-1] // nb
        y = jnp.dot(xn, w_ref[:, off:off + wd], preferred_element_type=F32)
        if len(o_ref.shape) == 3:
            o_ref[...] = jnp.swapaxes(y.reshape(nb, tm, wd), 0, 1)
        else:
            for bi in range(nb):
                o_ref[:, bi * wd:(bi + 1) * wd] = y[bi * tm:(bi + 1) * tm]
        off += wd


W_PACK = NH * D_MODEL


def _in_proj(x, norm_w, w_pack, l, tm, scan_3d):
    b, t, _ = x.shape
    assert sum(W_MIX) <= W_PACK
    specs, shapes = [], []
    for n, wd in enumerate(W_MIX):
        if scan_3d and n < 2:
            specs.append(pl.BlockSpec((tm, b, wd), lambda i: (i, 0, 0)))
            shapes.append(jax.ShapeDtypeStruct((t, b, wd), F32))
        else:
            specs.append(pl.BlockSpec((tm, b * wd), lambda i: (i, 0)))
            shapes.append(jax.ShapeDtypeStruct((t, b * wd), F32))
    return pl.pallas_call(
        _inproj_body,
        grid=(t // tm,),
        in_specs=[
            pl.BlockSpec((b, tm, D_MODEL), lambda i: (0, i, 0)),
            pl.BlockSpec((1, D_MODEL), lambda i: (0, 0)),
            pl.BlockSpec((None, D_MODEL, W_PACK), lambda i: (l, 0, 1)),
        ],
        out_specs=specs,
        out_shape=shapes,
        compiler_params=_cparams(("arbitrary",)),
        name="in_proj",
    )(x, norm_w, w_pack)


def _merge_body(x_ref, a_ref, b_ref, c_ref, d_ref, nw_ref, wm_ref, wb_ref, wo_ref, fw_ref, y_ref, *, final):
    nb, tm, _ = x_ref.shape
    x = x_ref[...].reshape(nb * tm, D_MODEL)
    xn = _rms_rows(x, nw_ref[...]).astype(BF16)
    merged = None
    for n, r in enumerate((a_ref, b_ref, c_ref, d_ref)):
        if len(r.shape) == 3:
            br = jnp.swapaxes(r[...], 0, 1).reshape(nb * tm, BW)
        else:
            br = jnp.concatenate([r[:, bi * BW:(bi + 1) * BW] for bi in range(nb)], axis=0)
        gate = _sigmoid(jnp.dot(xn, wm_ref[:, n * D_MODEL:(n + 1) * D_MODEL], preferred_element_type=F32))
        proj = jnp.dot(br.astype(BF16), wb_ref[n], preferred_element_type=F32)
        merged = gate * proj if merged is None else merged + gate * proj
    y = x + jnp.dot(merged.astype(BF16), wo_ref[...], preferred_element_type=F32)
    if final:
        y = _rms_rows(y, fw_ref[...])
    y_ref[...] = y.reshape(nb, tm, D_MODEL)


def _merge(x, branches, norm_w, w_pack, l, w_branch, w_out, final_w, tm, final):
    b, t, _ = x.shape
    const2 = lambda i: (0, 0)
    br_specs = [pl.BlockSpec((tm,) + r.shape[1:], (lambda i: (i, 0, 0)) if r.ndim == 3 else (lambda i: (i, 0)))
                for r in branches]
    return pl.pallas_call(
        functools.partial(_merge_body, final=final),
        grid=(t // tm,),
        in_specs=[
            pl.BlockSpec((b, tm, D_MODEL), lambda i: (0, i, 0)),
            *br_specs,
            pl.BlockSpec((1, D_MODEL), const2),
            pl.BlockSpec((None, D_MODEL, W_PACK), lambda i: (l, 0, 0)),
            pl.BlockSpec((NH, BW, D_MODEL), lambda i: (0, 0, 0)),
            pl.BlockSpec((D_MODEL, D_MODEL), const2),
            pl.BlockSpec((1, D_MODEL), const2),
        ],
        out_specs=pl.BlockSpec((b, tm, D_MODEL), lambda i: (0, i, 0)),
        out_shape=jax.ShapeDtypeStruct((b, t, D_MODEL), F32),
        compiler_params=_cparams(("arbitrary",)),
        name="merge",
    )(x, *branches, norm_w, w_pack, w_branch, w_out, final_w)


def _lru_body(p_ref, h0_ref, c0_ref, cw_ref, wg_ref, bg_ref, sp_ref, out_ref, ht_ref, ct_ref,
              xp_s, a_s, b_s, h_s):
    tt, nb = p_ref.shape[0], p_ref.shape[1]

    @pl.when(pl.program_id(0) == 0)
    def _():
        h_s[...] = h0_ref[...]
        xp_s[0:CONV_W - 1] = c0_ref[...]

    xp_s[CONV_W - 1:CONV_W - 1 + tt] = p_ref[:, :, 0:BW]
    xa = None
    for j in range(CONV_W):
        term = xp_s[j:j + tt] * cw_ref[j:j + 1, :].reshape(1, 1, BW)
        xa = term if xa is None else xa + term
    tail = xp_s[tt:tt + CONV_W - 1]
    xp_s[0:CONV_W - 1] = tail
    ct_ref[...] = tail

    xa2 = xa.reshape(tt * nb, BW)
    gates = jnp.dot(xa2.astype(BF16), wg_ref[...], preferred_element_type=F32) + bg_ref[...]
    r = _sigmoid(gates[:, 0:BW])
    ig = _sigmoid(gates[:, BW:2 * BW])
    log_a = (-LRU_C * r) * sp_ref[...]
    a = jnp.exp(log_a)
    bb = jnp.sqrt(1.0 - jnp.exp(2.0 * log_a)) * (ig * xa2)
    a_s[...] = a.reshape(tt, nb, BW)
    b_s[...] = bb.reshape(tt, nb, BW)

    def step(t, h):
        h = a_s[t] * h + b_s[t]
        b_s[t] = h
        return h

    h = lax.fori_loop(0, tt, step, h_s[...], unroll=min(8, tt))
    h_s[...] = h
    ht_ref[...] = h
    out_ref[...] = b_s[...] * _silu(p_ref[:, :, BW:2 * BW])


def _lru(p, h0, c0, conv_w, w_gates, b_gates, sp_lam, tt):
    t, nb, _ = p.shape
    c2 = lambda i: (0, 0)
    c3 = lambda i: (0, 0, 0)
    return pl.pallas_call(
        _lru_body,
        grid=(t // tt,),
        in_specs=[
            pl.BlockSpec((tt, nb, W_LRU), lambda i: (i, 0, 0)),
            pl.BlockSpec((nb, BW), c2),
            pl.BlockSpec((CONV_W - 1, nb, BW), c3),
            pl.BlockSpec((CONV_W, BW), c2),
            pl.BlockSpec((BW, 2 * BW), c2),
            pl.BlockSpec((1, 2 * BW), c2),
            pl.BlockSpec((1, BW), c2),
        ],
        out_specs=[
            pl.BlockSpec((tt, nb, BW), lambda i: (i, 0, 0)),
            pl.BlockSpec((nb, BW), c2),
            pl.BlockSpec((CONV_W - 1, nb, BW), c3),
        ],
        out_shape=[
            jax.ShapeDtypeStruct((t, nb, BW), F32),
            jax.ShapeDtypeStruct((nb, BW), F32),
            jax.ShapeDtypeStruct((CONV_W - 1, nb, BW), F32),
        ],
        scratch_shapes=[
            pltpu.VMEM((tt + CONV_W - 1, nb, BW), F32),
            pltpu.VMEM((tt, nb, BW), F32),
            pltpu.VMEM((tt, nb, BW), F32),
            pltpu.VMEM((nb, BW), F32),
        ],
        compiler_params=_cparams(("arbitrary",)),
        name="rglru",
    )(p, h0, c0, conv_w, w_gates, b_gates, sp_lam)


def _s5_body(p_ref, hr0_ref, hi0_ref, ar_ref, ai_ref, bre_ref, bim_ref, cre_ref, cim_ref, d_ref, wglu_ref,
             out_ref, hrt_ref, hit_ref, xr_s, xi_s, hr_s, hi_s):
    tt, nb = p_ref.shape[0], p_ref.shape[1]

    @pl.when(pl.program_id(0) == 0)
    def _():
        hr_s[...] = hr0_ref[...]
        hi_s[...] = hi0_ref[...]

    u2 = p_ref[:, :, 0:BW].reshape(tt * nb, BW)
    ub = u2.astype(BF16)
    xr_s[...] = jnp.dot(ub, bre_ref[...], preferred_element_type=F32).reshape(tt, nb, S5_W)
    xi_s[...] = jnp.dot(ub, bim_ref[...], preferred_element_type=F32).reshape(tt, nb, S5_W)
    ar = jnp.broadcast_to(ar_ref[...], (nb, S5_W))
    ai = jnp.broadcast_to(ai_ref[...], (nb, S5_W))

    def step(t, c):
        hr, hi = c
        nr = ar * hr - ai * hi + xr_s[t]
        ni = ar * hi + ai * hr + xi_s[t]
        xr_s[t] = nr
        xi_s[t] = ni
        return nr, ni

    hr, hi = lax.fori_loop(0, tt, step, (hr_s[...], hi_s[...]), unroll=min(4, tt))
    hr_s[...] = hr
    hi_s[...] = hi
    hrt_ref[...] = hr
    hit_ref[...] = hi

    y = (jnp.dot(xr_s[...].reshape(tt * nb, S5_W).astype(BF16), cre_ref[...], preferred_element_type=F32)
         - jnp.dot(xi_s[...].reshape(tt * nb, S5_W).astype(BF16), cim_ref[...], preferred_element_type=F32)
         + d_ref[...] * u2)
    y = _gelu_tanh(y)
    y = y * _sigmoid(jnp.dot(y.astype(BF16), wglu_ref[...], preferred_element_type=F32))
    out_ref[...] = y.reshape(tt, nb, BW) * _silu(p_ref[:, :, BW:2 * BW])


def _s5(p, hr0, hi0, ar, ai, bre, bim, cre, cim, d, wglu, tt):
    t, nb, _ = p.shape
    c2 = lambda i: (0, 0)
    st = pl.BlockSpec((nb, S5_W), c2)
    row = pl.BlockSpec((1, S5_W), c2)
    return pl.pallas_call(
        _s5_body,
        grid=(t // tt,),
        in_specs=[
            pl.BlockSpec((tt, nb, W_S5), lambda i: (i, 0, 0)),
            st, st, row, row,
            pl.BlockSpec((BW, S5_W), c2), pl.BlockSpec((BW, S5_W), c2),
            pl.BlockSpec((S5_W, BW), c2), pl.BlockSpec((S5_W, BW), c2),
            pl.BlockSpec((1, BW), c2),
            pl.BlockSpec((BW, BW), c2),
        ],
        out_specs=[pl.BlockSpec((tt, nb, BW), lambda i: (i, 0, 0)), st, st],
        out_shape=[
            jax.ShapeDtypeStruct((t, nb, BW), F32),
            jax.ShapeDtypeStruct((nb, S5_W), F32),
            jax.ShapeDtypeStruct((nb, S5_W), F32),
        ],
        scratch_shapes=[
            pltpu.VMEM((tt, nb, S5_W), F32),
            pltpu.VMEM((tt, nb, S5_W), F32),
            pltpu.VMEM((nb, S5_W), F32),
            pltpu.VMEM((nb, S5_W), F32),
        ],
        compiler_params=_cparams(("arbitrary",)),
        name="s5",
    )(p, hr0, hi0, ar, ai, bre, bim, cre, cim, d, wglu)


def _conv_rows(xp_s, p_cols, cw_ref, tg):
    xp_s[8:8 + tg] = p_cols
    y = None
    for j in range(CONV_W):
        term = xp_s[8 - (CONV_W - 1) + j:8 - (CONV_W - 1) + j + tg] * cw_ref[j:j + 1, :]
        y = term if y is None else y + term
    tail = xp_s[8 + tg - (CONV_W - 1):8 + tg]
    xp_s[8 - (CONV_W - 1):8] = tail
    return y, tail


SE = NH * CHUNK


def _se_masks():
    r = _iota((SE, SE), 0)
    c = _iota((SE, SE), 1)
    same = (r // CHUNK) == (c // CHUNK)
    return same, same & ((r % CHUNK) >= (c % CHUNK))


def _cols_masks(n):
    i = _iota((n, BW), 0) % CHUNK
    j = _iota((n, BW), 1) % CHUNK
    return i >= j, i > j, i == j


def _stack_se(x_bf, same):
    return jnp.where(same, jnp.concatenate([x_bf] * NH, axis=0), jnp.zeros((), x_bf.dtype))


def _tile_rows(x):
    return jnp.concatenate([x] * NH, axis=0)


def _fold_se(x_se):
    out = x_se[0:CHUNK]
    for h in range(1, NH):
        out = out + x_se[h * CHUNK:(h + 1) * CHUNK]
    return out


def _load_cols(s_s, blocks_ref):
    for h in range(NH):
        s_s[:, h * DH:(h + 1) * DH] = blocks_ref[h]


def _store_cols(blocks_ref, s_s):
    for h in range(NH):
        blocks_ref[h] = s_s[:, h * DH:(h + 1) * DH]


def _last_rows(x, nc):
    return jnp.concatenate([jnp.broadcast_to(x[c * CHUNK + CHUNK - 1:(c + 1) * CHUNK], (CHUNK, BW)) for c in range(nc)], axis=0)


def _gdn_init(s0_ref, c0_ref, xp_s, s_s):
    _load_cols(s_s, s0_ref)
    xp_s[8 - (CONV_W - 1):8] = c0_ref[...]


def _gdn_main(p_ref, cw_ref, nega_ref, dtb_ref, nw_ref, out_ref, st_ref, ct_ref, xp_s, o_s, s_s):
    tg = p_ref.shape[0]
    nc = tg // CHUNK
    y, tail = _conv_rows(xp_s, p_ref[:, 0:3 * BW], cw_ref, tg)
    ct_ref[...] = tail
    qkv = _silu(y)
    ones_h = _head_ones()
    q = qkv[:, 0:BW]
    k = qkv[:, BW:2 * BW]
    q = q * lax.rsqrt(_dot_sel(q * q, ones_h, split=_split2) + EPS) * (DH ** -0.5)
    k = k * lax.rsqrt(_dot_sel(k * k, ones_h, split=_split2) + EPS)
    v = qkv[:, 2 * BW:3 * BW]
    yield
    ga = p_ref[:, 4 * BW:4 * BW + 128]
    lane = _iota((tg, 128), 1)
    g = nega_ref[...] * _softplus(ga + dtb_ref[...])
    gall = jnp.where(lane < NH, g, _sigmoid(ga))

    same, tri = _se_masks()
    tri_c, strict_c, _ = _cols_masks(tg)
    tri_bf = tri.astype(BF16)
    beta_x = _dot_sel(gall, _expand_sel(NH), split=_split2)
    g_x = _dot_sel(gall, _expand_sel(0), split=_split2)
    gc_x = _sel_dot(tri_bf, g_x, split=_split2)
    gl_x = _last_rows(gc_x, nc)
    egc = jnp.exp(gc_x)
    kb = k * beta_x
    k_b = k.astype(BF16)
    kb_b = kb.astype(BF16)
    q_b = q.astype(BF16)
    vb_b = (v * beta_x).astype(BF16)
    kbg_b = (kb * egc).astype(BF16)
    qd_b = (q * egc).astype(BF16)
    kd_b = (k * jnp.exp(gl_x - gc_x)).astype(BF16)
    yield

    cs = range(nc)
    rows = [slice(c * CHUNK, (c + 1) * CHUNK) for c in cs]
    diff = _sel_dot(tri_bf, jnp.where(strict_c, g_x, 0.0), split=_split2)
    dec_all = jnp.where(tri_c, jnp.exp(diff), 0.0)
    dec_t = [_tile_rows(dec_all[rows[c]]) for c in cs]
    ndec_all = jnp.where(strict_c, -dec_all, 0.0)
    ndec_t = [_tile_rows(ndec_all[rows[c]]) for c in cs]

    dot = functools.partial(jnp.dot, preferred_element_type=F32)
    cat = functools.partial(jnp.concatenate, axis=0)
    k_se = [_stack_se(k_b[rows[c]], same) for c in cs]
    kq = [lax.dot_general(cat([_stack_se(kb_b[rows[c]], same), _stack_se(q_b[rows[c]], same)]), k_se[c], _NT,
                          preferred_element_type=F32) for c in cs]
    attn = [(kq[c][SE:] * dec_t[c]).astype(BF16) for c in cs]
    eye = (_iota((SE, SE), 0) == _iota((SE, SE), 1)).astype(F32)
    npow = [kq[c][:SE] * ndec_t[c] for c in cs]
    tinv = [eye + npow[c] for c in cs]
    nb = [n.astype(BF16) for n in npow]
    yield
    nb = [dot(n, n).astype(BF16) for n in nb]
    for _ in range(4):
        yield
        both = [dot(cat([t.astype(BF16), n]), n) for t, n in zip(tinv, nb)]
        tinv = [t + b2[:SE] for t, b2 in zip(tinv, both)]
        nb = [b2[SE:].astype(BF16) for b2 in both]
    yield
    tinv = [t + dot(t.astype(BF16), n) for t, n in zip(tinv, nb)]
    t_bf = [t.astype(BF16) for t in tinv]
    u = [dot(t_bf[c], _stack_se(vb_b[rows[c]], same)) for c in cs]
    w = [dot(t_bf[c], _stack_se(kbg_b[rows[c]], same)).astype(BF16) for c in cs]

    kd_t = [_stack_se(kd_b[rows[c]], same).T for c in cs]
    s = s_s[...]
    yield _READY
    for c in cs:
        sb = _stack_se(s.astype(BF16), same)
        ws = dot(cat([w[c], _stack_se(qd_b[rows[c]], same)]), sb)
        vb = (u[c] - ws[:SE]).astype(BF16)
        av = dot(cat([attn[c], kd_t[c]]), vb)
        s = s * jnp.exp(gl_x[c * CHUNK:c * CHUNK + 1]) + _fold_se(av[SE:])
        o_s[rows[c], :] = _fold_se(ws[SE:] + av[:SE])
        yield
    s_s[...] = s
    _store_cols(st_ref, s_s)

    o = o_s[...]
    ms = _dot_sel(o * o, ones_h, split=_split2) * (1.0 / DH)
    out_ref[...] = (o * lax.rsqrt(ms + EPS)) * nw_ref[...] * _silu(p_ref[:, 3 * BW:4 * BW])


def _cummax_rows(x):
    row = _iota(x.shape, 0)
    sh = 1
    while sh < x.shape[0]:
        x = jnp.maximum(x, jnp.where(row >= sh, pltpu.roll(x, sh, 0), -jnp.inf))
        sh *= 2
    return x


def _ml_init(c0m_ref, n0_ref, m0_ref, c0_ref, nt_ref, mt_ref, xp_s, cm_s):
    _load_cols(cm_s, c0m_ref)
    nt_ref[...] = n0_ref[...]
    mt_ref[...] = m0_ref[...]
    xp_s[8 - (CONV_W - 1):8] = c0_ref[...]


def _ml_main(p_ref, cw_ref, bi_ref, bf_ref, out_ref, ct_m_ref, nt_ref, mt_ref, cvt_ref, xp_s, h_s, cm_s):
    tg = p_ref.shape[0]
    nc = tg // CHUNK
    y, tail = _conv_rows(xp_s, p_ref[:, 0:2 * BW], cw_ref, tg)
    cvt_ref[...] = tail
    qk = _silu(y)
    q = qk[:, 0:BW]
    k = qk[:, BW:2 * BW] * (DH ** -0.5)
    yield
    ga = p_ref[:, 5 * BW:5 * BW + 128]
    lane = _iota((tg, 128), 1)
    gall = jnp.where(lane < NH, ga + bi_ref[...], -_softplus(-(ga + bf_ref[...])))

    same, tri = _se_masks()
    tri_c, strict_c, eye_c = _cols_masks(tg)
    tri_bf = tri.astype(BF16)
    ones_h = _head_ones()
    i_x = _dot_sel(gall, _expand_sel(0), split=_split2)
    lf_x = _dot_sel(gall, _expand_sel(NH), split=_split2)
    b_x = _sel_dot(tri_bf, lf_x, split=_split2)
    r_x = i_x - b_x
    e_x = _last_rows(b_x, nc) - b_x + i_x
    q_b = q.astype(BF16)
    k_b = k.astype(BF16)
    v_b = p_ref[:, 2 * BW:3 * BW].astype(BF16)
    yield

    cs = range(nc)
    rows = [slice(c * CHUNK, (c + 1) * CHUNK) for c in cs]
    dl_all = (_sel_dot(tri_bf, jnp.where(strict_c, lf_x, 0.0), split=_split2)
              + _sel_dot(same.astype(BF16), jnp.where(eye_c, i_x, 0.0), split=_split2))
    dl_all = jnp.where(tri_c, dl_all, -jnp.inf)
    dlog = [dl_all[rows[c]] for c in cs]
    yield
    dmax = [b_x[rows[c]] + _cummax_rows(r_x[rows[c]]) for c in cs]
    yield
    q_se = [_stack_se(q_b[rows[c]], same) for c in cs]
    v_se = [_stack_se(v_b[rows[c]], same) for c in cs]
    qkm = [_fold_se(lax.dot_general(q_se[c], _stack_se(k_b[rows[c]], same), _NT, preferred_element_type=F32)) for c in cs]

    cm = cm_s[...]
    nrow = nt_ref[...]
    mrow = mt_ref[...]
    yield _READY
    for c in cs:
        bx = b_x[rows[c]]
        inter = bx + mrow
        m_t = jnp.maximum(inter, dmax[c])
        s = qkm[c] * jnp.exp(dlog[c] - m_t)
        w_int = jnp.exp(inter - m_t)
        num = (w_int * _fold_se(jnp.dot(q_se[c], _stack_se(cm.astype(BF16), same), preferred_element_type=F32))
               + _fold_se(jnp.dot(_stack_se(s.astype(BF16), same), v_se[c], preferred_element_type=F32)))
        den = w_int * _dot_sel(q[rows[c]] * nrow, ones_h, split=_split2) + _dot_sel(s, ones_h, split=_split2)
        h_s[rows[c], :] = num / jnp.maximum(jnp.abs(den), jnp.exp(-m_t))
        blast = bx[CHUNK - 1:CHUNK]
        ex = e_x[rows[c]]
        m_new = jnp.maximum(blast + mrow, jnp.max(ex, axis=0, keepdims=True))
        dec = jnp.exp(blast + mrow - m_new)
        wk = jnp.exp(ex - m_new) * k[rows[c]]
        cm = dec * cm + _fold_se(lax.dot_general(_stack_se(wk.astype(BF16), same), v_se[c], _TN, preferred_element_type=F32))
        nrow = dec * nrow + jnp.sum(wk, axis=0, keepdims=True)
        mrow = m_new
        yield
    cm_s[...] = cm
    _store_cols(ct_m_ref, cm_s)
    nt_ref[...] = nrow
    mt_ref[...] = mrow
    out_ref[...] = h_s[...] * _sigmoid(p_ref[:, 3 * BW:4 * BW]) * _silu(p_ref[:, 4 * BW:5 * BW])


_READY, _DONE = object(), object()


def _chunked_body(pg_ref, s0_ref, gc0_ref, gcw_ref, nega_ref, dtb_ref, gnw_ref,
                  pm_ref, c0m_ref, n0_ref, m0_ref, mc0_ref, mcw_ref, bi_ref, bf_ref,
                  og_ref, st_ref, gct_ref, om_ref, ctm_ref, nt_ref, mt_ref, mct_ref,
                  gxp_s, go_s, gs_s, mxp_s, mh_s, mcm_s):
    @pl.when(pl.program_id(1) == 0)
    def _():
        _gdn_init(s0_ref, gc0_ref, gxp_s, gs_s)
        _ml_init(c0m_ref, n0_ref, m0_ref, mc0_ref, nt_ref, mt_ref, mxp_s, mcm_s)

    live = [_gdn_main(pg_ref, gcw_ref, nega_ref, dtb_ref, gnw_ref, og_ref, st_ref, gct_ref, gxp_s, go_s, gs_s),
            _ml_main(pm_ref, mcw_ref, bi_ref, bf_ref, om_ref, ctm_ref, nt_ref, mt_ref, mct_ref, mxp_s, mh_s, mcm_s)]
    pre = list(live)
    while pre:
        pre = [g for g in pre if next(g) is not _READY]
    while live:
        live = [g for g in live if next(g, _DONE) is not _DONE]


def _chunked(pg, s0, gc0, gcw, neg_a, dt_bias, gnw, pm, c0m, n0, m0, mc0, mcw, b_i, b_f, b, tg):
    assert tg == SE
    t = pg.shape[0]
    c2 = lambda bi, i: (0, 0)
    per_b3 = lambda bi, i: (bi, 0, 0)
    per_b4 = lambda bi, i: (bi, 0, 0, 0)
    tile = lambda wd: pl.BlockSpec((tg, wd), lambda bi, i: (i, bi))
    st4 = pl.BlockSpec((None, NH, DH, DH), per_b4)
    row = pl.BlockSpec((None, 1, BW), per_b3)
    gconv = pl.BlockSpec((None, CONV_W - 1, 3 * BW), per_b3)
    mconv = pl.BlockSpec((None, CONV_W - 1, 2 * BW), per_b3)
    gate = pl.BlockSpec((1, 128), c2)
    return pl.pallas_call(
        _chunked_body,
        grid=(b, t // tg),
        in_specs=[
            tile(W_GDN), st4, gconv, pl.BlockSpec((CONV_W, 3 * BW), c2), gate, gate, pl.BlockSpec((1, BW), c2),
            tile(W_ML), st4, row, row, mconv, pl.BlockSpec((CONV_W, 2 * BW), c2), gate, gate,
        ],
        out_specs=[tile(BW), st4, gconv, tile(BW), st4, row, row, mconv],
        out_shape=[
            jax.ShapeDtypeStruct((t, b * BW), F32),
            jax.ShapeDtypeStruct((b, NH, DH, DH), F32),
            jax.ShapeDtypeStruct((b, CONV_W - 1, 3 * BW), F32),
            jax.ShapeDtypeStruct((t, b * BW), F32),
            jax.ShapeDtypeStruct((b, NH, DH, DH), F32),
            jax.ShapeDtypeStruct((b, 1, BW), F32),
            jax.ShapeDtypeStruct((b, 1, BW), F32),
            jax.ShapeDtypeStruct((b, CONV_W - 1, 2 * BW), F32),
        ],
        scratch_shapes=[
            pltpu.VMEM((tg + 8, 3 * BW), F32), pltpu.VMEM((tg, BW), F32), pltpu.VMEM((DH, BW), F32),
            pltpu.VMEM((tg + 8, 2 * BW), F32), pltpu.VMEM((tg, BW), F32), pltpu.VMEM((DH, BW), F32),
        ],
        compiler_params=_cparams(("arbitrary", "arbitrary")),
        name="chunked",
    )(pg, s0, gc0, gcw, neg_a, dt_bias, gnw, pm, c0m, n0, m0, mc0, mcw, b_i, b_f)


def _transpose_rows(x):
    eye = (_iota((BW, BW), 0) == _iota((BW, BW), 1)).astype(BF16)
    return _sel_dot(eye, x, _NT)


def _full(shape):
    nd = len(shape)
    return pl.BlockSpec(shape, lambda i: (0,) * nd)


def _head_rows(ref, h):
    return ref[pl.ds(pl.multiple_of(h * DH, DH), DH), :]


def _gdn_step_body(p_ref, s_ref, c_ref, cw_ref, nega_ref, dtb_ref, nw_ref, out_ref, so_ref, co_ref,
                   kt_s, qet_s, wt_s, egt_s, vbt_s, qkt_s, ot_s):
    h = pl.program_id(0)

    @pl.when(h == 0)
    def _():
        x = p_ref[:, 0:3 * BW]
        xa = x * cw_ref[CONV_W - 1:CONV_W, :]
        for j in range(CONV_W - 1):
            xa = xa + c_ref[j] * cw_ref[j:j + 1, :]
        co_ref[0] = c_ref[1]
        co_ref[1] = c_ref[2]
        co_ref[2] = x
        qkv = _silu(xa)
        ones_h = _head_ones()
        q = qkv[:, 0:BW]
        k = qkv[:, BW:2 * BW]
        q = q * lax.rsqrt(_dot_sel(q * q, ones_h) + EPS) * (DH ** -0.5)
        k = k * lax.rsqrt(_dot_sel(k * k, ones_h) + EPS)
        v = qkv[:, 2 * BW:3 * BW]
        ga = p_ref[:, 4 * BW:4 * BW + 128]
        g = nega_ref[...] * _softplus(ga + dtb_ref[...])
        eg = jnp.exp(_dot_sel(g, _expand_sel(0)))
        be = _dot_sel(_sigmoid(ga), _expand_sel(NH))
        kt_s[...] = _transpose_rows(k)
        qet_s[...] = _transpose_rows(q * eg)
        wt_s[...] = _transpose_rows(k * be * eg)
        egt_s[...] = _transpose_rows(eg)
        vbt_s[...] = _transpose_rows(v * be)
        qkt_s[...] = _transpose_rows(_dot_sel(q * k, ones_h))

    r0 = pl.multiple_of(h * DH, DH)

    def reduce_k(kk, acc):
        aw, aq = acc
        s = s_ref[kk]
        return aw + s * wt_s[pl.ds(r0 + kk, 1), :], aq + s * qet_s[pl.ds(r0 + kk, 1), :]

    nbl = s_ref.shape[-1]
    zero = jnp.zeros((DH, nbl), F32)
    aw, aq = lax.fori_loop(0, DH, reduce_k, (zero, zero), unroll=8)
    v_new = _head_rows(vbt_s, h) - aw
    ot_s[pl.ds(r0, DH), :] = aq + qkt_s[pl.ds(r0, 1), :] * v_new

    def update_k(kk, carry):
        so_ref[kk] = s_ref[kk] * egt_s[pl.ds(r0 + kk, 1), :] + kt_s[pl.ds(r0 + kk, 1), :] * v_new
        return carry

    lax.fori_loop(0, DH, update_k, 0, unroll=8)

    @pl.when(h == NH - 1)
    def _():
        o = ot_s[...].T
        ms = _dot_sel(o * o, _head_ones()) * (1.0 / DH)
        out_ref[...] = (o * lax.rsqrt(ms + EPS)) * nw_ref[...] * _silu(p_ref[:, 3 * BW:4 * BW])


def _lanes_block(nb, l):
    return pl.BlockSpec((None, None, DH, DH, nb), lambda h: (l, h, 0, 0, 0))


def _gdn_step(p, s_all, s_acc, l, c0, conv_w, neg_a, dt_bias, norm_w):
    nb = p.shape[0]
    ins = (p, s_all, c0, conv_w, neg_a, dt_bias, norm_w)
    in_specs = [_full(a.shape) for a in ins]
    in_specs[1] = _lanes_block(nb, l)
    if s_acc is None:
        extra, specs_x, alias, body = (), [], {}, _gdn_step_body
    else:
        extra, specs_x, alias = (s_acc,), [pl.BlockSpec(memory_space=pl.ANY)], {7: 1}
        body = lambda *refs: _gdn_step_body(*refs[:7], *refs[8:])
    return pl.pallas_call(
        body,
        grid=(NH,),
        in_specs=in_specs + specs_x,
        out_specs=[_full((nb, BW)), _lanes_block(nb, l), _full((CONV_W - 1, nb, 3 * BW))],
        out_shape=[
            jax.ShapeDtypeStruct((nb, BW), F32),
            jax.ShapeDtypeStruct(s_all.shape, F32),
            jax.ShapeDtypeStruct((CONV_W - 1, nb, 3 * BW), F32),
        ],
        input_output_aliases=alias,
        scratch_shapes=[pltpu.VMEM((BW, nb), F32) for _ in range(7)],
        compiler_params=_cparams(("arbitrary",)),
        name="gdn_step",
    )(*ins, *extra)


def _ml_step_body(p_ref, c_ref, n_ref, m_ref, cv_ref, cw_ref, bi_ref, bf_ref, out_ref, co_ref, no_ref, mo_ref, cvo_ref,
                  qwt_s, qt_s, wit_s, wkt_s, vt_s, st_s, flt_s, nt_s, ht_s, nto_s):
    h = pl.program_id(0)

    @pl.when(h == 0)
    def _():
        x = p_ref[:, 0:2 * BW]
        xa = x * cw_ref[CONV_W - 1:CONV_W, :]
        for j in range(CONV_W - 1):
            xa = xa + cv_ref[j] * cw_ref[j:j + 1, :]
        cvo_ref[0] = cv_ref[1]
        cvo_ref[1] = cv_ref[2]
        cvo_ref[2] = x
        qk = _silu(xa)
        q = qk[:, 0:BW]
        k = qk[:, BW:2 * BW] * (DH ** -0.5)
        ga = p_ref[:, 5 * BW:5 * BW + 128]
        ip = _dot_sel(ga + bi_ref[...], _expand_sel(0))
        lf = _dot_sel(-_softplus(-(ga + bf_ref[...])), _expand_sel(NH))
        mm = _dot_sel(m_ref[...], _expand_sel(0))
        inter = lf + mm
        m_t = jnp.maximum(inter, ip)
        w_int = jnp.exp(inter - m_t)
        wgt = jnp.exp(ip - m_t)
        mo_ref[...] = m_t
        qwt_s[...] = _transpose_rows(q * w_int)
        qt_s[...] = _transpose_rows(q)
        wit_s[...] = _transpose_rows(w_int)
        wkt_s[...] = _transpose_rows(wgt * k)
        vt_s[...] = _transpose_rows(p_ref[:, 2 * BW:3 * BW])
        st_s[...] = _transpose_rows(_dot_sel(q * k, _head_ones()) * wgt)
        flt_s[...] = _transpose_rows(jnp.exp(-m_t))
        nt_s[...] = _transpose_rows(n_ref[...])

    r0 = pl.multiple_of(h * DH, DH)
    v_h = _head_rows(vt_s, h)
    w_row = wit_s[pl.ds(r0, 1), :]

    def one_d(d, acc):
        c = c_ref[d]
        co_ref[d] = c * w_row + wkt_s[pl.ds(r0 + d, 1), :] * v_h
        return acc + c * qwt_s[pl.ds(r0 + d, 1), :]

    nbl = c_ref.shape[-1]
    num = lax.fori_loop(0, DH, one_d, jnp.zeros((DH, nbl), F32), unroll=8)
    s_row = st_s[pl.ds(r0, 1), :]
    n_h = _head_rows(nt_s, h)
    qn = jnp.sum(_head_rows(qt_s, h) * n_h, axis=0, keepdims=True)
    den = w_row * qn + s_row
    ht_s[pl.ds(r0, DH), :] = (num + s_row * v_h) / jnp.maximum(jnp.abs(den), flt_s[pl.ds(r0, 1), :])
    nto_s[pl.ds(r0, DH), :] = w_row * n_h + _head_rows(wkt_s, h)

    @pl.when(h == NH - 1)
    def _():
        out_ref[...] = ht_s[...].T * _sigmoid(p_ref[:, 3 * BW:4 * BW]) * _silu(p_ref[:, 4 * BW:5 * BW])
        no_ref[...] = nto_s[...].T


def _ml_step(p, c_all, c_acc, l, n0, m0, c0, conv_w, b_i, b_f):
    nb = p.shape[0]
    ins = (p, c_all, n0, m0, c0, conv_w, b_i, b_f)
    in_specs = [_full(a.shape) for a in ins]
    in_specs[1] = _lanes_block(nb, l)
    if c_acc is None:
        extra, specs_x, alias, body = (), [], {}, _ml_step_body
    else:
        extra, specs_x, alias = (c_acc,), [pl.BlockSpec(memory_space=pl.ANY)], {8: 1}
        body = lambda *refs: _ml_step_body(*refs[:8], *refs[9:])
    return pl.pallas_call(
        body,
        grid=(NH,),
        in_specs=in_specs + specs_x,
        out_specs=[_full((nb, BW)), _lanes_block(nb, l), _full((nb, BW)), _full((nb, BW)),
                   _full((CONV_W - 1, nb, 2 * BW))],
        out_shape=[
            jax.ShapeDtypeStruct((nb, BW), F32),
            jax.ShapeDtypeStruct(c_all.shape, F32),
            jax.ShapeDtypeStruct((nb, BW), F32),
            jax.ShapeDtypeStruct((nb, BW), F32),
            jax.ShapeDtypeStruct((CONV_W - 1, nb, 2 * BW), F32),
        ],
        input_output_aliases=alias,
        scratch_shapes=[pltpu.VMEM((BW, nb), F32) for _ in range(10)],
        compiler_params=_cparams(("arbitrary",)),
        name="mlstm_step",
    )(*ins, *extra)


def _pad_lanes(x, width):
    return jnp.pad(x, [(0, 0)] * (x.ndim - 1) + [(0, width - x.shape[-1])])


IN_SEGMENTS = (('lru_x', BW), ('lru_z', BW), ('gdn_q', BW), ('gdn_k', BW), ('gdn_v', BW), ('gdn_z', BW),
               ('gdn_a', NH), ('gdn_b', NH), ('s5_u', BW), ('s5_z', BW),
               ('ml_q', BW), ('ml_k', BW), ('ml_v', BW), ('ml_o', BW), ('ml_z', BW),
               ('ml_i', NH), ('ml_f', NH), ('merge', NH * D_MODEL))
IN_WIDTH = sum(wd for _, wd in IN_SEGMENTS)
PACK_ORDER = (('merge',), ('lru_x', 'lru_z'), ('s5_u', 's5_z'), ('gdn_q', 'gdn_k', 'gdn_v', 'gdn_z'), ('gdn_a', 'gdn_b'),
              ('ml_q', 'ml_k', 'ml_v', 'ml_o', 'ml_z'), ('ml_i', 'ml_f'))
PACK_ROWS = 256


def _pack_body(w_ref, o_ref):
    src = {}
    off = 0
    for name, wd in IN_SEGMENTS:
        src[name] = (off, wd)
        off += wd
    o_ref[:, W_PACK + sum(W_MIX):] = jnp.zeros((o_ref.shape[0], W_PACK - sum(W_MIX)), BF16)
    dst = 0
    for group in PACK_ORDER:
        a = src[group[0]][0]
        wd = sum(src[n][1] for n in group)
        if wd % 128:
            o_ref[:, dst:dst + 128] = jnp.zeros((o_ref.shape[0], 128), BF16)
        o_ref[:, dst:dst + wd] = w_ref[:, a:a + wd].astype(BF16)
        dst += -(-wd // 128) * 128


def _pack_w_in(w_in):
    depth = w_in.shape[0]
    return pl.pallas_call(
        _pack_body,
        grid=(depth, D_MODEL // PACK_ROWS),
        in_specs=[pl.BlockSpec((None, PACK_ROWS, IN_WIDTH), lambda l, i: (l, i, 0))],
        out_specs=pl.BlockSpec((None, PACK_ROWS, 2 * W_PACK), lambda l, i: (l, i, 0)),
        out_shape=jax.ShapeDtypeStruct((depth, D_MODEL, 2 * W_PACK), BF16),
        compiler_params=_cparams(("arbitrary", "arbitrary")),
        name="pack_w_in",
    )(w_in)


def _prep_layer(l, prm):
    eye4 = jnp.eye(4, dtype=F32)
    blockdiag = lambda w: jnp.einsum('ncd,nm->ncmd', w, eye4).reshape(BW, BW)
    lru_wg = jnp.concatenate([blockdiag(prm['lru_w_a'][l]), blockdiag(prm['lru_w_x'][l])], axis=1).astype(BF16)
    lru_bg = jnp.concatenate([prm['lru_b_a'][l], prm['lru_b_x'][l]])[None, :]
    lru_sp = jax.nn.softplus(-prm['lru_lambda'][l])[None, :]

    lam_re, lam_im = prm['s5_lam_re'][l], prm['s5_lam_im'][l]
    dt = jnp.exp(prm['s5_log_dt'][l])[:, None]
    mag = jnp.exp(lam_re * dt)
    ang = lam_im * dt
    ar, ai = mag * jnp.cos(ang), mag * jnp.sin(ang)
    den = lam_re * lam_re + lam_im * lam_im
    cr = ((ar - 1.0) * lam_re + ai * lam_im) / den
    ci = (ai * lam_re - (ar - 1.0) * lam_im) / den
    b_re, b_im = prm['s5_b_re'][l], prm['s5_b_im'][l]
    bb_re = cr[..., None] * b_re - ci[..., None] * b_im
    bb_im = cr[..., None] * b_im + ci[..., None] * b_re
    eye_g = jnp.eye(S5_G, dtype=F32)
    in_dense = lambda bb: jnp.einsum('gnc,gh->gchn', bb, eye_g).reshape(BW, S5_W).astype(BF16)
    out_dense = lambda cc: jnp.einsum('gcn,gh->gnhc', cc, eye_g).reshape(S5_W, BW).astype(BF16)

    gate_row = lambda a, b: _pad_lanes(jnp.concatenate([a, b])[None, :], 128)
    zeros4 = jnp.zeros((NH,), F32)
    return dict(
        norm_w=prm['norm_w'][l][None, :],
        w_branch=prm['w_branch'][l].astype(BF16),
        w_out=prm['w_out'][l].astype(BF16),
        lru_cw=prm['lru_conv_w'][l], lru_wg=lru_wg, lru_bg=lru_bg, lru_sp=lru_sp,
        s5_ar=ar.reshape(1, S5_W), s5_ai=ai.reshape(1, S5_W),
        s5_bre=in_dense(bb_re), s5_bim=in_dense(bb_im),
        s5_cre=out_dense(prm['s5_c_re'][l]), s5_cim=out_dense(prm['s5_c_im'][l]),
        s5_d=prm['s5_d'][l][None, :], s5_wglu=prm['s5_w_glu'][l].astype(BF16),
        gdn_cw=prm['gdn_conv_w'][l],
        gdn_nega=gate_row(-jnp.exp(prm['gdn_A_log'][l]), zeros4),
        gdn_dtb=gate_row(prm['gdn_dt_bias'][l], zeros4),
        gdn_nw=jnp.tile(prm['gdn_norm_w'][l], NH)[None, :],
        ml_cw=prm['ml_conv_w'][l],
        ml_bi=gate_row(prm['ml_b_i'][l], zeros4),
        ml_bf=gate_row(zeros4, prm['ml_b_f'][l]),
    )


def _prompt_layer(x, pp, w_pack, l, final_w, final, tm, tt, tg):
    b, t, _ = x.shape
    p_lru, p_s5, p_gdn, p_ml = _in_proj(x, pp['norm_w'], w_pack, l, tm, True)
    z = lambda *s: jnp.zeros(s, F32)
    out_a, lru_h, lru_c = _lru(p_lru, z(b, BW), z(CONV_W - 1, b, BW),
                               pp['lru_cw'], pp['lru_wg'], pp['lru_bg'], pp['lru_sp'], tt)
    out_c, s5_re, s5_im = _s5(p_s5, z(b, S5_W), z(b, S5_W), pp['s5_ar'], pp['s5_ai'],
                              pp['s5_bre'], pp['s5_bim'], pp['s5_cre'], pp['s5_cim'], pp['s5_d'], pp['s5_wglu'], tt)
    out_b, gdn_s, gdn_c, out_d, ml_c, ml_n, ml_m, ml_cv = _chunked(
        p_gdn, z(b, NH, DH, DH), z(b, CONV_W - 1, 3 * BW), pp['gdn_cw'], pp['gdn_nega'], pp['gdn_dtb'], pp['gdn_nw'],
        p_ml, z(b, NH, DH, DH), z(b, 1, BW), z(b, 1, BW), z(b, CONV_W - 1, 2 * BW), pp['ml_cw'], pp['ml_bi'], pp['ml_bf'],
        b, tg)
    y = _merge(x, (out_a, out_b, out_c, out_d), pp['norm_w'], w_pack, l,
               pp['w_branch'], pp['w_out'], final_w, tm, final)
    st = dict(lru_h=lru_h, lru_conv=jnp.transpose(lru_c, (1, 0, 2)), gdn_S=gdn_s, gdn_conv=gdn_c,
              s5_re=s5_re.reshape(b, S5_G, S5_N), s5_im=s5_im.reshape(b, S5_G, S5_N),
              ml_C=ml_c, ml_n=ml_n.reshape(b, NH, DH), ml_m=ml_m.reshape(b, NH, DH)[:, :, 0], ml_conv=ml_cv)
    return y, st


BIG_STATES = ('gdn_S', 'ml_C')
TO_LANES = (0, 2, 3, 4, 1)
FROM_LANES = (0, 4, 1, 2, 3)


def _sample_layer(x, st, big_in, big_acc, l, pp, w_pack, final_w, final):
    nb = x.shape[1]
    p_lru, p_s5, p_gdn, p_ml = _in_proj(x, pp['norm_w'], w_pack, l, nb, False)
    tmaj = lambda c: jnp.transpose(c, (1, 0, 2))
    out_a, lru_h, lru_c = _lru(p_lru.reshape(1, nb, W_LRU), st['lru_h'], tmaj(st['lru_conv']),
                               pp['lru_cw'], pp['lru_wg'], pp['lru_bg'], pp['lru_sp'], 1)
    out_c, s5_re, s5_im = _s5(p_s5.reshape(1, nb, W_S5), st['s5_re'].reshape(nb, S5_W), st['s5_im'].reshape(nb, S5_W),
                              pp['s5_ar'], pp['s5_ai'], pp['s5_bre'], pp['s5_bim'], pp['s5_cre'], pp['s5_cim'],
                              pp['s5_d'], pp['s5_wglu'], 1)
    out_b, gdn_s, gdn_c = _gdn_step(p_gdn, big_in['gdn_S'], big_acc['gdn_S'], l, tmaj(st['gdn_conv']), pp['gdn_cw'],
                                    pp['gdn_nega'], pp['gdn_dtb'], pp['gdn_nw'])
    out_d, ml_c, ml_n, ml_m, ml_cv = _ml_step(p_ml, big_in['ml_C'], big_acc['ml_C'], l,
                                              st['ml_n'].reshape(nb, BW), _pad_lanes(st['ml_m'], 128),
                                              tmaj(st['ml_conv']), pp['ml_cw'], pp['ml_bi'], pp['ml_bf'])
    y = _merge(x, (out_a.reshape(nb, BW), out_b, out_c.reshape(nb, BW), out_d),
               pp['norm_w'], w_pack, l, pp['w_branch'], pp['w_out'], final_w, nb, final)
    new = dict(lru_h=lru_h, lru_conv=tmaj(lru_c), gdn_conv=tmaj(gdn_c),
               s5_re=s5_re.reshape(nb, S5_G, S5_N), s5_im=s5_im.reshape(nb, S5_G, S5_N),
               ml_n=ml_n.reshape(nb, NH, DH), ml_m=ml_m.reshape(nb, NH, DH)[:, :, 0], ml_conv=tmaj(ml_cv))
    return y, new, dict(gdn_S=gdn_s, ml_C=ml_c)


STATE_NAMES = ('lru_h', 'lru_conv', 'gdn_S', 'gdn_conv', 's5_re', 's5_im', 'ml_C', 'ml_n', 'ml_m', 'ml_conv')

TM_ROWS = 64
TT_SCAN = 128


def kernel(x_prompt, x_sample, state_lru_h, state_lru_conv, state_gdn_S, state_gdn_conv, state_s5_re, state_s5_im, state_ml_C, state_ml_n, state_ml_m, state_ml_conv, norm_w, w_in, lru_conv_w, lru_w_a, lru_b_a, lru_w_x, lru_b_x, lru_lambda, gdn_conv_w, gdn_A_log, gdn_dt_bias, gdn_norm_w, s5_lam_re, s5_lam_im, s5_log_dt, s5_b_re, s5_b_im, s5_c_re, s5_c_im, s5_d, s5_w_glu, ml_conv_w, ml_b_i, ml_b_f, w_branch, w_out, final_norm_w):
    prm = dict(norm_w=norm_w, w_in=w_in, lru_conv_w=lru_conv_w, lru_w_a=lru_w_a, lru_b_a=lru_b_a, lru_w_x=lru_w_x,
               lru_b_x=lru_b_x, lru_lambda=lru_lambda, gdn_conv_w=gdn_conv_w, gdn_A_log=gdn_A_log,
               gdn_dt_bias=gdn_dt_bias, gdn_norm_w=gdn_norm_w, s5_lam_re=s5_lam_re, s5_lam_im=s5_lam_im,
               s5_log_dt=s5_log_dt, s5_b_re=s5_b_re, s5_b_im=s5_b_im, s5_c_re=s5_c_re, s5_c_im=s5_c_im, s5_d=s5_d,
               s5_w_glu=s5_w_glu, ml_conv_w=ml_conv_w, ml_b_i=ml_b_i, ml_b_f=ml_b_f, w_branch=w_branch, w_out=w_out)
    depth = w_in.shape[0]
    t = x_prompt.shape[1]
    nb = x_sample.shape[0]
    final_w = final_norm_w[None, :]
    st_in = dict(lru_h=state_lru_h, lru_conv=state_lru_conv, gdn_S=state_gdn_S, gdn_conv=state_gdn_conv,
                 s5_re=state_s5_re, s5_im=state_s5_im, ml_C=state_ml_C, ml_n=state_ml_n, ml_m=state_ml_m,
                 ml_conv=state_ml_conv)
    tm = min(TM_ROWS, t)
    tt = min(TT_SCAN, t)
    tg = min(SE, t)

    xp = x_prompt
    xs = x_sample.reshape(1, nb, D_MODEL)
    small = [n for n in STATE_NAMES if n not in BIG_STATES]
    new_p = {n: [] for n in STATE_NAMES}
    new_s = {n: [] for n in small}
    big_in = {n: jnp.transpose(st_in[n], TO_LANES) for n in BIG_STATES}
    big_acc = {n: None for n in BIG_STATES}
    w_pack = _pack_w_in(w_in)
    for l in range(depth):
        pp = _prep_layer(l, prm)
        final = l == depth - 1
        xp, sp = _prompt_layer(xp, pp, w_pack, l, final_w, final, tm, tt, tg)
        xs, ss, big_acc = _sample_layer(xs, {n: st_in[n][l] for n in small}, big_in, big_acc, l, pp, w_pack,
                                        final_w, final)
        for n in STATE_NAMES:
            new_p[n].append(sp[n])
        for n in small:
            new_s[n].append(ss[n])
    np_ = {n: jnp.stack(v) for n, v in new_p.items()}
    ns_ = {n: jnp.stack(v) for n, v in new_s.items()}
    ns_.update({n: jnp.transpose(big_acc[n], FROM_LANES) for n in BIG_STATES})
    y_sample = xs.reshape(nb, 1, D_MODEL)
    return (xp, y_sample,
            np_['lru_h'], ns_['lru_h'], np_['lru_conv'], ns_['lru_conv'],
            np_['gdn_S'], ns_['gdn_S'], np_['gdn_conv'], ns_['gdn_conv'],
            np_['s5_re'], ns_['s5_re'], np_['s5_im'], ns_['s5_im'],
            np_['ml_C'], ns_['ml_C'], np_['ml_n'], ns_['ml_n'], np_['ml_m'], ns_['ml_m'],
            np_['ml_conv'], ns_['ml_conv'])
```

```python
import functools
import math

import jax
import jax.numpy as jnp
from jax import lax
from jax.experimental import pallas as pl
from jax.experimental.pallas import tpu as pltpu

F32 = jnp.float32
BF16 = jnp.bfloat16

D_MODEL = 1024
BW = 256
NH = 4
DH = 64
CONV_W = 4
CHUNK = 64
LRU_C = 8.0
S5_G = 16
S5_N = 64
S5_W = S5_G * S5_N
EPS = 1e-6

W_LRU = 2 * BW
W_S5 = 2 * BW
W_GDN = 4 * BW + 128
W_ML = 5 * BW + 128
W_MIX = (W_LRU, W_S5, W_GDN, W_ML)

VMEM_LIMIT_BYTES = 56 * 1024 * 1024


def _cparams(sem):
    return pltpu.CompilerParams(dimension_semantics=sem, vmem_limit_bytes=VMEM_LIMIT_BYTES)


def _sigmoid(x):
    return 1.0 / (1.0 + jnp.exp(-x))


def _silu(x):
    return x * _sigmoid(x)


def _softplus(x):
    return jnp.maximum(x, 0.0) + jnp.log(1.0 + jnp.exp(-jnp.abs(x)))


def _gelu_tanh(x):
    c = math.sqrt(2.0 / math.pi)
    return x * (0.5 * (1.0 + jnp.tanh(c * (x + 0.044715 * (x * x * x)))))


def _split3(x):
    hi = x.astype(BF16)
    r = x - hi.astype(F32)
    mid = r.astype(BF16)
    lo = (r - mid.astype(F32)).astype(BF16)
    return hi, mid, lo


def _split2(x):
    hi = x.astype(BF16)
    return hi, (x - hi.astype(F32)).astype(BF16)


_NN = (((1,), (0,)), ((), ()))
_NT = (((1,), (1,)), ((), ()))
_TN = (((0,), (0,)), ((), ()))


def _sel_dot(sel, x, dims=_NN, split=_split3):
    out = None
    for p in split(x):
        t = lax.dot_general(sel, p, dims, preferred_element_type=F32)
        out = t if out is None else out + t
    return out


def _dot_sel(x, sel, dims=_NN, split=_split3):
    out = None
    for p in split(x):
        t = lax.dot_general(p, sel, dims, preferred_element_type=F32)
        out = t if out is None else out + t
    return out


def _iota(shape, axis):
    return lax.broadcasted_iota(jnp.int32, shape, axis)


def _head_ones():
    return (_iota((BW, BW), 0) // DH == _iota((BW, BW), 1) // DH).astype(BF16)


def _expand_sel(offset):
    return (_iota((128, BW), 0) == _iota((128, BW), 1) // DH + offset).astype(BF16)


def _rms_rows(x, w_row):
    ms = jnp.mean(x * x, axis=-1, keepdims=True)
    return (x * lax.rsqrt(ms + EPS)) * w_row


def _inproj_body(x_ref, nw_ref, w_ref, *o_refs):
    nb, tm, _ = x_ref.shape
    xn = _rms_rows(x_ref[...].reshape(nb * tm, D_MODEL), nw_ref[...]).astype(BF16)
    off = 0
    for o_ref in o_refs:
        wd = o_ref.shape[-1] if len(o_ref.shape) == 3 else o_ref.shape[-1] // nb
        y = jnp.dot(xn, w_ref[:, off:off + wd], preferred_element_type=F32)
        if len(o_ref.shape) == 3:
            o_ref[...] = jnp.swapaxes(y.reshape(nb, tm, wd), 0, 1)
        else:
            for bi in range(nb):
                o_ref[:, bi * wd:(bi + 1) * wd] = y[bi * tm:(bi + 1) * tm]
        off += wd


W_PACK = NH * D_MODEL


def _in_proj(x, norm_w, w_pack, l, tm, scan_3d):
    b, t, _ = x.shape
    assert sum(W_MIX) <= W_PACK
    specs, shapes = [], []
    for n, wd in enumerate(W_MIX):
        if scan_3d and n < 2:
            specs.append(pl.BlockSpec((tm, b, wd), lambda i: (i, 0, 0)))
            shapes.append(jax.ShapeDtypeStruct((t, b, wd), F32))
        else:
            specs.append(pl.BlockSpec((tm, b * wd), lambda i: (i, 0)))
            shapes.append(jax.ShapeDtypeStruct((t, b * wd), F32))
    return pl.pallas_call(
        _inproj_body,
        grid=(t // tm,),
        in_specs=[
            pl.BlockSpec((b, tm, D_MODEL), lambda i: (0, i, 0)),
            pl.BlockSpec((1, D_MODEL), lambda i: (0, 0)),
            pl.BlockSpec((None, D_MODEL, W_PACK), lambda i: (l, 0, 1)),
        ],
        out_specs=specs,
        out_shape=shapes,
        compiler_params=_cparams(("arbitrary",)),
        name="in_proj",
    )(x, norm_w, w_pack)


def _merge_body(x_ref, a_ref, b_ref, c_ref, d_ref, nw_ref, wm_ref, wb_ref, wo_ref, fw_ref, y_ref, *, final):
    nb, tm, _ = x_ref.shape
    x = x_ref[...].reshape(nb * tm, D_MODEL)
    xn = _rms_rows(x, nw_ref[...]).astype(BF16)
    merged = None
    for n, r in enumerate((a_ref, b_ref, c_ref, d_ref)):
        if len(r.shape) == 3:
            br = jnp.swapaxes(r[...], 0, 1).reshape(nb * tm, BW)
        else:
            br = jnp.concatenate([r[:, bi * BW:(bi + 1) * BW] for bi in range(nb)], axis=0)
        gate = _sigmoid(jnp.dot(xn, wm_ref[:, n * D_MODEL:(n + 1) * D_MODEL], preferred_element_type=F32))
        proj = jnp.dot(br.astype(BF16), wb_ref[n], preferred_element_type=F32)
        merged = gate * proj if merged is None else merged + gate * proj
    y = x + jnp.dot(merged.astype(BF16), wo_ref[...], preferred_element_type=F32)
    if final:
        y = _rms_rows(y, fw_ref[...])
    y_ref[...] = y.reshape(nb, tm, D_MODEL)


def _merge(x, branches, norm_w, w_pack, l, w_branch, w_out, final_w, tm, final):
    b, t, _ = x.shape
    const2 = lambda i: (0, 0)
    br_specs = [pl.BlockSpec((tm,) + r.shape[1:], (lambda i: (i, 0, 0)) if r.ndim == 3 else (lambda i: (i, 0)))
                for r in branches]
    return pl.pallas_call(
        functools.partial(_merge_body, final=final),
        grid=(t // tm,),
        in_specs=[
            pl.BlockSpec((b, tm, D_MODEL), lambda i: (0, i, 0)),
            *br_specs,
            pl.BlockSpec((1, D_MODEL), const2),
            pl.BlockSpec((None, D_MODEL, W_PACK), lambda i: (l, 0, 0)),
            pl.BlockSpec((NH, BW, D_MODEL), lambda i: (0, 0, 0)),
            pl.BlockSpec((D_MODEL, D_MODEL), const2),
            pl.BlockSpec((1, D_MODEL), const2),
        ],
        out_specs=pl.BlockSpec((b, tm, D_MODEL), lambda i: (0, i, 0)),
        out_shape=jax.ShapeDtypeStruct((b, t, D_MODEL), F32),
        compiler_params=_cparams(("arbitrary",)),
        name="merge",
    )(x, *branches, norm_w, w_pack, w_branch, w_out, final_w)


def _lru_body(p_ref, h0_ref, c0_ref, cw_ref, wg_ref, bg_ref, sp_ref, out_ref, ht_ref, ct_ref,
              xp_s, a_s, b_s, h_s):
    tt, nb = p_ref.shape[0], p_ref.shape[1]

    @pl.when(pl.program_id(0) == 0)
    def _():
        h_s[...] = h0_ref[...]
        xp_s[0:CONV_W - 1] = c0_ref[...]

    xp_s[CONV_W - 1:CONV_W - 1 + tt] = p_ref[:, :, 0:BW]
    xa = None
    for j in range(CONV_W):
        term = xp_s[j:j + tt] * cw_ref[j:j + 1, :].reshape(1, 1, BW)
        xa = term if xa is None else xa + term
    tail = xp_s[tt:tt + CONV_W - 1]
    xp_s[0:CONV_W - 1] = tail
    ct_ref[...] = tail

    xa2 = xa.reshape(tt * nb, BW)
    gates = jnp.dot(xa2.astype(BF16), wg_ref[...], preferred_element_type=F32) + bg_ref[...]
    r = _sigmoid(gates[:, 0:BW])
    ig = _sigmoid(gates[:, BW:2 * BW])
    log_a = (-LRU_C * r) * sp_ref[...]
    a = jnp.exp(log_a)
    bb = jnp.sqrt(1.0 - jnp.exp(2.0 * log_a)) * (ig * xa2)
    a_s[...] = a.reshape(tt, nb, BW)
    b_s[...] = bb.reshape(tt, nb, BW)

    def step(t, h):
        h = a_s[t] * h + b_s[t]
        b_s[t] = h
        return h

    h = lax.fori_loop(0, tt, step, h_s[...], unroll=min(8, tt))
    h_s[...] = h
    ht_ref[...] = h
    out_ref[...] = b_s[...] * _silu(p_ref[:, :, BW:2 * BW])


def _lru(p, h0, c0, conv_w, w_gates, b_gates, sp_lam, tt):
    t, nb, _ = p.shape
    c2 = lambda i: (0, 0)
    c3 = lambda i: (0, 0, 0)
    return pl.pallas_call(
        _lru_body,
        grid=(t // tt,),
        in_specs=[
            pl.BlockSpec((tt, nb, W_LRU), lambda i: (i, 0, 0)),
            pl.BlockSpec((nb, BW), c2),
            pl.BlockSpec((CONV_W - 1, nb, BW), c3),
            pl.BlockSpec((CONV_W, BW), c2),
            pl.BlockSpec((BW, 2 * BW), c2),
            pl.BlockSpec((1, 2 * BW), c2),
            pl.BlockSpec((1, BW), c2),
        ],
        out_specs=[
            pl.BlockSpec((tt, nb, BW), lambda i: (i, 0, 0)),
            pl.BlockSpec((nb, BW), c2),
            pl.BlockSpec((CONV_W - 1, nb, BW), c3),
        ],
        out_shape=[
            jax.ShapeDtypeStruct((t, nb, BW), F32),
            jax.ShapeDtypeStruct((nb, BW), F32),
            jax.ShapeDtypeStruct((CONV_W - 1, nb, BW), F32),
        ],
        scratch_shapes=[
            pltpu.VMEM((tt + CONV_W - 1, nb, BW), F32),
            pltpu.VMEM((tt, nb, BW), F32),
            pltpu.VMEM((tt, nb, BW), F32),
            pltpu.VMEM((nb, BW), F32),
        ],
        compiler_params=_cparams(("arbitrary",)),
        name="rglru",
    )(p, h0, c0, conv_w, w_gates, b_gates, sp_lam)


def _s5_body(p_ref, hr0_ref, hi0_ref, ar_ref, ai_ref, bre_ref, bim_ref, cre_ref, cim_ref, d_ref, wglu_ref,
             out_ref, hrt_ref, hit_ref, xr_s, xi_s, hr_s, hi_s):
    tt, nb = p_ref.shape[0], p_ref.shape[1]

    @pl.when(pl.program_id(0) == 0)
    def _():
        hr_s[...] = hr0_ref[...]
        hi_s[...] = hi0_ref[...]

    u2 = p_ref[:, :, 0:BW].reshape(tt * nb, BW)
    ub = u2.astype(BF16)
    xr_s[...] = jnp.dot(ub, bre_ref[...], preferred_element_type=F32).reshape(tt, nb, S5_W)
    xi_s[...] = jnp.dot(ub, bim_ref[...], preferred_element_type=F32).reshape(tt, nb, S5_W)
    ar = jnp.broadcast_to(ar_ref[...], (nb, S5_W))
    ai = jnp.broadcast_to(ai_ref[...], (nb, S5_W))

    def step(t, c):
        hr, hi = c
        nr = ar * hr - ai * hi + xr_s[t]
        ni = ar * hi + ai * hr + xi_s[t]
        xr_s[t] = nr
        xi_s[t] = ni
        return nr, ni

    hr, hi = lax.fori_loop(0, tt, step, (hr_s[...], hi_s[...]), unroll=min(4, tt))
    hr_s[...] = hr
    hi_s[...] = hi
    hrt_ref[...] = hr
    hit_ref[...] = hi

    y = (jnp.dot(xr_s[...].reshape(tt * nb, S5_W).astype(BF16), cre_ref[...], preferred_element_type=F32)
         - jnp.dot(xi_s[...].reshape(tt * nb, S5_W).astype(BF16), cim_ref[...], preferred_element_type=F32)
         + d_ref[...] * u2)
    y = _gelu_tanh(y)
    y = y * _sigmoid(jnp.dot(y.astype(BF16), wglu_ref[...], preferred_element_type=F32))
    out_ref[...] = y.reshape(tt, nb, BW) * _silu(p_ref[:, :, BW:2 * BW])


def _s5(p, hr0, hi0, ar, ai, bre, bim, cre, cim, d, wglu, tt):
    t, nb, _ = p.shape
    c2 = lambda i: (0, 0)
    st = pl.BlockSpec((nb, S5_W), c2)
    row = pl.BlockSpec((1, S5_W), c2)
    return pl.pallas_call(
        _s5_body,
        grid=(t // tt,),
        in_specs=[
            pl.BlockSpec((tt, nb, W_S5), lambda i: (i, 0, 0)),
            st, st, row, row,
            pl.BlockSpec((BW, S5_W), c2), pl.BlockSpec((BW, S5_W), c2),
            pl.BlockSpec((S5_W, BW), c2), pl.BlockSpec((S5_W, BW), c2),
            pl.BlockSpec((1, BW), c2),
            pl.BlockSpec((BW, BW), c2),
        ],
        out_specs=[pl.BlockSpec((tt, nb, BW), lambda i: (i, 0, 0)), st, st],
        out_shape=[
            jax.ShapeDtypeStruct((t, nb, BW), F32),
            jax.ShapeDtypeStruct((nb, S5_W), F32),
            jax.ShapeDtypeStruct((nb, S5_W), F32),
        ],
        scratch_shapes=[
            pltpu.VMEM((tt, nb, S5_W), F32),
            pltpu.VMEM((tt, nb, S5_W), F32),
            pltpu.VMEM((nb, S5_W), F32),
            pltpu.VMEM((nb, S5_W), F32),
        ],
        compiler_params=_cparams(("arbitrary",)),
        name="s5",
    )(p, hr0, hi0, ar, ai, bre, bim, cre, cim, d, wglu)


def _conv_rows(xp_s, p_cols, cw_ref, tg):
    xp_s[8:8 + tg] = p_cols
    y = None
    for j in range(CONV_W):
        term = xp_s[8 - (CONV_W - 1) + j:8 - (CONV_W - 1) + j + tg] * cw_ref[j:j + 1, :]
        y = term if y is None else y + term
    tail = xp_s[8 + tg - (CONV_W - 1):8 + tg]
    xp_s[8 - (CONV_W - 1):8] = tail
    return y, tail


SE = NH * CHUNK


def _se_masks():
    r = _iota((SE, SE), 0)
    c = _iota((SE, SE), 1)
    same = (r // CHUNK) == (c // CHUNK)
    return same, same & ((r % CHUNK) >= (c % CHUNK))


def _cols_masks(n):
    i = _iota((n, BW), 0) % CHUNK
    j = _iota((n, BW), 1) % CHUNK
    return i >= j, i > j, i == j


def _stack_se(x_bf, same):
    return jnp.where(same, jnp.concatenate([x_bf] * NH, axis=0), jnp.zeros((), x_bf.dtype))


def _tile_rows(x):
    return jnp.concatenate([x] * NH, axis=0)


def _fold_se(x_se):
    out = x_se[0:CHUNK]
    for h in range(1, NH):
        out = out + x_se[h * CHUNK:(h + 1) * CHUNK]
    return out


def _load_cols(s_s, blocks_ref):
    for h in range(NH):
        s_s[:, h * DH:(h + 1) * DH] = blocks_ref[h]


def _store_cols(blocks_ref, s_s):
    for h in range(NH):
        blocks_ref[h] = s_s[:, h * DH:(h + 1) * DH]


def _last_rows(x, nc):
    return jnp.concatenate([jnp.broadcast_to(x[c * CHUNK + CHUNK - 1:(c + 1) * CHUNK], (CHUNK, BW)) for c in range(nc)], axis=0)


def _gdn_init(s0_ref, c0_ref, xp_s, s_s):
    _load_cols(s_s, s0_ref)
    xp_s[8 - (CONV_W - 1):8] = c0_ref[...]


def _gdn_main(p_ref, cw_ref, nega_ref, dtb_ref, nw_ref, out_ref, st_ref, ct_ref, xp_s, o_s, s_s):
    tg = p_ref.shape[0]
    nc = tg // CHUNK
    y, tail = _conv_rows(xp_s, p_ref[:, 0:3 * BW], cw_ref, tg)
    ct_ref[...] = tail
    qkv = _silu(y)
    ones_h = _head_ones()
    q = qkv[:, 0:BW]
    k = qkv[:, BW:2 * BW]
    q = q * lax.rsqrt(_dot_sel(q * q, ones_h, split=_split2) + EPS) * (DH ** -0.5)
    k = k * lax.rsqrt(_dot_sel(k * k, ones_h, split=_split2) + EPS)
    v = qkv[:, 2 * BW:3 * BW]
    yield
    ga = p_ref[:, 4 * BW:4 * BW + 128]
    lane = _iota((tg, 128), 1)
    g = nega_ref[...] * _softplus(ga + dtb_ref[...])
    gall = jnp.where(lane < NH, g, _sigmoid(ga))

    same, tri = _se_masks()
    tri_c, strict_c, _ = _cols_masks(tg)
    tri_bf = tri.astype(BF16)
    beta_x = _dot_sel(gall, _expand_sel(NH), split=_split2)
    g_x = _dot_sel(gall, _expand_sel(0), split=_split2)
    gc_x = _sel_dot(tri_bf, g_x, split=_split2)
    gl_x = _last_rows(gc_x, nc)
    egc = jnp.exp(gc_x)
    kb = k * beta_x
    k_b = k.astype(BF16)
    kb_b = kb.astype(BF16)
    q_b = q.astype(BF16)
    vb_b = (v * beta_x).astype(BF16)
    kbg_b = (kb * egc).astype(BF16)
    qd_b = (q * egc).astype(BF16)
    kd_b = (k * jnp.exp(gl_x - gc_x)).astype(BF16)
    yield

    cs = range(nc)
    rows = [slice(c * CHUNK, (c + 1) * CHUNK) for c in cs]
    diff = _sel_dot(tri_bf, jnp.where(strict_c, g_x, 0.0), split=_split2)
    dec_all = jnp.where(tri_c, jnp.exp(diff), 0.0)
    dec_t = [_tile_rows(dec_all[rows[c]]) for c in cs]
    ndec_all = jnp.where(strict_c, -dec_all, 0.0)
    ndec_t = [_tile_rows(ndec_all[rows[c]]) for c in cs]

    dot = functools.partial(jnp.dot, preferred_element_type=F32)
    cat = functools.partial(jnp.concatenate, axis=0)
    k_se = [_stack_se(k_b[rows[c]], same) for c in cs]
    kq = [lax.dot_general(cat([_stack_se(kb_b[rows[c]], same), _stack_se(q_b[rows[c]], same)]), k_se[c], _NT,
                          preferred_element_type=F32) for c in cs]
    attn = [(kq[c][SE:] * dec_t[c]).astype(BF16) for c in cs]
    eye = (_iota((SE, SE), 0) == _iota((SE, SE), 1)).astype(F32)
    npow = [kq[c][:SE] * ndec_t[c] for c in cs]
    tinv = [eye + npow[c] for c in cs]
    nb = [n.astype(BF16) for n in npow]
    yield
    nb = [dot(n, n).astype(BF16) for n in nb]
    for _ in range(4):
        yield
        both = [dot(cat([t.astype(BF16), n]), n) for t, n in zip(tinv, nb)]
        tinv = [t + b2[:SE] for t, b2 in zip(tinv, both)]
        nb = [b2[SE:].astype(BF16) for b2 in both]
    yield
    tinv = [t + dot(t.astype(BF16), n) for t, n in zip(tinv, nb)]
    t_bf = [t.astype(BF16) for t in tinv]
    u = [dot(t_bf[c], _stack_se(vb_b[rows[c]], same)) for c in cs]
    w = [dot(t_bf[c], _stack_se(kbg_b[rows[c]], same)).astype(BF16) for c in cs]

    kd_t = [_stack_se(kd_b[rows[c]], same).T for c in cs]
    s = s_s[...]
    yield _READY
    for c in cs:
        sb = _stack_se(s.astype(BF16), same)
        ws = dot(cat([w[c], _stack_se(qd_b[rows[c]], same)]), sb)
        vb = (u[c] - ws[:SE]).astype(BF16)
        av = dot(cat([attn[c], kd_t[c]]), vb)
        s = s * jnp.exp(gl_x[c * CHUNK:c * CHUNK + 1]) + _fold_se(av[SE:])
        o_s[rows[c], :] = _fold_se(ws[SE:] + av[:SE])
        yield
    s_s[...] = s
    _store_cols(st_ref, s_s)

    o = o_s[...]
    ms = _dot_sel(o * o, ones_h, split=_split2) * (1.0 / DH)
    out_ref[...] = (o * lax.rsqrt(ms + EPS)) * nw_ref[...] * _silu(p_ref[:, 3 * BW:4 * BW])


def _cummax_rows(x):
    row = _iota(x.shape, 0)
    sh = 1
    while sh < x.shape[0]:
        x = jnp.maximum(x, jnp.where(row >= sh, pltpu.roll(x, sh, 0), -jnp.inf))
        sh *= 2
    return x


def _ml_init(c0m_ref, n0_ref, m0_ref, c0_ref, nt_ref, mt_ref, xp_s, cm_s):
    _load_cols(cm_s, c0m_ref)
    nt_ref[...] = n0_ref[...]
    mt_ref[...] = m0_ref[...]
    xp_s[8 - (CONV_W - 1):8] = c0_ref[...]


def _ml_main(p_ref, cw_ref, bi_ref, bf_ref, out_ref, ct_m_ref, nt_ref, mt_ref, cvt_ref, xp_s, h_s, cm_s):
    tg = p_ref.shape[0]
    nc = tg // CHUNK
    y, tail = _conv_rows(xp_s, p_ref[:, 0:2 * BW], cw_ref, tg)
    cvt_ref[...] = tail
    qk = _silu(y)
    q = qk[:, 0:BW]
    k = qk[:, BW:2 * BW] * (DH ** -0.5)
    yield
    ga = p_ref[:, 5 * BW:5 * BW + 128]
    lane = _iota((tg, 128), 1)
    gall = jnp.where(lane < NH, ga + bi_ref[...], -_softplus(-(ga + bf_ref[...])))

    same, tri = _se_masks()
    tri_c, strict_c, eye_c = _cols_masks(tg)
    tri_bf = tri.astype(BF16)
    ones_h = _head_ones()
    i_x = _dot_sel(gall, _expand_sel(0), split=_split2)
    lf_x = _dot_sel(gall, _expand_sel(NH), split=_split2)
    b_x = _sel_dot(tri_bf, lf_x, split=_split2)
    r_x = i_x - b_x
    e_x = _last_rows(b_x, nc) - b_x + i_x
    q_b = q.astype(BF16)
    k_b = k.astype(BF16)
    v_b = p_ref[:, 2 * BW:3 * BW].astype(BF16)
    yield

    cs = range(nc)
    rows = [slice(c * CHUNK, (c + 1) * CHUNK) for c in cs]
    dl_all = (_sel_dot(tri_bf, jnp.where(strict_c, lf_x, 0.0), split=_split2)
              + _sel_dot(same.astype(BF16), jnp.where(eye_c, i_x, 0.0), split=_split2))
    dl_all = jnp.where(tri_c, dl_all, -jnp.inf)
    dlog = [dl_all[rows[c]] for c in cs]
    yield
    dmax = [b_x[rows[c]] + _cummax_rows(r_x[rows[c]]) for c in cs]
    yield
    q_se = [_stack_se(q_b[rows[c]], same) for c in cs]
    v_se = [_stack_se(v_b[rows[c]], same) for c in cs]
    qkm = [_fold_se(lax.dot_general(q_se[c], _stack_se(k_b[rows[c]], same), _NT, preferred_element_type=F32)) for c in cs]

    cm = cm_s[...]
    nrow = nt_ref[...]
    mrow = mt_ref[...]
    yield _READY
    for c in cs:
        bx = b_x[rows[c]]
        inter = bx + mrow
        m_t = jnp.maximum(inter, dmax[c])
        s = qkm[c] * jnp.exp(dlog[c] - m_t)
        w_int = jnp.exp(inter - m_t)
        num = (w_int * _fold_se(jnp.dot(q_se[c], _stack_se(cm.astype(BF16), same), preferred_element_type=F32))
               + _fold_se(jnp.dot(_stack_se(s.astype(BF16), same), v_se[c], preferred_element_type=F32)))
        den = w_int * _dot_sel(q[rows[c]] * nrow, ones_h, split=_split2) + _dot_sel(s, ones_h, split=_split2)
        h_s[rows[c], :] = num / jnp.maximum(jnp.abs(den), jnp.exp(-m_t))
        blast = bx[CHUNK - 1:CHUNK]
        ex = e_x[rows[c]]
        m_new = jnp.maximum(blast + mrow, jnp.max(ex, axis=0, keepdims=True))
        dec = jnp.exp(blast + mrow - m_new)
        wk = jnp.exp(ex - m_new) * k[rows[c]]
        cm = dec * cm + _fold_se(lax.dot_general(_stack_se(wk.astype(BF16), same), v_se[c], _TN, preferred_element_type=F32))
        nrow = dec * nrow + jnp.sum(wk, axis=0, keepdims=True)
        mrow = m_new
        yield
    cm_s[...] = cm
    _store_cols(ct_m_ref, cm_s)
    nt_ref[...] = nrow
    mt_ref[...] = mrow
    out_ref[...] = h_s[...] * _sigmoid(p_ref[:, 3 * BW:4 * BW]) * _silu(p_ref[:, 4 * BW:5 * BW])


_READY, _DONE = object(), object()


def _chunked_body(pg_ref, s0_ref, gc0_ref, gcw_ref, nega_ref, dtb_ref, gnw_ref,
                  pm_ref, c0m_ref, n0_ref, m0_ref, mc0_ref, mcw_ref, bi_ref, bf_ref,
                  og_ref, st_ref, gct_ref, om_ref, ctm_ref, nt_ref, mt_ref, mct_ref,
                  gxp_s, go_s, gs_s, mxp_s, mh_s, mcm_s):
    @pl.when(pl.program_id(1) == 0)
    def _():
        _gdn_init(s0_ref, gc0_ref, gxp_s, gs_s)
        _ml_init(c0m_ref, n0_ref, m0_ref, mc0_ref, nt_ref, mt_ref, mxp_s, mcm_s)

    live = [_gdn_main(pg_ref, gcw_ref, nega_ref, dtb_ref, gnw_ref, og_ref, st_ref, gct_ref, gxp_s, go_s, gs_s),
            _ml_main(pm_ref, mcw_ref, bi_ref, bf_ref, om_ref, ctm_ref, nt_ref, mt_ref, mct_ref, mxp_s, mh_s, mcm_s)]
    pre = list(live)
    while pre:
        pre = [g for g in pre if next(g) is not _READY]
    while live:
        live = [g for g in live if next(g, _DONE) is not _DONE]


def _chunked(pg, s0, gc0, gcw, neg_a, dt_bias, gnw, pm, c0m, n0, m0, mc0, mcw, b_i, b_f, b, tg):
    assert tg == SE
    t = pg.shape[0]
    c2 = lambda bi, i: (0, 0)
    per_b3 = lambda bi, i: (bi, 0, 0)
    per_b4 = lambda bi, i: (bi, 0, 0, 0)
    tile = lambda wd: pl.BlockSpec((tg, wd), lambda bi, i: (i, bi))
    st4 = pl.BlockSpec((None, NH, DH, DH), per_b4)
    row = pl.BlockSpec((None, 1, BW), per_b3)
    gconv = pl.BlockSpec((None, CONV_W - 1, 3 * BW), per_b3)
    mconv = pl.BlockSpec((None, CONV_W - 1, 2 * BW), per_b3)
    gate = pl.BlockSpec((1, 128), c2)
    return pl.pallas_call(
        _chunked_body,
        grid=(b, t // tg),
        in_specs=[
            tile(W_GDN), st4, gconv, pl.BlockSpec((CONV_W, 3 * BW), c2), gate, gate, pl.BlockSpec((1, BW), c2),
            tile(W_ML), st4, row, row, mconv, pl.BlockSpec((CONV_W, 2 * BW), c2), gate, gate,
        ],
        out_specs=[tile(BW), st4, gconv, tile(BW), st4, row, row, mconv],
        out_shape=[
            jax.ShapeDtypeStruct((t, b * BW), F32),
            jax.ShapeDtypeStruct((b, NH, DH, DH), F32),
            jax.ShapeDtypeStruct((b, CONV_W - 1, 3 * BW), F32),
            jax.ShapeDtypeStruct((t, b * BW), F32),
            jax.ShapeDtypeStruct((b, NH, DH, DH), F32),
            jax.ShapeDtypeStruct((b, 1, BW), F32),
            jax.ShapeDtypeStruct((b, 1, BW), F32),
            jax.ShapeDtypeStruct((b, CONV_W - 1, 2 * BW), F32),
        ],
        scratch_shapes=[
            pltpu.VMEM((tg + 8, 3 * BW), F32), pltpu.VMEM((tg, BW), F32), pltpu.VMEM((DH, BW), F32),
            pltpu.VMEM((tg + 8, 2 * BW), F32), pltpu.VMEM((tg, BW), F32), pltpu.VMEM((DH, BW), F32),
        ],
        compiler_params=_cparams(("arbitrary", "arbitrary")),
        name="chunked",
    )(pg, s0, gc0, gcw, neg_a, dt_bias, gnw, pm, c0m, n0, m0, mc0, mcw, b_i, b_f)


def _transpose_rows(x):
    eye = (_iota((BW, BW), 0) == _iota((BW, BW), 1)).astype(BF16)
    return _sel_dot(eye, x, _NT)


def _full(shape):
    nd = len(shape)
    return pl.BlockSpec(shape, lambda i: (0,) * nd)


def _head_rows(ref, h):
    return ref[pl.ds(pl.multiple_of(h * DH, DH), DH), :]


def _gdn_step_body(p_ref, s_ref, c_ref, cw_ref, nega_ref, dtb_ref, nw_ref, out_ref, so_ref, co_ref,
                   kt_s, qet_s, wt_s, egt_s, vbt_s, qkt_s, ot_s):
    h = pl.program_id(0)

    @pl.when(h == 0)
    def _():
        x = p_ref[:, 0:3 * BW]
        xa = x * cw_ref[CONV_W - 1:CONV_W, :]
        for j in range(CONV_W - 1):
            xa = xa + c_ref[j] * cw_ref[j:j + 1, :]
        co_ref[0] = c_ref[1]
        co_ref[1] = c_ref[2]
        co_ref[2] = x
        qkv = _silu(xa)
        ones_h = _head_ones()
        q = qkv[:, 0:BW]
        k = qkv[:, BW:2 * BW]
        q = q * lax.rsqrt(_dot_sel(q * q, ones_h) + EPS) * (DH ** -0.5)
        k = k * lax.rsqrt(_dot_sel(k * k, ones_h) + EPS)
        v = qkv[:, 2 * BW:3 * BW]
        ga = p_ref[:, 4 * BW:4 * BW + 128]
        g = nega_ref[...] * _softplus(ga + dtb_ref[...])
        eg = jnp.exp(_dot_sel(g, _expand_sel(0)))
        be = _dot_sel(_sigmoid(ga), _expand_sel(NH))
        kt_s[...] = _transpose_rows(k)
        qet_s[...] = _transpose_rows(q * eg)
        wt_s[...] = _transpose_rows(k * be * eg)
        egt_s[...] = _transpose_rows(eg)
        vbt_s[...] = _transpose_rows(v * be)
        qkt_s[...] = _transpose_rows(_dot_sel(q * k, ones_h))

    r0 = pl.multiple_of(h * DH, DH)

    def reduce_k(kk, acc):
        aw, aq = acc
        s = s_ref[kk]
        return aw + s * wt_s[pl.ds(r0 + kk, 1), :], aq + s * qet_s[pl.ds(r0 + kk, 1), :]

    nbl = s_ref.shape[-1]
    zero = jnp.zeros((DH, nbl), F32)
    aw, aq = lax.fori_loop(0, DH, reduce_k, (zero, zero), unroll=8)
    v_new = _head_rows(vbt_s, h) - aw
    ot_s[pl.ds(r0, DH), :] = aq + qkt_s[pl.ds(r0, 1), :] * v_new

    def update_k(kk, carry):
        so_ref[kk] = s_ref[kk] * egt_s[pl.ds(r0 + kk, 1), :] + kt_s[pl.ds(r0 + kk, 1), :] * v_new
        return carry

    lax.fori_loop(0, DH, update_k, 0, unroll=8)

    @pl.when(h == NH - 1)
    def _():
        o = ot_s[...].T
        ms = _dot_sel(o * o, _head_ones()) * (1.0 / DH)
        out_ref[...] = (o * lax.rsqrt(ms + EPS)) * nw_ref[...] * _silu(p_ref[:, 3 * BW:4 * BW])


def _lanes_block(nb, l):
    return pl.BlockSpec((None, None, DH, DH, nb), lambda h: (l, h, 0, 0, 0))


def _gdn_step(p, s_all, s_acc, l, c0, conv_w, neg_a, dt_bias, norm_w):
    nb = p.shape[0]
    ins = (p, s_all, c0, conv_w, neg_a, dt_bias, norm_w)
    in_specs = [_full(a.shape) for a in ins]
    in_specs[1] = _lanes_block(nb, l)
    if s_acc is None:
        extra, specs_x, alias, body = (), [], {}, _gdn_step_body
    else:
        extra, specs_x, alias = (s_acc,), [pl.BlockSpec(memory_space=pl.ANY)], {7: 1}
        body = lambda *refs: _gdn_step_body(*refs[:7], *refs[8:])
    return pl.pallas_call(
        body,
        grid=(NH,),
        in_specs=in_specs + specs_x,
        out_specs=[_full((nb, BW)), _lanes_block(nb, l), _full((CONV_W - 1, nb, 3 * BW))],
        out_shape=[
            jax.ShapeDtypeStruct((nb, BW), F32),
            jax.ShapeDtypeStruct(s_all.shape, F32),
            jax.ShapeDtypeStruct((CONV_W - 1, nb, 3 * BW), F32),
        ],
        input_output_aliases=alias,
        scratch_shapes=[pltpu.VMEM((BW, nb), F32) for _ in range(7)],
        compiler_params=_cparams(("arbitrary",)),
        name="gdn_step",
    )(*ins, *extra)


def _ml_step_body(p_ref, c_ref, n_ref, m_ref, cv_ref, cw_ref, bi_ref, bf_ref, out_ref, co_ref, no_ref, mo_ref, cvo_ref,
                  qwt_s, qt_s, wit_s, wkt_s, vt_s, st_s, flt_s, nt_s, ht_s, nto_s):
    h = pl.program_id(0)

    @pl.when(h == 0)
    def _():
        x = p_ref[:, 0:2 * BW]
        xa = x * cw_ref[CONV_W - 1:CONV_W, :]
        for j in range(CONV_W - 1):
            xa = xa + cv_ref[j] * cw_ref[j:j + 1, :]
        cvo_ref[0] = cv_ref[1]
        cvo_ref[1] = cv_ref[2]
        cvo_ref[2] = x
        qk = _silu(xa)
        q = qk[:, 0:BW]
        k = qk[:, BW:2 * BW] * (DH ** -0.5)
        ga = p_ref[:, 5 * BW:5 * BW + 128]
        ip = _dot_sel(ga + bi_ref[...], _expand_sel(0))
        lf = _dot_sel(-_softplus(-(ga + bf_ref[...])), _expand_sel(NH))
        mm = _dot_sel(m_ref[...], _expand_sel(0))
        inter = lf + mm
        m_t = jnp.maximum(inter, ip)
        w_int = jnp.exp(inter - m_t)
        wgt = jnp.exp(ip - m_t)
        mo_ref[...] = m_t
        qwt_s[...] = _transpose_rows(q * w_int)
        qt_s[...] = _transpose_rows(q)
        wit_s[...] = _transpose_rows(w_int)
        wkt_s[...] = _transpose_rows(wgt * k)
        vt_s[...] = _transpose_rows(p_ref[:, 2 * BW:3 * BW])
        st_s[...] = _transpose_rows(_dot_sel(q * k, _head_ones()) * wgt)
        flt_s[...] = _transpose_rows(jnp.exp(-m_t))
        nt_s[...] = _transpose_rows(n_ref[...])

    r0 = pl.multiple_of(h * DH, DH)
    v_h = _head_rows(vt_s, h)
    w_row = wit_s[pl.ds(r0, 1), :]

    def one_d(d, acc):
        c = c_ref[d]
        co_ref[d] = c * w_row + wkt_s[pl.ds(r0 + d, 1), :] * v_h
        return acc + c * qwt_s[pl.ds(r0 + d, 1), :]

    nbl = c_ref.shape[-1]
    num = lax.fori_loop(0, DH, one_d, jnp.zeros((DH, nbl), F32), unroll=8)
    s_row = st_s[pl.ds(r0, 1), :]
    n_h = _head_rows(nt_s, h)
    qn = jnp.sum(_head_rows(qt_s, h) * n_h, axis=0, keepdims=True)
    den = w_row * qn + s_row
    ht_s[pl.ds(r0, DH), :] = (num + s_row * v_h) / jnp.maximum(jnp.abs(den), flt_s[pl.ds(r0, 1), :])
    nto_s[pl.ds(r0, DH), :] = w_row * n_h + _head_rows(wkt_s, h)

    @pl.when(h == NH - 1)
    def _():
        out_ref[...] = ht_s[...].T * _sigmoid(p_ref[:, 3 * BW:4 * BW]) * _silu(p_ref[:, 4 * BW:5 * BW])
        no_ref[...] = nto_s[...].T


def _ml_step(p, c_all, c_acc, l, n0, m0, c0, conv_w, b_i, b_f):
    nb = p.shape[0]
    ins = (p, c_all, n0, m0, c0, conv_w, b_i, b_f)
    in_specs = [_full(a.shape) for a in ins]
    in_specs[1] = _lanes_block(nb, l)
    if c_acc is None:
        extra, specs_x, alias, body = (), [], {}, _ml_step_body
    else:
        extra, specs_x, alias = (c_acc,), [pl.BlockSpec(memory_space=pl.ANY)], {8: 1}
        body = lambda *refs: _ml_step_body(*refs[:8], *refs[9:])
    return pl.pallas_call(
        body,
        grid=(NH,),
        in_specs=in_specs + specs_x,
        out_specs=[_full((nb, BW)), _lanes_block(nb, l), _full((nb, BW)), _full((nb, BW)),
                   _full((CONV_W - 1, nb, 2 * BW))],
        out_shape=[
            jax.ShapeDtypeStruct((nb, BW), F32),
            jax.ShapeDtypeStruct(c_all.shape, F32),
            jax.ShapeDtypeStruct((nb, BW), F32),
            jax.ShapeDtypeStruct((nb, BW), F32),
            jax.ShapeDtypeStruct((CONV_W - 1, nb, 2 * BW), F32),
        ],
        input_output_aliases=alias,
        scratch_shapes=[pltpu.VMEM((BW, nb), F32) for _ in range(10)],
        compiler_params=_cparams(("arbitrary",)),
        name="mlstm_step",
    )(*ins, *extra)


def _pad_lanes(x, width):
    return jnp.pad(x, [(0, 0)] * (x.ndim - 1) + [(0, width - x.shape[-1])])


IN_SEGMENTS = (('lru_x', BW), ('lru_z', BW), ('gdn_q', BW), ('gdn_k', BW), ('gdn_v', BW), ('gdn_z', BW),
               ('gdn_a', NH), ('gdn_b', NH), ('s5_u', BW), ('s5_z', BW),
               ('ml_q', BW), ('ml_k', BW), ('ml_v', BW), ('ml_o', BW), ('ml_z', BW),
               ('ml_i', NH), ('ml_f', NH), ('merge', NH * D_MODEL))
IN_WIDTH = sum(wd for _, wd in IN_SEGMENTS)
PACK_ORDER = (('merge',), ('lru_x', 'lru_z'), ('s5_u', 's5_z'), ('gdn_q', 'gdn_k', 'gdn_v', 'gdn_z'), ('gdn_a', 'gdn_b'),
              ('ml_q', 'ml_k', 'ml_v', 'ml_o', 'ml_z'), ('ml_i', 'ml_f'))
PACK_ROWS = 256


def _pack_body(wt_ref, o_ref):
    rows = o_ref.shape[0]
    eye = (_iota((rows, rows), 0) == _iota((rows, rows), 1)).astype(BF16)
    src = {}
    off = 0
    for name, wd in IN_SEGMENTS:
        src[name] = (off, wd)
        off += wd
    o_ref[:, W_PACK + sum(W_MIX):] = jnp.zeros((rows, W_PACK - sum(W_MIX)), BF16)
    dst = 0
    for group in PACK_ORDER:
        a = src[group[0]][0]
        wd = sum(src[n][1] for n in group)
        if wd % 128:
            o_ref[:, dst:dst + 128] = jnp.zeros((rows, 128), BF16)
        seg = wt_ref[a:a + wd, :].astype(BF16)
        o_ref[:, dst:dst + wd] = lax.dot_general(eye, seg, _NT, preferred_element_type=F32).astype(BF16)
        dst += -(-wd // 128) * 128


def _pack_w_in(w_in):
    depth = w_in.shape[0]
    return pl.pallas_call(
        _pack_body,
        grid=(depth, D_MODEL // PACK_ROWS),
        in_specs=[pl.BlockSpec((None, IN_WIDTH, PACK_ROWS), lambda l, i: (l, 0, i))],
        out_specs=pl.BlockSpec((None, PACK_ROWS, 2 * W_PACK), lambda l, i: (l, i, 0)),
        out_shape=jax.ShapeDtypeStruct((depth, D_MODEL, 2 * W_PACK), BF16),
        compiler_params=_cparams(("arbitrary", "arbitrary")),
        name="pack_w_in",
    )(jnp.swapaxes(w_in, 1, 2))


def _prep_layer(l, prm):
    eye4 = jnp.eye(4, dtype=F32)
    blockdiag = lambda w: jnp.einsum('ncd,nm->ncmd', w, eye4).reshape(BW, BW)
    lru_wg = jnp.concatenate([blockdiag(prm['lru_w_a'][l]), blockdiag(prm['lru_w_x'][l])], axis=1).astype(BF16)
    lru_bg = jnp.concatenate([prm['lru_b_a'][l], prm['lru_b_x'][l]])[None, :]
    lru_sp = jax.nn.softplus(-prm['lru_lambda'][l])[None, :]

    lam_re, lam_im = prm['s5_lam_re'][l], prm['s5_lam_im'][l]
    dt = jnp.exp(prm['s5_log_dt'][l])[:, None]
    mag = jnp.exp(lam_re * dt)
    ang = lam_im * dt
    ar, ai = mag * jnp.cos(ang), mag * jnp.sin(ang)
    den = lam_re * lam_re + lam_im * lam_im
    cr = ((ar - 1.0) * lam_re + ai * lam_im) / den
    ci = (ai * lam_re - (ar - 1.0) * lam_im) / den
    b_re, b_im = prm['s5_b_re'][l], prm['s5_b_im'][l]
    bb_re = cr[..., None] * b_re - ci[..., None] * b_im
    bb_im = cr[..., None] * b_im + ci[..., None] * b_re
    eye_g = jnp.eye(S5_G, dtype=F32)
    in_dense = lambda bb: jnp.einsum('gnc,gh->gchn', bb, eye_g).reshape(BW, S5_W).astype(BF16)
    out_dense = lambda cc: jnp.einsum('gcn,gh->gnhc', cc, eye_g).reshape(S5_W, BW).astype(BF16)

    gate_row = lambda a, b: _pad_lanes(jnp.concatenate([a, b])[None, :], 128)
    zeros4 = jnp.zeros((NH,), F32)
    return dict(
        norm_w=prm['norm_w'][l][None, :],
        w_branch=prm['w_branch'][l].astype(BF16),
        w_out=prm['w_out'][l].astype(BF16),
        lru_cw=prm['lru_conv_w'][l], lru_wg=lru_wg, lru_bg=lru_bg, lru_sp=lru_sp,
        s5_ar=ar.reshape(1, S5_W), s5_ai=ai.reshape(1, S5_W),
        s5_bre=in_dense(bb_re), s5_bim=in_dense(bb_im),
        s5_cre=out_dense(prm['s5_c_re'][l]), s5_cim=out_dense(prm['s5_c_im'][l]),
        s5_d=prm['s5_d'][l][None, :], s5_wglu=prm['s5_w_glu'][l].astype(BF16),
        gdn_cw=prm['gdn_conv_w'][l],
        gdn_nega=gate_row(-jnp.exp(prm['gdn_A_log'][l]), zeros4),
        gdn_dtb=gate_row(prm['gdn_dt_bias'][l], zeros4),
        gdn_nw=jnp.tile(prm['gdn_norm_w'][l], NH)[None, :],
        ml_cw=prm['ml_conv_w'][l],
        ml_bi=gate_row(prm['ml_b_i'][l], zeros4),
        ml_bf=gate_row(zeros4, prm['ml_b_f'][l]),
    )


def _prompt_layer(x, pp, w_pack, l, final_w, final, tm, tt, tg):
    b, t, _ = x.shape
    p_lru, p_s5, p_gdn, p_ml = _in_proj(x, pp['norm_w'], w_pack, l, tm, True)
    z = lambda *s: jnp.zeros(s, F32)
    out_a, lru_h, lru_c = _lru(p_lru, z(b, BW), z(CONV_W - 1, b, BW),
                               pp['lru_cw'], pp['lru_wg'], pp['lru_bg'], pp['lru_sp'], tt)
    out_c, s5_re, s5_im = _s5(p_s5, z(b, S5_W), z(b, S5_W), pp['s5_ar'], pp['s5_ai'],
                              pp['s5_bre'], pp['s5_bim'], pp['s5_cre'], pp['s5_cim'], pp['s5_d'], pp['s5_wglu'], tt)
    out_b, gdn_s, gdn_c, out_d, ml_c, ml_n, ml_m, ml_cv = _chunked(
        p_gdn, z(b, NH, DH, DH), z(b, CONV_W - 1, 3 * BW), pp['gdn_cw'], pp['gdn_nega'], pp['gdn_dtb'], pp['gdn_nw'],
        p_ml, z(b, NH, DH, DH), z(b, 1, BW), z(b, 1, BW), z(b, CONV_W - 1, 2 * BW), pp['ml_cw'], pp['ml_bi'], pp['ml_bf'],
        b, tg)
    y = _merge(x, (out_a, out_b, out_c, out_d), pp['norm_w'], w_pack, l,
               pp['w_branch'], pp['w_out'], final_w, tm, final)
    st = dict(lru_h=lru_h, lru_conv=jnp.transpose(lru_c, (1, 0, 2)), gdn_S=gdn_s, gdn_conv=gdn_c,
              s5_re=s5_re.reshape(b, S5_G, S5_N), s5_im=s5_im.reshape(b, S5_G, S5_N),
              ml_C=ml_c, ml_n=ml_n.reshape(b, NH, DH), ml_m=ml_m.reshape(b, NH, DH)[:, :, 0], ml_conv=ml_cv)
    return y, st


BIG_STATES = ('gdn_S', 'ml_C')
TO_LANES = (0, 2, 3, 4, 1)
FROM_LANES = (0, 4, 1, 2, 3)


def _sample_layer(x, st, big_in, big_acc, l, pp, w_pack, final_w, final):
    nb = x.shape[1]
    p_lru, p_s5, p_gdn, p_ml = _in_proj(x, pp['norm_w'], w_pack, l, nb, False)
    tmaj = lambda c: jnp.transpose(c, (1, 0, 2))
    out_a, lru_h, lru_c = _lru(p_lru.reshape(1, nb, W_LRU), st['lru_h'], tmaj(st['lru_conv']),
                               pp['lru_cw'], pp['lru_wg'], pp['lru_bg'], pp['lru_sp'], 1)
    out_c, s5_re, s5_im = _s5(p_s5.reshape(1, nb, W_S5), st['s5_re'].reshape(nb, S5_W), st['s5_im'].reshape(nb, S5_W),
                              pp['s5_ar'], pp['s5_ai'], pp['s5_bre'], pp['s5_bim'], pp['s5_cre'], pp['s5_cim'],
                              pp['s5_d'], pp['s5_wglu'], 1)
    out_b, gdn_s, gdn_c = _gdn_step(p_gdn, big_in['gdn_S'], big_acc['gdn_S'], l, tmaj(st['gdn_conv']), pp['gdn_cw'],
                                    pp['gdn_nega'], pp['gdn_dtb'], pp['gdn_nw'])
    out_d, ml_c, ml_n, ml_m, ml_cv = _ml_step(p_ml, big_in['ml_C'], big_acc['ml_C'], l,
                                              st['ml_n'].reshape(nb, BW), _pad_lanes(st['ml_m'], 128),
                                              tmaj(st['ml_conv']), pp['ml_cw'], pp['ml_bi'], pp['ml_bf'])
    y = _merge(x, (out_a.reshape(nb, BW), out_b, out_c.reshape(nb, BW), out_d),
               pp['norm_w'], w_pack, l, pp['w_branch'], pp['w_out'], final_w, nb, final)
    new = dict(lru_h=lru_h, lru_conv=tmaj(lru_c), gdn_conv=tmaj(gdn_c),
               s5_re=s5_re.reshape(nb, S5_G, S5_N), s5_im=s5_im.reshape(nb, S5_G, S5_N),
               ml_n=ml_n.reshape(nb, NH, DH), ml_m=ml_m.reshape(nb, NH, DH)[:, :, 0], ml_conv=tmaj(ml_cv))
    return y, new, dict(gdn_S=gdn_s, ml_C=ml_c)


STATE_NAMES = ('lru_h', 'lru_conv', 'gdn_S', 'gdn_conv', 's5_re', 's5_im', 'ml_C', 'ml_n', 'ml_m', 'ml_conv')

TM_ROWS = 64
TT_SCAN = 128


def kernel(x_prompt, x_sample, state_lru_h, state_lru_conv, state_gdn_S, state_gdn_conv, state_s5_re, state_s5_im, state_ml_C, state_ml_n, state_ml_m, state_ml_conv, norm_w, w_in, lru_conv_w, lru_w_a, lru_b_a, lru_w_x, lru_b_x, lru_lambda, gdn_conv_w, gdn_A_log, gdn_dt_bias, gdn_norm_w, s5_lam_re, s5_lam_im, s5_log_dt, s5_b_re, s5_b_im, s5_c_re, s5_c_im, s5_d, s5_w_glu, ml_conv_w, ml_b_i, ml_b_f, w_branch, w_out, final_norm_w):
    prm = dict(norm_w=norm_w, w_in=w_in, lru_conv_w=lru_conv_w, lru_w_a=lru_w_a, lru_b_a=lru_b_a, lru_w_x=lru_w_x,
               lru_b_x=lru_b_x, lru_lambda=lru_lambda, gdn_conv_w=gdn_conv_w, gdn_A_log=gdn_A_log,
               gdn_dt_bias=gdn_dt_bias, gdn_norm_w=gdn_norm_w, s5_lam_re=s5_lam_re, s5_lam_im=s5_lam_im,
               s5_log_dt=s5_log_dt, s5_b_re=s5_b_re, s5_b_im=s5_b_im, s5_c_re=s5_c_re, s5_c_im=s5_c_im, s5_d=s5_d,
               s5_w_glu=s5_w_glu, ml_conv_w=ml_conv_w, ml_b_i=ml_b_i, ml_b_f=ml_b_f, w_branch=w_branch, w_out=w_out)
    depth = w_in.shape[0]
    t = x_prompt.shape[1]
    nb = x_sample.shape[0]
    final_w = final_norm_w[None, :]
    st_in = dict(lru_h=state_lru_h, lru_conv=state_lru_conv, gdn_S=state_gdn_S, gdn_conv=state_gdn_conv,
                 s5_re=state_s5_re, s5_im=state_s5_im, ml_C=state_ml_C, ml_n=state_ml_n, ml_m=state_ml_m,
                 ml_conv=state_ml_conv)
    tm = min(TM_ROWS, t)
    tt = min(TT_SCAN, t)
    tg = min(SE, t)

    xp = x_prompt
    xs = x_sample.reshape(1, nb, D_MODEL)
    small = [n for n in STATE_NAMES if n not in BIG_STATES]
    new_p = {n: [] for n in STATE_NAMES}
    new_s = {n: [] for n in small}
    big_in = {n: jnp.transpose(st_in[n], TO_LANES) for n in BIG_STATES}
    big_acc = {n: None for n in BIG_STATES}
    w_pack = _pack_w_in(w_in)
    for l in range(depth):
        pp = _prep_layer(l, prm)
        final = l == depth - 1
        xp, sp = _prompt_layer(xp, pp, w_pack, l, final_w, final, tm, tt, tg)
        xs, ss, big_acc = _sample_layer(xs, {n: st_in[n][l] for n in small}, big_in, big_acc, l, pp, w_pack,
                                        final_w, final)
        for n in STATE_NAMES:
            new_p[n].append(sp[n])
        for n in small:
            new_s[n].append(ss[n])
    np_ = {n: jnp.stack(v) for n, v in new_p.items()}
    ns_ = {n: jnp.stack(v) for n, v in new_s.items()}
    ns_.update({n: jnp.transpose(big_acc[n], FROM_LANES) for n in BIG_STATES})
    y_sample = xs.reshape(nb, 1, D_MODEL)
    return (xp, y_sample,
            np_['lru_h'], ns_['lru_h'], np_['lru_conv'], ns_['lru_conv'],
            np_['gdn_S'], ns_['gdn_S'], np_['gdn_conv'], ns_['gdn_conv'],
            np_['s5_re'], ns_['s5_re'], np_['s5_im'], ns_['s5_im'],
            np_['ml_C'], ns_['ml_C'], np_['ml_n'], ns_['ml_n'], np_['ml_m'], ns_['ml_m'],
            np_['ml_conv'], ns_['ml_conv'])
```

```python
import functools
import math

import jax
import jax.numpy as jnp
from jax import lax
from jax.experimental import pallas as pl
from jax.experimental.pallas import tpu as pltpu

F32 = jnp.float32
BF16 = jnp.bfloat16

D_MODEL = 1024
BW = 256
NH = 4
DH = 64
CONV_W = 4
CHUNK = 64
LRU_C = 8.0
S5_G = 16
S5_N = 64
S5_W = S5_G * S5_N
EPS = 1e-6

W_LRU = 2 * BW
W_S5 = 2 * BW
W_GDN = 4 * BW + 128
W_ML = 5 * BW + 128
W_MIX = (W_LRU, W_S5, W_GDN, W_ML)

VMEM_LIMIT_BYTES = 56 * 1024 * 1024


def _cparams(sem):
    return pltpu.CompilerParams(dimension_semantics=sem, vmem_limit_bytes=VMEM_LIMIT_BYTES)


def _sigmoid(x):
    return 1.0 / (1.0 + jnp.exp(-x))


def _silu(x):
    return x * _sigmoid(x)


def _softplus(x):
    return jnp.maximum(x, 0.0) + jnp.log(1.0 + jnp.exp(-jnp.abs(x)))


def _gelu_tanh(x):
    c = math.sqrt(2.0 / math.pi)
    return x * (0.5 * (1.0 + jnp.tanh(c * (x + 0.044715 * (x * x * x)))))


def _split3(x):
    hi = x.astype(BF16)
    r = x - hi.astype(F32)
    mid = r.astype(BF16)
    lo = (r - mid.astype(F32)).astype(BF16)
    return hi, mid, lo


def _split2(x):
    hi = x.astype(BF16)
    return hi, (x - hi.astype(F32)).astype(BF16)


_NN = (((1,), (0,)), ((), ()))
_NT = (((1,), (1,)), ((), ()))
_TN = (((0,), (0,)), ((), ()))


def _sel_dot(sel, x, dims=_NN, split=_split3):
    out = None
    for p in split(x):
        t = lax.dot_general(sel, p, dims, preferred_element_type=F32)
        out = t if out is None else out + t
    return out


def _dot_sel(x, sel, dims=_NN, split=_split3):
    out = None
    for p in split(x):
        t = lax.dot_general(p, sel, dims, preferred_element_type=F32)
        out = t if out is None else out + t
    return out


def _iota(shape, axis):
    return lax.broadcasted_iota(jnp.int32, shape, axis)


def _head_ones():
    return (_iota((BW, BW), 0) // DH == _iota((BW, BW), 1) // DH).astype(BF16)


def _expand_sel(offset):
    return (_iota((128, BW), 0) == _iota((128, BW), 1) // DH + offset).astype(BF16)


def _rms_rows(x, w_row):
    ms = jnp.mean(x * x, axis=-1, keepdims=True)
    return (x * lax.rsqrt(ms + EPS)) * w_row


def _inproj_body(x_ref, nw_ref, w_ref, *o_refs):
    nb, tm, _ = x_ref.shape
    xn = _rms_rows(x_ref[...].reshape(nb * tm, D_MODEL), nw_ref[...]).astype(BF16)
    off = 0
    for o_ref in o_refs:
        wd = o_ref.shape[-1] if len(o_ref.shape) == 3 else o_ref.shape[-1] // nb
        y = jnp.dot(xn, w_ref[:, off:off + wd], preferred_element_type=F32)
        if len(o_ref.shape) == 3:
            o_ref[...] = jnp.swapaxes(y.reshape(nb, tm, wd), 0, 1)
        else:
            for bi in range(nb):
                o_ref[:, bi * wd:(bi + 1) * wd] = y[bi * tm:(bi + 1) * tm]
        off += wd


W_PACK = NH * D_MODEL


def _in_proj(x, norm_w, w_pack, l, tm, scan_3d):
    b, t, _ = x.shape
    assert sum(W_MIX) <= W_PACK
    specs, shapes = [], []
    for n, wd in enumerate(W_MIX):
        if scan_3d and n < 2:
            specs.append(pl.BlockSpec((tm, b, wd), lambda i: (i, 0, 0)))
            shapes.append(jax.ShapeDtypeStruct((t, b, wd), F32))
        else:
            specs.append(pl.BlockSpec((tm, b * wd), lambda i: (i, 0)))
            shapes.append(jax.ShapeDtypeStruct((t, b * wd), F32))
    return pl.pallas_call(
        _inproj_body,
        grid=(t // tm,),
        in_specs=[
            pl.BlockSpec((b, tm, D_MODEL), lambda i: (0, i, 0)),
            pl.BlockSpec((1, D_MODEL), lambda i: (0, 0)),
            pl.BlockSpec((None, D_MODEL, W_PACK), lambda i: (l, 0, 1)),
        ],
        out_specs=specs,
        out_shape=shapes,
        compiler_params=_cparams(("arbitrary",)),
        name="in_proj",
    )(x, norm_w, w_pack)


def _merge_body(x_ref, a_ref, b_ref, c_ref, d_ref, nw_ref, wm_ref, wb_ref, wo_ref, fw_ref, y_ref, *, final):
    nb, tm, _ = x_ref.shape
    x = x_ref[...].reshape(nb * tm, D_MODEL)
    xn = _rms_rows(x, nw_ref[...]).astype(BF16)
    merged = None
    for n, r in enumerate((a_ref, b_ref, c_ref, d_ref)):
        if len(r.shape) == 3:
            br = jnp.swapaxes(r[...], 0, 1).reshape(nb * tm, BW)
        else:
            br = jnp.concatenate([r[:, bi * BW:(bi + 1) * BW] for bi in range(nb)], axis=0)
        gate = _sigmoid(jnp.dot(xn, wm_ref[:, n * D_MODEL:(n + 1) * D_MODEL], preferred_element_type=F32))
        proj = jnp.dot(br.astype(BF16), wb_ref[n], preferred_element_type=F32)
        merged = gate * proj if merged is None else merged + gate * proj
    y = x + jnp.dot(merged.astype(BF16), wo_ref[...], preferred_element_type=F32)
    if final:
        y = _rms_rows(y, fw_ref[...])
    y_ref[...] = y.reshape(nb, tm, D_MODEL)


def _merge(x, branches, norm_w, w_pack, l, w_branch, w_out, final_w, tm, final):
    b, t, _ = x.shape
    const2 = lambda i: (0, 0)
    br_specs = [pl.BlockSpec((tm,) + r.shape[1:], (lambda i: (i, 0, 0)) if r.ndim == 3 else (lambda i: (i, 0)))
                for r in branches]
    return pl.pallas_call(
        functools.partial(_merge_body, final=final),
        grid=(t // tm,),
        in_specs=[
            pl.BlockSpec((b, tm, D_MODEL), lambda i: (0, i, 0)),
            *br_specs,
            pl.BlockSpec((1, D_MODEL), const2),
            pl.BlockSpec((None, D_MODEL, W_PACK), lambda i: (l, 0, 0)),
            pl.BlockSpec((NH, BW, D_MODEL), lambda i: (0, 0, 0)),
            pl.BlockSpec((D_MODEL, D_MODEL), const2),
            pl.BlockSpec((1, D_MODEL), const2),
        ],
        out_specs=pl.BlockSpec((b, tm, D_MODEL), lambda i: (0, i, 0)),
        out_shape=jax.ShapeDtypeStruct((b, t, D_MODEL), F32),
        compiler_params=_cparams(("arbitrary",)),
        name="merge",
    )(x, *branches, norm_w, w_pack, w_branch, w_out, final_w)


def _lru_body(p_ref, h0_ref, c0_ref, cw_ref, wg_ref, bg_ref, sp_ref, out_ref, ht_ref, ct_ref,
              xp_s, a_s, b_s, h_s):
    tt, nb = p_ref.shape[0], p_ref.shape[1]

    @pl.when(pl.program_id(0) == 0)
    def _():
        h_s[...] = h0_ref[...]
        xp_s[0:CONV_W - 1] = c0_ref[...]

    xp_s[CONV_W - 1:CONV_W - 1 + tt] = p_ref[:, :, 0:BW]
    xa = None
    for j in range(CONV_W):
        term = xp_s[j:j + tt] * cw_ref[j:j + 1, :].reshape(1, 1, BW)
        xa = term if xa is None else xa + term
    tail = xp_s[tt:tt + CONV_W - 1]
    xp_s[0:CONV_W - 1] = tail
    ct_ref[...] = tail

    xa2 = xa.reshape(tt * nb, BW)
    gates = jnp.dot(xa2.astype(BF16), wg_ref[...], preferred_element_type=F32) + bg_ref[...]
    r = _sigmoid(gates[:, 0:BW])
    ig = _sigmoid(gates[:, BW:2 * BW])
    log_a = (-LRU_C * r) * sp_ref[...]
    a = jnp.exp(log_a)
    bb = jnp.sqrt(1.0 - jnp.exp(2.0 * log_a)) * (ig * xa2)
    a_s[...] = a.reshape(tt, nb, BW)
    b_s[...] = bb.reshape(tt, nb, BW)

    def step(t, h):
        h = a_s[t] * h + b_s[t]
        b_s[t] = h
        return h

    h = lax.fori_loop(0, tt, step, h_s[...], unroll=min(8, tt))
    h_s[...] = h
    ht_ref[...] = h
    out_ref[...] = b_s[...] * _silu(p_ref[:, :, BW:2 * BW])


def _lru(p, h0, c0, conv_w, w_gates, b_gates, sp_lam, tt):
    t, nb, _ = p.shape
    c2 = lambda i: (0, 0)
    c3 = lambda i: (0, 0, 0)
    return pl.pallas_call(
        _lru_body,
        grid=(t // tt,),
        in_specs=[
            pl.BlockSpec((tt, nb, W_LRU), lambda i: (i, 0, 0)),
            pl.BlockSpec((nb, BW), c2),
            pl.BlockSpec((CONV_W - 1, nb, BW), c3),
            pl.BlockSpec((CONV_W, BW), c2),
            pl.BlockSpec((BW, 2 * BW), c2),
            pl.BlockSpec((1, 2 * BW), c2),
            pl.BlockSpec((1, BW), c2),
        ],
        out_specs=[
            pl.BlockSpec((tt, nb, BW), lambda i: (i, 0, 0)),
            pl.BlockSpec((nb, BW), c2),
            pl.BlockSpec((CONV_W - 1, nb, BW), c3),
        ],
        out_shape=[
            jax.ShapeDtypeStruct((t, nb, BW), F32),
            jax.ShapeDtypeStruct((nb, BW), F32),
            jax.ShapeDtypeStruct((CONV_W - 1, nb, BW), F32),
        ],
        scratch_shapes=[
            pltpu.VMEM((tt + CONV_W - 1, nb, BW), F32),
            pltpu.VMEM((tt, nb, BW), F32),
            pltpu.VMEM((tt, nb, BW), F32),
            pltpu.VMEM((nb, BW), F32),
        ],
        compiler_params=_cparams(("arbitrary",)),
        name="rglru",
    )(p, h0, c0, conv_w, w_gates, b_gates, sp_lam)


def _s5_body(p_ref, hr0_ref, hi0_ref, ar_ref, ai_ref, bre_ref, bim_ref, cre_ref, cim_ref, d_ref, wglu_ref,
             out_ref, hrt_ref, hit_ref, xr_s, xi_s, hr_s, hi_s):
    tt, nb = p_ref.shape[0], p_ref.shape[1]

    @pl.when(pl.program_id(0) == 0)
    def _():
        hr_s[...] = hr0_ref[...]
        hi_s[...] = hi0_ref[...]

    u2 = p_ref[:, :, 0:BW].reshape(tt * nb, BW)
    ub = u2.astype(BF16)
    xr_s[...] = jnp.dot(ub, bre_ref[...], preferred_element_type=F32).reshape(tt, nb, S5_W)
    xi_s[...] = jnp.dot(ub, bim_ref[...], preferred_element_type=F32).reshape(tt, nb, S5_W)
    ar = jnp.broadcast_to(ar_ref[...], (nb, S5_W))
    ai = jnp.broadcast_to(ai_ref[...], (nb, S5_W))

    def step(t, c):
        hr, hi = c
        nr = ar * hr - ai * hi + xr_s[t]
        ni = ar * hi + ai * hr + xi_s[t]
        xr_s[t] = nr
        xi_s[t] = ni
        return nr, ni

    hr, hi = lax.fori_loop(0, tt, step, (hr_s[...], hi_s[...]), unroll=min(4, tt))
    hr_s[...] = hr
    hi_s[...] = hi
    hrt_ref[...] = hr
    hit_ref[...] = hi

    y = (jnp.dot(xr_s[...].reshape(tt * nb, S5_W).astype(BF16), cre_ref[...], preferred_element_type=F32)
         - jnp.dot(xi_s[...].reshape(tt * nb, S5_W).astype(BF16), cim_ref[...], preferred_element_type=F32)
         + d_ref[...] * u2)
    y = _gelu_tanh(y)
    y = y * _sigmoid(jnp.dot(y.astype(BF16), wglu_ref[...], preferred_element_type=F32))
    out_ref[...] = y.reshape(tt, nb, BW) * _silu(p_ref[:, :, BW:2 * BW])


def _s5(p, hr0, hi0, ar, ai, bre, bim, cre, cim, d, wglu, tt):
    t, nb, _ = p.shape
    c2 = lambda i: (0, 0)
    st = pl.BlockSpec((nb, S5_W), c2)
    row = pl.BlockSpec((1, S5_W), c2)
    return pl.pallas_call(
        _s5_body,
        grid=(t // tt,),
        in_specs=[
            pl.BlockSpec((tt, nb, W_S5), lambda i: (i, 0, 0)),
            st, st, row, row,
            pl.BlockSpec((BW, S5_W), c2), pl.BlockSpec((BW, S5_W), c2),
            pl.BlockSpec((S5_W, BW), c2), pl.BlockSpec((S5_W, BW), c2),
            pl.BlockSpec((1, BW), c2),
            pl.BlockSpec((BW, BW), c2),
        ],
        out_specs=[pl.BlockSpec((tt, nb, BW), lambda i: (i, 0, 0)), st, st],
        out_shape=[
            jax.ShapeDtypeStruct((t, nb, BW), F32),
            jax.ShapeDtypeStruct((nb, S5_W), F32),
            jax.ShapeDtypeStruct((nb, S5_W), F32),
        ],
        scratch_shapes=[
            pltpu.VMEM((tt, nb, S5_W), F32),
            pltpu.VMEM((tt, nb, S5_W), F32),
            pltpu.VMEM((nb, S5_W), F32),
            pltpu.VMEM((nb, S5_W), F32),
        ],
        compiler_params=_cparams(("arbitrary",)),
        name="s5",
    )(p, hr0, hi0, ar, ai, bre, bim, cre, cim, d, wglu)


def _conv_rows(xp_s, p_cols, cw_ref, tg):
    xp_s[8:8 + tg] = p_cols
    y = None
    for j in range(CONV_W):
        term = xp_s[8 - (CONV_W - 1) + j:8 - (CONV_W - 1) + j + tg] * cw_ref[j:j + 1, :]
        y = term if y is None else y + term
    tail = xp_s[8 + tg - (CONV_W - 1):8 + tg]
    xp_s[8 - (CONV_W - 1):8] = tail
    return y, tail


SE = NH * CHUNK


def _se_masks():
    r = _iota((SE, SE), 0)
    c = _iota((SE, SE), 1)
    same = (r // CHUNK) == (c // CHUNK)
    return same, same & ((r % CHUNK) >= (c % CHUNK))


def _cols_masks(n):
    i = _iota((n, BW), 0) % CHUNK
    j = _iota((n, BW), 1) % CHUNK
    return i >= j, i > j, i == j


def _stack_se(x_bf, same):
    return jnp.where(same, jnp.concatenate([x_bf] * NH, axis=0), jnp.zeros((), x_bf.dtype))


def _tile_rows(x):
    return jnp.concatenate([x] * NH, axis=0)


def _fold_se(x_se):
    out = x_se[0:CHUNK]
    for h in range(1, NH):
        out = out + x_se[h * CHUNK:(h + 1) * CHUNK]
    return out


def _load_cols(s_s, blocks_ref):
    for h in range(NH):
        s_s[:, h * DH:(h + 1) * DH] = blocks_ref[h]


def _store_cols(blocks_ref, s_s):
    for h in range(NH):
        blocks_ref[h] = s_s[:, h * DH:(h + 1) * DH]


def _last_rows(x, nc):
    return jnp.concatenate([jnp.broadcast_to(x[c * CHUNK + CHUNK - 1:(c + 1) * CHUNK], (CHUNK, BW)) for c in range(nc)], axis=0)


def _gdn_init(s0_ref, c0_ref, xp_s, s_s):
    _load_cols(s_s, s0_ref)
    xp_s[8 - (CONV_W - 1):8] = c0_ref[...]


def _gdn_main(p_ref, cw_ref, nega_ref, dtb_ref, nw_ref, out_ref, st_ref, ct_ref, xp_s, o_s, s_s):
    tg = p_ref.shape[0]
    nc = tg // CHUNK
    y, tail = _conv_rows(xp_s, p_ref[:, 0:3 * BW], cw_ref, tg)
    ct_ref[...] = tail
    qkv = _silu(y)
    ones_h = _head_ones()
    q = qkv[:, 0:BW]
    k = qkv[:, BW:2 * BW]
    q = q * lax.rsqrt(_dot_sel(q * q, ones_h, split=_split2) + EPS) * (DH ** -0.5)
    k = k * lax.rsqrt(_dot_sel(k * k, ones_h, split=_split2) + EPS)
    v = qkv[:, 2 * BW:3 * BW]
    yield
    ga = p_ref[:, 4 * BW:4 * BW + 128]
    lane = _iota((tg, 128), 1)
    g = nega_ref[...] * _softplus(ga + dtb_ref[...])
    gall = jnp.where(lane < NH, g, _sigmoid(ga))

    same, tri = _se_masks()
    tri_c, strict_c, _ = _cols_masks(tg)
    tri_bf = tri.astype(BF16)
    beta_x = _dot_sel(gall, _expand_sel(NH), split=_split2)
    g_x = _dot_sel(gall, _expand_sel(0), split=_split2)
    gc_x = _sel_dot(tri_bf, g_x, split=_split2)
    gl_x = _last_rows(gc_x, nc)
    egc = jnp.exp(gc_x)
    kb = k * beta_x
    k_b = k.astype(BF16)
    kb_b = kb.astype(BF16)
    q_b = q.astype(BF16)
    vb_b = (v * beta_x).astype(BF16)
    kbg_b = (kb * egc).astype(BF16)
    qd_b = (q * egc).astype(BF16)
    kd_b = (k * jnp.exp(gl_x - gc_x)).astype(BF16)
    yield

    cs = range(nc)
    rows = [slice(c * CHUNK, (c + 1) * CHUNK) for c in cs]
    diff = _sel_dot(tri_bf, jnp.where(strict_c, g_x, 0.0), split=_split2)
    dec_all = jnp.where(tri_c, jnp.exp(diff), 0.0)
    dec_t = [_tile_rows(dec_all[rows[c]]) for c in cs]
    ndec_all = jnp.where(strict_c, -dec_all, 0.0)
    ndec_t = [_tile_rows(ndec_all[rows[c]]) for c in cs]

    dot = functools.partial(jnp.dot, preferred_element_type=F32)
    cat = functools.partial(jnp.concatenate, axis=0)
    k_se = [_stack_se(k_b[rows[c]], same) for c in cs]
    kq = [lax.dot_general(cat([_stack_se(kb_b[rows[c]], same), _stack_se(q_b[rows[c]], same)]), k_se[c], _NT,
                          preferred_element_type=F32) for c in cs]
    attn = [(kq[c][SE:] * dec_t[c]).astype(BF16) for c in cs]
    eye = (_iota((SE, SE), 0) == _iota((SE, SE), 1)).astype(F32)
    npow = [kq[c][:SE] * ndec_t[c] for c in cs]
    tinv = [eye + npow[c] for c in cs]
    nb = [n.astype(BF16) for n in npow]
    yield
    nb = [dot(n, n).astype(BF16) for n in nb]
    for _ in range(4):
        yield
        both = [dot(cat([t.astype(BF16), n]), n) for t, n in zip(tinv, nb)]
        tinv = [t + b2[:SE] for t, b2 in zip(tinv, both)]
        nb = [b2[SE:].astype(BF16) for b2 in both]
    yield
    tinv = [t + dot(t.astype(BF16), n) for t, n in zip(tinv, nb)]
    t_bf = [t.astype(BF16) for t in tinv]
    u = [dot(t_bf[c], _stack_se(vb_b[rows[c]], same)) for c in cs]
    w = [dot(t_bf[c], _stack_se(kbg_b[rows[c]], same)).astype(BF16) for c in cs]

    kd_t = [_stack_se(kd_b[rows[c]], same).T for c in cs]
    s = s_s[...]
    yield _READY
    for c in cs:
        sb = _stack_se(s.astype(BF16), same)
        ws = dot(cat([w[c], _stack_se(qd_b[rows[c]], same)]), sb)
        vb = (u[c] - ws[:SE]).astype(BF16)
        av = dot(cat([attn[c], kd_t[c]]), vb)
        s = s * jnp.exp(gl_x[c * CHUNK:c * CHUNK + 1]) + _fold_se(av[SE:])
        o_s[rows[c], :] = _fold_se(ws[SE:] + av[:SE])
        yield
    s_s[...] = s
    _store_cols(st_ref, s_s)

    o = o_s[...]
    ms = _dot_sel(o * o, ones_h, split=_split2) * (1.0 / DH)
    out_ref[...] = (o * lax.rsqrt(ms + EPS)) * nw_ref[...] * _silu(p_ref[:, 3 * BW:4 * BW])


def _cummax_rows(x):
    row = _iota(x.shape, 0)
    sh = 1
    while sh < x.shape[0]:
        x = jnp.maximum(x, jnp.where(row >= sh, pltpu.roll(x, sh, 0), -jnp.inf))
        sh *= 2
    return x


def _ml_init(c0m_ref, n0_ref, m0_ref, c0_ref, nt_ref, mt_ref, xp_s, cm_s):
    _load_cols(cm_s, c0m_ref)
    nt_ref[...] = n0_ref[...]
    mt_ref[...] = m0_ref[...]
    xp_s[8 - (CONV_W - 1):8] = c0_ref[...]


def _ml_main(p_ref, cw_ref, bi_ref, bf_ref, out_ref, ct_m_ref, nt_ref, mt_ref, cvt_ref, xp_s, h_s, cm_s):
    tg = p_ref.shape[0]
    nc = tg // CHUNK
    y, tail = _conv_rows(xp_s, p_ref[:, 0:2 * BW], cw_ref, tg)
    cvt_ref[...] = tail
    qk = _silu(y)
    q = qk[:, 0:BW]
    k = qk[:, BW:2 * BW] * (DH ** -0.5)
    yield
    ga = p_ref[:, 5 * BW:5 * BW + 128]
    lane = _iota((tg, 128), 1)
    gall = jnp.where(lane < NH, ga + bi_ref[...], -_softplus(-(ga + bf_ref[...])))

    same, tri = _se_masks()
    tri_c, strict_c, eye_c = _cols_masks(tg)
    tri_bf = tri.astype(BF16)
    ones_h = _head_ones()
    i_x = _dot_sel(gall, _expand_sel(0), split=_split2)
    lf_x = _dot_sel(gall, _expand_sel(NH), split=_split2)
    b_x = _sel_dot(tri_bf, lf_x, split=_split2)
    r_x = i_x - b_x
    e_x = _last_rows(b_x, nc) - b_x + i_x
    q_b = q.astype(BF16)
    k_b = k.astype(BF16)
    v_b = p_ref[:, 2 * BW:3 * BW].astype(BF16)
    yield

    cs = range(nc)
    rows = [slice(c * CHUNK, (c + 1) * CHUNK) for c in cs]
    dl_all = (_sel_dot(tri_bf, jnp.where(strict_c, lf_x, 0.0), split=_split2)
              + _sel_dot(same.astype(BF16), jnp.where(eye_c, i_x, 0.0), split=_split2))
    dl_all = jnp.where(tri_c, dl_all, -jnp.inf)
    dlog = [dl_all[rows[c]] for c in cs]
    yield
    dmax = [b_x[rows[c]] + _cummax_rows(r_x[rows[c]]) for c in cs]
    yield
    q_se = [_stack_se(q_b[rows[c]], same) for c in cs]
    v_se = [_stack_se(v_b[rows[c]], same) for c in cs]
    qkm = [_fold_se(lax.dot_general(q_se[c], _stack_se(k_b[rows[c]], same), _NT, preferred_element_type=F32)) for c in cs]

    cm = cm_s[...]
    nrow = nt_ref[...]
    mrow = mt_ref[...]
    yield _READY
    for c in cs:
        bx = b_x[rows[c]]
        inter = bx + mrow
        m_t = jnp.maximum(inter, dmax[c])
        s = qkm[c] * jnp.exp(dlog[c] - m_t)
        w_int = jnp.exp(inter - m_t)
        num = (w_int * _fold_se(jnp.dot(q_se[c], _stack_se(cm.astype(BF16), same), preferred_element_type=F32))
               + _fold_se(jnp.dot(_stack_se(s.astype(BF16), same), v_se[c], preferred_element_type=F32)))
        den = w_int * _dot_sel(q[rows[c]] * nrow, ones_h, split=_split2) + _dot_sel(s, ones_h, split=_split2)
        h_s[rows[c], :] = num / jnp.maximum(jnp.abs(den), jnp.exp(-m_t))
        blast = bx[CHUNK - 1:CHUNK]
        ex = e_x[rows[c]]
        m_new = jnp.maximum(blast + mrow, jnp.max(ex, axis=0, keepdims=True))
        dec = jnp.exp(blast + mrow - m_new)
        wk = jnp.exp(ex - m_new) * k[rows[c]]
        cm = dec * cm + _fold_se(lax.dot_general(_stack_se(wk.astype(BF16), same), v_se[c], _TN, preferred_element_type=F32))
        nrow = dec * nrow + jnp.sum(wk, axis=0, keepdims=True)
        mrow = m_new
        yield
    cm_s[...] = cm
    _store_cols(ct_m_ref, cm_s)
    nt_ref[...] = nrow
    mt_ref[...] = mrow
    out_ref[...] = h_s[...] * _sigmoid(p_ref[:, 3 * BW:4 * BW]) * _silu(p_ref[:, 4 * BW:5 * BW])


_READY, _DONE = object(), object()


def _chunked_body(pg_ref, s0_ref, gc0_ref, gcw_ref, nega_ref, dtb_ref, gnw_ref,
                  pm_ref, c0m_ref, n0_ref, m0_ref, mc0_ref, mcw_ref, bi_ref, bf_ref,
                  og_ref, st_ref, gct_ref, om_ref, ctm_ref, nt_ref, mt_ref, mct_ref,
                  gxp_s, go_s, gs_s, mxp_s, mh_s, mcm_s):
    @pl.when(pl.program_id(1) == 0)
    def _():
        _gdn_init(s0_ref, gc0_ref, gxp_s, gs_s)
        _ml_init(c0m_ref, n0_ref, m0_ref, mc0_ref, nt_ref, mt_ref, mxp_s, mcm_s)

    live = [_gdn_main(pg_ref, gcw_ref, nega_ref, dtb_ref, gnw_ref, og_ref, st_ref, gct_ref, gxp_s, go_s, gs_s),
            _ml_main(pm_ref, mcw_ref, bi_ref, bf_ref, om_ref, ctm_ref, nt_ref, mt_ref, mct_ref, mxp_s, mh_s, mcm_s)]
    pre = list(live)
    while pre:
        pre = [g for g in pre if next(g) is not _READY]
    while live:
        live = [g for g in live if next(g, _DONE) is not _DONE]


def _chunked(pg, s0, gc0, gcw, neg_a, dt_bias, gnw, pm, c0m, n0, m0, mc0, mcw, b_i, b_f, b, tg):
    assert tg == SE
    t = pg.shape[0]
    c2 = lambda bi, i: (0, 0)
    per_b3 = lambda bi, i: (bi, 0, 0)
    per_b4 = lambda bi, i: (bi, 0, 0, 0)
    tile = lambda wd: pl.BlockSpec((tg, wd), lambda bi, i: (i, bi))
    st4 = pl.BlockSpec((None, NH, DH, DH), per_b4)
    row = pl.BlockSpec((None, 1, BW), per_b3)
    gconv = pl.BlockSpec((None, CONV_W - 1, 3 * BW), per_b3)
    mconv = pl.BlockSpec((None, CONV_W - 1, 2 * BW), per_b3)
    gate = pl.BlockSpec((1, 128), c2)
    return pl.pallas_call(
        _chunked_body,
        grid=(b, t // tg),
        in_specs=[
            tile(W_GDN), st4, gconv, pl.BlockSpec((CONV_W, 3 * BW), c2), gate, gate, pl.BlockSpec((1, BW), c2),
            tile(W_ML), st4, row, row, mconv, pl.BlockSpec((CONV_W, 2 * BW), c2), gate, gate,
        ],
        out_specs=[tile(BW), st4, gconv, tile(BW), st4, row, row, mconv],
        out_shape=[
            jax.ShapeDtypeStruct((t, b * BW), F32),
            jax.ShapeDtypeStruct((b, NH, DH, DH), F32),
            jax.ShapeDtypeStruct((b, CONV_W - 1, 3 * BW), F32),
            jax.ShapeDtypeStruct((t, b * BW), F32),
            jax.ShapeDtypeStruct((b, NH, DH, DH), F32),
            jax.ShapeDtypeStruct((b, 1, BW), F32),
            jax.ShapeDtypeStruct((b, 1, BW), F32),
            jax.ShapeDtypeStruct((b, CONV_W - 1, 2 * BW), F32),
        ],
        scratch_shapes=[
            pltpu.VMEM((tg + 8, 3 * BW), F32), pltpu.VMEM((tg, BW), F32), pltpu.VMEM((DH, BW), F32),
            pltpu.VMEM((tg + 8, 2 * BW), F32), pltpu.VMEM((tg, BW), F32), pltpu.VMEM((DH, BW), F32),
        ],
        compiler_params=_cparams(("arbitrary", "arbitrary")),
        name="chunked",
    )(pg, s0, gc0, gcw, neg_a, dt_bias, gnw, pm, c0m, n0, m0, mc0, mcw, b_i, b_f)


def _transpose_rows(x):
    eye = (_iota((BW, BW), 0) == _iota((BW, BW), 1)).astype(BF16)
    return _sel_dot(eye, x, _NT)


def _full(shape):
    nd = len(shape)
    return pl.BlockSpec(shape, lambda i: (0,) * nd)


def _head_rows(ref, h):
    return ref[pl.ds(pl.multiple_of(h * DH, DH), DH), :]


def _gdn_step_body(p_ref, s_ref, c_ref, cw_ref, nega_ref, dtb_ref, nw_ref, out_ref, so_ref, co_ref,
                   kt_s, qet_s, wt_s, egt_s, vbt_s, qkt_s, ot_s):
    h = pl.program_id(0)

    @pl.when(h == 0)
    def _():
        x = p_ref[:, 0:3 * BW]
        xa = x * cw_ref[CONV_W - 1:CONV_W, :]
        for j in range(CONV_W - 1):
            xa = xa + c_ref[j] * cw_ref[j:j + 1, :]
        co_ref[0] = c_ref[1]
        co_ref[1] = c_ref[2]
        co_ref[2] = x
        qkv = _silu(xa)
        ones_h = _head_ones()
        q = qkv[:, 0:BW]
        k = qkv[:, BW:2 * BW]
        q = q * lax.rsqrt(_dot_sel(q * q, ones_h) + EPS) * (DH ** -0.5)
        k = k * lax.rsqrt(_dot_sel(k * k, ones_h) + EPS)
        v = qkv[:, 2 * BW:3 * BW]
        ga = p_ref[:, 4 * BW:4 * BW + 128]
        g = nega_ref[...] * _softplus(ga + dtb_ref[...])
        eg = jnp.exp(_dot_sel(g, _expand_sel(0)))
        be = _dot_sel(_sigmoid(ga), _expand_sel(NH))
        kt_s[...] = _transpose_rows(k)
        qet_s[...] = _transpose_rows(q * eg)
        wt_s[...] = _transpose_rows(k * be * eg)
        egt_s[...] = _transpose_rows(eg)
        vbt_s[...] = _transpose_rows(v * be)
        qkt_s[...] = _transpose_rows(_dot_sel(q * k, ones_h))

    r0 = pl.multiple_of(h * DH, DH)

    def reduce_k(kk, acc):
        aw, aq = acc
        s = s_ref[kk]
        return aw + s * wt_s[pl.ds(r0 + kk, 1), :], aq + s * qet_s[pl.ds(r0 + kk, 1), :]

    nbl = s_ref.shape[-1]
    zero = jnp.zeros((DH, nbl), F32)
    aw, aq = lax.fori_loop(0, DH, reduce_k, (zero, zero), unroll=8)
    v_new = _head_rows(vbt_s, h) - aw
    ot_s[pl.ds(r0, DH), :] = aq + qkt_s[pl.ds(r0, 1), :] * v_new

    def update_k(kk, carry):
        so_ref[kk] = s_ref[kk] * egt_s[pl.ds(r0 + kk, 1), :] + kt_s[pl.ds(r0 + kk, 1), :] * v_new
        return carry

    lax.fori_loop(0, DH, update_k, 0, unroll=8)

    @pl.when(h == NH - 1)
    def _():
        o = ot_s[...].T
        ms = _dot_sel(o * o, _head_ones()) * (1.0 / DH)
        out_ref[...] = (o * lax.rsqrt(ms + EPS)) * nw_ref[...] * _silu(p_ref[:, 3 * BW:4 * BW])


def _lanes_block(nb, l):
    return pl.BlockSpec((None, None, DH, DH, nb), lambda h: (l, h, 0, 0, 0))


def _gdn_step(p, s_all, s_acc, l, c0, conv_w, neg_a, dt_bias, norm_w):
    nb = p.shape[0]
    ins = (p, s_all, c0, conv_w, neg_a, dt_bias, norm_w)
    in_specs = [_full(a.shape) for a in ins]
    in_specs[1] = _lanes_block(nb, l)
    return pl.pallas_call(
        lambda *refs: _gdn_step_body(*refs[:7], *refs[8:]),
        grid=(NH,),
        in_specs=in_specs + [pl.BlockSpec(memory_space=pl.ANY)],
        out_specs=[_full((nb, BW)), _lanes_block(nb, l), _full((CONV_W - 1, nb, 3 * BW))],
        out_shape=[
            jax.ShapeDtypeStruct((nb, BW), F32),
            jax.ShapeDtypeStruct(s_all.shape, F32),
            jax.ShapeDtypeStruct((CONV_W - 1, nb, 3 * BW), F32),
        ],
        input_output_aliases={7: 1},
        scratch_shapes=[pltpu.VMEM((BW, nb), F32) for _ in range(7)],
        compiler_params=_cparams(("arbitrary",)),
        name="gdn_step",
    )(*ins, s_acc)


def _ml_step_body(p_ref, c_ref, n_ref, m_ref, cv_ref, cw_ref, bi_ref, bf_ref, out_ref, co_ref, no_ref, mo_ref, cvo_ref,
                  qwt_s, qt_s, wit_s, wkt_s, vt_s, st_s, flt_s, nt_s, ht_s, nto_s):
    h = pl.program_id(0)

    @pl.when(h == 0)
    def _():
        x = p_ref[:, 0:2 * BW]
        xa = x * cw_ref[CONV_W - 1:CONV_W, :]
        for j in range(CONV_W - 1):
            xa = xa + cv_ref[j] * cw_ref[j:j + 1, :]
        cvo_ref[0] = cv_ref[1]
        cvo_ref[1] = cv_ref[2]
        cvo_ref[2] = x
        qk = _silu(xa)
        q = qk[:, 0:BW]
        k = qk[:, BW:2 * BW] * (DH ** -0.5)
        ga = p_ref[:, 5 * BW:5 * BW + 128]
        ip = _dot_sel(ga + bi_ref[...], _expand_sel(0))
        lf = _dot_sel(-_softplus(-(ga + bf_ref[...])), _expand_sel(NH))
        mm = _dot_sel(m_ref[...], _expand_sel(0))
        inter = lf + mm
        m_t = jnp.maximum(inter, ip)
        w_int = jnp.exp(inter - m_t)
        wgt = jnp.exp(ip - m_t)
        mo_ref[...] = m_t
        qwt_s[...] = _transpose_rows(q * w_int)
        qt_s[...] = _transpose_rows(q)
        wit_s[...] = _transpose_rows(w_int)
        wkt_s[...] = _transpose_rows(wgt * k)
        vt_s[...] = _transpose_rows(p_ref[:, 2 * BW:3 * BW])
        st_s[...] = _transpose_rows(_dot_sel(q * k, _head_ones()) * wgt)
        flt_s[...] = _transpose_rows(jnp.exp(-m_t))
        nt_s[...] = _transpose_rows(n_ref[...])

    r0 = pl.multiple_of(h * DH, DH)
    v_h = _head_rows(vt_s, h)
    w_row = wit_s[pl.ds(r0, 1), :]

    def one_d(d, acc):
        c = c_ref[d]
        co_ref[d] = c * w_row + wkt_s[pl.ds(r0 + d, 1), :] * v_h
        return acc + c * qwt_s[pl.ds(r0 + d, 1), :]

    nbl = c_ref.shape[-1]
    num = lax.fori_loop(0, DH, one_d, jnp.zeros((DH, nbl), F32), unroll=8)
    s_row = st_s[pl.ds(r0, 1), :]
    n_h = _head_rows(nt_s, h)
    qn = jnp.sum(_head_rows(qt_s, h) * n_h, axis=0, keepdims=True)
    den = w_row * qn + s_row
    ht_s[pl.ds(r0, DH), :] = (num + s_row * v_h) / jnp.maximum(jnp.abs(den), flt_s[pl.ds(r0, 1), :])
    nto_s[pl.ds(r0, DH), :] = w_row * n_h + _head_rows(wkt_s, h)

    @pl.when(h == NH - 1)
    def _():
        out_ref[...] = ht_s[...].T * _sigmoid(p_ref[:, 3 * BW:4 * BW]) * _silu(p_ref[:, 4 * BW:5 * BW])
        no_ref[...] = nto_s[...].T


def _ml_step(p, c_all, c_acc, l, n0, m0, c0, conv_w, b_i, b_f):
    nb = p.shape[0]
    ins = (p, c_all, n0, m0, c0, conv_w, b_i, b_f)
    in_specs = [_full(a.shape) for a in ins]
    in_specs[1] = _lanes_block(nb, l)
    return pl.pallas_call(
        lambda *refs: _ml_step_body(*refs[:8], *refs[9:]),
        grid=(NH,),
        in_specs=in_specs + [pl.BlockSpec(memory_space=pl.ANY)],
        out_specs=[_full((nb, BW)), _lanes_block(nb, l), _full((nb, BW)), _full((nb, BW)),
                   _full((CONV_W - 1, nb, 2 * BW))],
        out_shape=[
            jax.ShapeDtypeStruct((nb, BW), F32),
            jax.ShapeDtypeStruct(c_all.shape, F32),
            jax.ShapeDtypeStruct((nb, BW), F32),
            jax.ShapeDtypeStruct((nb, BW), F32),
            jax.ShapeDtypeStruct((CONV_W - 1, nb, 2 * BW), F32),
        ],
        input_output_aliases={8: 1},
        scratch_shapes=[pltpu.VMEM((BW, nb), F32) for _ in range(10)],
        compiler_params=_cparams(("arbitrary",)),
        name="mlstm_step",
    )(*ins, c_acc)


def _pad_lanes(x, width):
    return jnp.pad(x, [(0, 0)] * (x.ndim - 1) + [(0, width - x.shape[-1])])


IN_SEGMENTS = (('lru_x', BW), ('lru_z', BW), ('gdn_q', BW), ('gdn_k', BW), ('gdn_v', BW), ('gdn_z', BW),
               ('gdn_a', NH), ('gdn_b', NH), ('s5_u', BW), ('s5_z', BW),
               ('ml_q', BW), ('ml_k', BW), ('ml_v', BW), ('ml_o', BW), ('ml_z', BW),
               ('ml_i', NH), ('ml_f', NH), ('merge', NH * D_MODEL))
IN_WIDTH = sum(wd for _, wd in IN_SEGMENTS)
PACK_ORDER = (('merge',), ('lru_x', 'lru_z'), ('s5_u', 's5_z'), ('gdn_q', 'gdn_k', 'gdn_v', 'gdn_z'), ('gdn_a', 'gdn_b'),
              ('ml_q', 'ml_k', 'ml_v', 'ml_o', 'ml_z'), ('ml_i', 'ml_f'))
PACK_ROWS = 256


def _pack_body(wt_ref, o_ref):
    rows = o_ref.shape[0]
    eye = (_iota((rows, rows), 0) == _iota((rows, rows), 1)).astype(BF16)
    src = {}
    off = 0
    for name, wd in IN_SEGMENTS:
        src[name] = (off, wd)
        off += wd
    o_ref[:, W_PACK + sum(W_MIX):] = jnp.zeros((rows, W_PACK - sum(W_MIX)), BF16)
    dst = 0
    for group in PACK_ORDER:
        a = src[group[0]][0]
        wd = sum(src[n][1] for n in group)
        if wd % 128:
            o_ref[:, dst:dst + 128] = jnp.zeros((rows, 128), BF16)
        seg = wt_ref[a:a + wd, :].astype(BF16)
        o_ref[:, dst:dst + wd] = lax.dot_general(eye, seg, _NT, preferred_element_type=F32).astype(BF16)
        dst += -(-wd // 128) * 128


def _pack_w_in(w_in):
    depth = w_in.shape[0]
    return pl.pallas_call(
        _pack_body,
        grid=(depth, D_MODEL // PACK_ROWS),
        in_specs=[pl.BlockSpec((None, IN_WIDTH, PACK_ROWS), lambda l, i: (l, 0, i))],
        out_specs=pl.BlockSpec((None, PACK_ROWS, 2 * W_PACK), lambda l, i: (l, i, 0)),
        out_shape=jax.ShapeDtypeStruct((depth, D_MODEL, 2 * W_PACK), BF16),
        compiler_params=_cparams(("arbitrary", "arbitrary")),
        name="pack_w_in",
    )(jnp.swapaxes(w_in, 1, 2))


def _prep_layer(l, prm):
    eye4 = jnp.eye(4, dtype=F32)
    blockdiag = lambda w: jnp.einsum('ncd,nm->ncmd', w, eye4).reshape(BW, BW)
    lru_wg = jnp.concatenate([blockdiag(prm['lru_w_a'][l]), blockdiag(prm['lru_w_x'][l])], axis=1).astype(BF16)
    lru_bg = jnp.concatenate([prm['lru_b_a'][l], prm['lru_b_x'][l]])[None, :]
    lru_sp = jax.nn.softplus(-prm['lru_lambda'][l])[None, :]

    lam_re, lam_im = prm['s5_lam_re'][l], prm['s5_lam_im'][l]
    dt = jnp.exp(prm['s5_log_dt'][l])[:, None]
    mag = jnp.exp(lam_re * dt)
    ang = lam_im * dt
    ar, ai = mag * jnp.cos(ang), mag * jnp.sin(ang)
    den = lam_re * lam_re + lam_im * lam_im
    cr = ((ar - 1.0) * lam_re + ai * lam_im) / den
    ci = (ai * lam_re - (ar - 1.0) * lam_im) / den
    b_re, b_im = prm['s5_b_re'][l], prm['s5_b_im'][l]
    bb_re = cr[..., None] * b_re - ci[..., None] * b_im
    bb_im = cr[..., None] * b_im + ci[..., None] * b_re
    eye_g = jnp.eye(S5_G, dtype=F32)
    in_dense = lambda bb: jnp.einsum('gnc,gh->gchn', bb, eye_g).reshape(BW, S5_W).astype(BF16)
    out_dense = lambda cc: jnp.einsum('gcn,gh->gnhc', cc, eye_g).reshape(S5_W, BW).astype(BF16)

    gate_row = lambda a, b: _pad_lanes(jnp.concatenate([a, b])[None, :], 128)
    zeros4 = jnp.zeros((NH,), F32)
    return dict(
        norm_w=prm['norm_w'][l][None, :],
        w_branch=prm['w_branch'][l].astype(BF16),
        w_out=prm['w_out'][l].astype(BF16),
        lru_cw=prm['lru_conv_w'][l], lru_wg=lru_wg, lru_bg=lru_bg, lru_sp=lru_sp,
        s5_ar=ar.reshape(1, S5_W), s5_ai=ai.reshape(1, S5_W),
        s5_bre=in_dense(bb_re), s5_bim=in_dense(bb_im),
        s5_cre=out_dense(prm['s5_c_re'][l]), s5_cim=out_dense(prm['s5_c_im'][l]),
        s5_d=prm['s5_d'][l][None, :], s5_wglu=prm['s5_w_glu'][l].astype(BF16),
        gdn_cw=prm['gdn_conv_w'][l],
        gdn_nega=gate_row(-jnp.exp(prm['gdn_A_log'][l]), zeros4),
        gdn_dtb=gate_row(prm['gdn_dt_bias'][l], zeros4),
        gdn_nw=jnp.tile(prm['gdn_norm_w'][l], NH)[None, :],
        ml_cw=prm['ml_conv_w'][l],
        ml_bi=gate_row(prm['ml_b_i'][l], zeros4),
        ml_bf=gate_row(zeros4, prm['ml_b_f'][l]),
    )


def _prompt_layer(x, pp, w_pack, l, final_w, final, tm, tt, tg):
    b, t, _ = x.shape
    p_lru, p_s5, p_gdn, p_ml = _in_proj(x, pp['norm_w'], w_pack, l, tm, True)
    z = lambda *s: jnp.zeros(s, F32)
    out_a, lru_h, lru_c = _lru(p_lru, z(b, BW), z(CONV_W - 1, b, BW),
                               pp['lru_cw'], pp['lru_wg'], pp['lru_bg'], pp['lru_sp'], tt)
    out_c, s5_re, s5_im = _s5(p_s5, z(b, S5_W), z(b, S5_W), pp['s5_ar'], pp['s5_ai'],
                              pp['s5_bre'], pp['s5_bim'], pp['s5_cre'], pp['s5_cim'], pp['s5_d'], pp['s5_wglu'], tt)
    out_b, gdn_s, gdn_c, out_d, ml_c, ml_n, ml_m, ml_cv = _chunked(
        p_gdn, z(b, NH, DH, DH), z(b, CONV_W - 1, 3 * BW), pp['gdn_cw'], pp['gdn_nega'], pp['gdn_dtb'], pp['gdn_nw'],
        p_ml, z(b, NH, DH, DH), z(b, 1, BW), z(b, 1, BW), z(b, CONV_W - 1, 2 * BW), pp['ml_cw'], pp['ml_bi'], pp['ml_bf'],
        b, tg)
    y = _merge(x, (out_a, out_b, out_c, out_d), pp['norm_w'], w_pack, l,
               pp['w_branch'], pp['w_out'], final_w, tm, final)
    st = dict(lru_h=lru_h, lru_conv=jnp.transpose(lru_c, (1, 0, 2)), gdn_S=gdn_s, gdn_conv=gdn_c,
              s5_re=s5_re.reshape(b, S5_G, S5_N), s5_im=s5_im.reshape(b, S5_G, S5_N),
              ml_C=ml_c, ml_n=ml_n.reshape(b, NH, DH), ml_m=ml_m.reshape(b, NH, DH)[:, :, 0], ml_conv=ml_cv)
    return y, st


BIG_STATES = ('gdn_S', 'ml_C')
TO_LANES = (0, 2, 3, 4, 1)
FROM_LANES = (0, 4, 1, 2, 3)


def _sample_layer(x, st, big_in, big_acc, l, pp, w_pack, final_w, final):
    nb = x.shape[1]
    p_lru, p_s5, p_gdn, p_ml = _in_proj(x, pp['norm_w'], w_pack, l, nb, False)
    tmaj = lambda c: jnp.transpose(c, (1, 0, 2))
    out_a, lru_h, lru_c = _lru(p_lru.reshape(1, nb, W_LRU), st['lru_h'], tmaj(st['lru_conv']),
                               pp['lru_cw'], pp['lru_wg'], pp['lru_bg'], pp['lru_sp'], 1)
    out_c, s5_re, s5_im = _s5(p_s5.reshape(1, nb, W_S5), st['s5_re'].reshape(nb, S5_W), st['s5_im'].reshape(nb, S5_W),
                              pp['s5_ar'], pp['s5_ai'], pp['s5_bre'], pp['s5_bim'], pp['s5_cre'], pp['s5_cim'],
                              pp['s5_d'], pp['s5_wglu'], 1)
    out_b, gdn_s, gdn_c = _gdn_step(p_gdn, big_in['gdn_S'], big_acc['gdn_S'], l, tmaj(st['gdn_conv']), pp['gdn_cw'],
                                    pp['gdn_nega'], pp['gdn_dtb'], pp['gdn_nw'])
    out_d, ml_c, ml_n, ml_m, ml_cv = _ml_step(p_ml, big_in['ml_C'], big_acc['ml_C'], l,
                                              st['ml_n'].reshape(nb, BW), _pad_lanes(st['ml_m'], 128),
                                              tmaj(st['ml_conv']), pp['ml_cw'], pp['ml_bi'], pp['ml_bf'])
    y = _merge(x, (out_a.reshape(nb, BW), out_b, out_c.reshape(nb, BW), out_d),
               pp['norm_w'], w_pack, l, pp['w_branch'], pp['w_out'], final_w, nb, final)
    new = dict(lru_h=lru_h, lru_conv=tmaj(lru_c), gdn_conv=tmaj(gdn_c),
               s5_re=s5_re.reshape(nb, S5_G, S5_N), s5_im=s5_im.reshape(nb, S5_G, S5_N),
               ml_n=ml_n.reshape(nb, NH, DH), ml_m=ml_m.reshape(nb, NH, DH)[:, :, 0], ml_conv=tmaj(ml_cv))
    return y, new, dict(gdn_S=gdn_s, ml_C=ml_c)


STATE_NAMES = ('lru_h', 'lru_conv', 'gdn_S', 'gdn_conv', 's5_re', 's5_im', 'ml_C', 'ml_n', 'ml_m', 'ml_conv')

TM_ROWS = 64
TT_SCAN = 128


def kernel(x_prompt, x_sample, state_lru_h, state_lru_conv, state_gdn_S, state_gdn_conv, state_s5_re, state_s5_im, state_ml_C, state_ml_n, state_ml_m, state_ml_conv, norm_w, w_in, lru_conv_w, lru_w_a, lru_b_a, lru_w_x, lru_b_x, lru_lambda, gdn_conv_w, gdn_A_log, gdn_dt_bias, gdn_norm_w, s5_lam_re, s5_lam_im, s5_log_dt, s5_b_re, s5_b_im, s5_c_re, s5_c_im, s5_d, s5_w_glu, ml_conv_w, ml_b_i, ml_b_f, w_branch, w_out, final_norm_w):
    prm = dict(norm_w=norm_w, w_in=w_in, lru_conv_w=lru_conv_w, lru_w_a=lru_w_a, lru_b_a=lru_b_a, lru_w_x=lru_w_x,
               lru_b_x=lru_b_x, lru_lambda=lru_lambda, gdn_conv_w=gdn_conv_w, gdn_A_log=gdn_A_log,
               gdn_dt_bias=gdn_dt_bias, gdn_norm_w=gdn_norm_w, s5_lam_re=s5_lam_re, s5_lam_im=s5_lam_im,
               s5_log_dt=s5_log_dt, s5_b_re=s5_b_re, s5_b_im=s5_b_im, s5_c_re=s5_c_re, s5_c_im=s5_c_im, s5_d=s5_d,
               s5_w_glu=s5_w_glu, ml_conv_w=ml_conv_w, ml_b_i=ml_b_i, ml_b_f=ml_b_f, w_branch=w_branch, w_out=w_out)
    depth = w_in.shape[0]
    t = x_prompt.shape[1]
    nb = x_sample.shape[0]
    final_w = final_norm_w[None, :]
    st_in = dict(lru_h=state_lru_h, lru_conv=state_lru_conv, gdn_S=state_gdn_S, gdn_conv=state_gdn_conv,
                 s5_re=state_s5_re, s5_im=state_s5_im, ml_C=state_ml_C, ml_n=state_ml_n, ml_m=state_ml_m,
                 ml_conv=state_ml_conv)
    tm = min(TM_ROWS, t)
    tt = min(TT_SCAN, t)
    tg = min(SE, t)

    xp = x_prompt
    xs = x_sample.reshape(1, nb, D_MODEL)
    small = [n for n in STATE_NAMES if n not in BIG_STATES]
    new_p = {n: [] for n in STATE_NAMES}
    new_s = {n: [] for n in small}
    big_in = {n: jnp.transpose(st_in[n], TO_LANES) for n in BIG_STATES}
    big_acc = {n: jnp.zeros(big_in[n].shape, F32) for n in BIG_STATES}
    w_pack = _pack_w_in(w_in)
    for l in range(depth):
        pp = _prep_layer(l, prm)
        final = l == depth - 1
        xp, sp = _prompt_layer(xp, pp, w_pack, l, final_w, final, tm, tt, tg)
        xs, ss, big_acc = _sample_layer(xs, {n: st_in[n][l] for n in small}, big_in, big_acc, l, pp, w_pack,
                                        final_w, final)
        for n in STATE_NAMES:
            new_p[n].append(sp[n])
        for n in small:
            new_s[n].append(ss[n])
    np_ = {n: jnp.stack(v) for n, v in new_p.items()}
    ns_ = {n: jnp.stack(v) for n, v in new_s.items()}
    ns_.update({n: jnp.transpose(big_acc[n], FROM_LANES) for n in BIG_STATES})
    y_sample = xs.reshape(nb, 1, D_MODEL)
    return (xp, y_sample,
            np_['lru_h'], ns_['lru_h'], np_['lru_conv'], ns_['lru_conv'],
            np_['gdn_S'], ns_['gdn_S'], np_['gdn_conv'], ns_['gdn_conv'],
            np_['s5_re'], ns_['s5_re'], np_['s5_im'], ns_['s5_im'],
            np_['ml_C'], ns_['ml_C'], np_['ml_n'], ns_['ml_n'], np_['ml_m'], ns_['ml_m'],
            np_['ml_conv'], ns_['ml_conv'])
```

```python
import functools
import math

import jax
import jax.numpy as jnp
from jax import lax
from jax.experimental import pallas as pl
from jax.experimental.pallas import tpu as pltpu

F32 = jnp.float32
BF16 = jnp.bfloat16

D_MODEL = 1024
BW = 256
NH = 4
DH = 64
CONV_W = 4
CHUNK = 64
LRU_C = 8.0
S5_G = 16
S5_N = 64
S5_W = S5_G * S5_N
EPS = 1e-6

W_LRU = 2 * BW
W_S5 = 2 * BW
W_GDN = 4 * BW + 128
W_ML = 5 * BW + 128
W_MIX = (W_LRU, W_S5, W_GDN, W_ML)

VMEM_LIMIT_BYTES = 56 * 1024 * 1024


def _cparams(sem):
    return pltpu.CompilerParams(dimension_semantics=sem, vmem_limit_bytes=VMEM_LIMIT_BYTES)


def _sigmoid(x):
    return 1.0 / (1.0 + jnp.exp(-x))


def _silu(x):
    return x * _sigmoid(x)


def _softplus(x):
    return jnp.maximum(x, 0.0) + jnp.log(1.0 + jnp.exp(-jnp.abs(x)))


def _gelu_tanh(x):
    c = math.sqrt(2.0 / math.pi)
    return x * (0.5 * (1.0 + jnp.tanh(c * (x + 0.044715 * (x * x * x)))))


def _split3(x):
    hi = x.astype(BF16)
    r = x - hi.astype(F32)
    mid = r.astype(BF16)
    lo = (r - mid.astype(F32)).astype(BF16)
    return hi, mid, lo


def _split2(x):
    hi = x.astype(BF16)
    return hi, (x - hi.astype(F32)).astype(BF16)


_NN = (((1,), (0,)), ((), ()))
_NT = (((1,), (1,)), ((), ()))
_TN = (((0,), (0,)), ((), ()))


def _sel_dot(sel, x, dims=_NN, split=_split3):
    out = None
    for p in split(x):
        t = lax.dot_general(sel, p, dims, preferred_element_type=F32)
        out = t if out is None else out + t
    return out


def _dot_sel(x, sel, dims=_NN, split=_split3):
    out = None
    for p in split(x):
        t = lax.dot_general(p, sel, dims, preferred_element_type=F32)
        out = t if out is None else out + t
    return out


def _iota(shape, axis):
    return lax.broadcasted_iota(jnp.int32, shape, axis)


def _head_ones():
    return (_iota((BW, BW), 0) // DH == _iota((BW, BW), 1) // DH).astype(BF16)


def _expand_sel(offset):
    return (_iota((128, BW), 0) == _iota((128, BW), 1) // DH + offset).astype(BF16)


def _rms_rows(x, w_row):
    ms = jnp.mean(x * x, axis=-1, keepdims=True)
    return (x * lax.rsqrt(ms + EPS)) * w_row


def _inproj_body(x_ref, nw_ref, w_ref, *o_refs):
    nb, tm, _ = x_ref.shape
    xn = _rms_rows(x_ref[...].reshape(nb * tm, D_MODEL), nw_ref[...]).astype(BF16)
    off = 0
    for o_ref in o_refs:
        wd = o_ref.shape[-1] if len(o_ref.shape) == 3 else o_ref.shape[-1] // nb
        y = jnp.dot(xn, w_ref[:, off:off + wd], preferred_element_type=F32)
        if len(o_ref.shape) == 3:
            o_ref[...] = jnp.swapaxes(y.reshape(nb, tm, wd), 0, 1)
        else:
            for bi in range(nb):
                o_ref[:, bi * wd:(bi + 1) * wd] = y[bi * tm:(bi + 1) * tm]
        off += wd


W_PACK = NH * D_MODEL


def _in_proj(x, norm_w, w_pack, l, tm, scan_3d):
    b, t, _ = x.shape
    assert sum(W_MIX) <= W_PACK
    specs, shapes = [], []
    for n, wd in enumerate(W_MIX):
        if scan_3d and n < 2:
            specs.append(pl.BlockSpec((tm, b, wd), lambda i: (i, 0, 0)))
            shapes.append(jax.ShapeDtypeStruct((t, b, wd), F32))
        else:
            specs.append(pl.BlockSpec((tm, b * wd), lambda i: (i, 0)))
            shapes.append(jax.ShapeDtypeStruct((t, b * wd), F32))
    return pl.pallas_call(
        _inproj_body,
        grid=(t // tm,),
        in_specs=[
            pl.BlockSpec((b, tm, D_MODEL), lambda i: (0, i, 0)),
            pl.BlockSpec((1, D_MODEL), lambda i: (0, 0)),
            pl.BlockSpec((None, D_MODEL, W_PACK), lambda i: (l, 0, 1)),
        ],
        out_specs=specs,
        out_shape=shapes,
        compiler_params=_cparams(("arbitrary",)),
        name="in_proj",
    )(x, norm_w, w_pack)


def _merge_body(x_ref, a_ref, b_ref, c_ref, d_ref, nw_ref, wm_ref, wb_ref, wo_ref, fw_ref, y_ref, *, final):
    nb, tm, _ = x_ref.shape
    x = x_ref[...].reshape(nb * tm, D_MODEL)
    xn = _rms_rows(x, nw_ref[...]).astype(BF16)
    merged = None
    for n, r in enumerate((a_ref, b_ref, c_ref, d_ref)):
        if len(r.shape) == 3:
            br = jnp.swapaxes(r[...], 0, 1).reshape(nb * tm, BW)
        else:
            br = jnp.concatenate([r[:, bi * BW:(bi + 1) * BW] for bi in range(nb)], axis=0)
        gate = _sigmoid(jnp.dot(xn, wm_ref[:, n * D_MODEL:(n + 1) * D_MODEL], preferred_element_type=F32))
        proj = jnp.dot(br.astype(BF16), wb_ref[n], preferred_element_type=F32)
        merged = gate * proj if merged is None else merged + gate * proj
    y = x + jnp.dot(merged.astype(BF16), wo_ref[...], preferred_element_type=F32)
    if final:
        y = _rms_rows(y, fw_ref[...])
    y_ref[...] = y.reshape(nb, tm, D_MODEL)


def _merge(x, branches, norm_w, w_pack, l, w_branch, w_out, final_w, tm, final):
    b, t, _ = x.shape
    const2 = lambda i: (0, 0)
    br_specs = [pl.BlockSpec((tm,) + r.shape[1:], (lambda i: (i, 0, 0)) if r.ndim == 3 else (lambda i: (i, 0)))
                for r in branches]
    return pl.pallas_call(
        functools.partial(_merge_body, final=final),
        grid=(t // tm,),
        in_specs=[
            pl.BlockSpec((b, tm, D_MODEL), lambda i: (0, i, 0)),
            *br_specs,
            pl.BlockSpec((1, D_MODEL), const2),
            pl.BlockSpec((None, D_MODEL, W_PACK), lambda i: (l, 0, 0)),
            pl.BlockSpec((NH, BW, D_MODEL), lambda i: (0, 0, 0)),
            pl.BlockSpec((D_MODEL, D_MODEL), const2),
            pl.BlockSpec((1, D_MODEL), const2),
        ],
        out_specs=pl.BlockSpec((b, tm, D_MODEL), lambda i: (0, i, 0)),
        out_shape=jax.ShapeDtypeStruct((b, t, D_MODEL), F32),
        compiler_params=_cparams(("arbitrary",)),
        name="merge",
    )(x, *branches, norm_w, w_pack, w_branch, w_out, final_w)


def _lru_body(p_ref, h0_ref, c0_ref, cw_ref, wg_ref, bg_ref, sp_ref, out_ref, ht_ref, ct_ref,
              xp_s, a_s, b_s, h_s):
    tt, nb = p_ref.shape[0], p_ref.shape[1]

    @pl.when(pl.program_id(0) == 0)
    def _():
        h_s[...] = h0_ref[...]
        xp_s[0:CONV_W - 1] = c0_ref[...]

    xp_s[CONV_W - 1:CONV_W - 1 + tt] = p_ref[:, :, 0:BW]
    xa = None
    for j in range(CONV_W):
        term = xp_s[j:j + tt] * cw_ref[j:j + 1, :].reshape(1, 1, BW)
        xa = term if xa is None else xa + term
    tail = xp_s[tt:tt + CONV_W - 1]
    xp_s[0:CONV_W - 1] = tail
    ct_ref[...] = tail

    xa2 = xa.reshape(tt * nb, BW)
    gates = jnp.dot(xa2.astype(BF16), wg_ref[...], preferred_element_type=F32) + bg_ref[...]
    r = _sigmoid(gates[:, 0:BW])
    ig = _sigmoid(gates[:, BW:2 * BW])
    log_a = (-LRU_C * r) * sp_ref[...]
    a = jnp.exp(log_a)
    bb = jnp.sqrt(1.0 - jnp.exp(2.0 * log_a)) * (ig * xa2)
    a_s[...] = a.reshape(tt, nb, BW)
    b_s[...] = bb.reshape(tt, nb, BW)

    def step(t, h):
        h = a_s[t] * h + b_s[t]
        b_s[t] = h
        return h

    h = lax.fori_loop(0, tt, step, h_s[...], unroll=min(8, tt))
    h_s[...] = h
    ht_ref[...] = h
    out_ref[...] = b_s[...] * _silu(p_ref[:, :, BW:2 * BW])


def _lru(p, h0, c0, conv_w, w_gates, b_gates, sp_lam, tt):
    t, nb, _ = p.shape
    c2 = lambda i: (0, 0)
    c3 = lambda i: (0, 0, 0)
    return pl.pallas_call(
        _lru_body,
        grid=(t // tt,),
        in_specs=[
            pl.BlockSpec((tt, nb, W_LRU), lambda i: (i, 0, 0)),
            pl.BlockSpec((nb, BW), c2),
            pl.BlockSpec((CONV_W - 1, nb, BW), c3),
            pl.BlockSpec((CONV_W, BW), c2),
            pl.BlockSpec((BW, 2 * BW), c2),
            pl.BlockSpec((1, 2 * BW), c2),
            pl.BlockSpec((1, BW), c2),
        ],
        out_specs=[
            pl.BlockSpec((tt, nb, BW), lambda i: (i, 0, 0)),
            pl.BlockSpec((nb, BW), c2),
            pl.BlockSpec((CONV_W - 1, nb, BW), c3),
        ],
        out_shape=[
            jax.ShapeDtypeStruct((t, nb, BW), F32),
            jax.ShapeDtypeStruct((nb, BW), F32),
            jax.ShapeDtypeStruct((CONV_W - 1, nb, BW), F32),
        ],
        scratch_shapes=[
            pltpu.VMEM((tt + CONV_W - 1, nb, BW), F32),
            pltpu.VMEM((tt, nb, BW), F32),
            pltpu.VMEM((tt, nb, BW), F32),
            pltpu.VMEM((nb, BW), F32),
        ],
        compiler_params=_cparams(("arbitrary",)),
        name="rglru",
    )(p, h0, c0, conv_w, w_gates, b_gates, sp_lam)


def _s5_body(p_ref, hr0_ref, hi0_ref, ar_ref, ai_ref, bre_ref, bim_ref, cre_ref, cim_ref, d_ref, wglu_ref,
             out_ref, hrt_ref, hit_ref, xr_s, xi_s, hr_s, hi_s):
    tt, nb = p_ref.shape[0], p_ref.shape[1]

    @pl.when(pl.program_id(0) == 0)
    def _():
        hr_s[...] = hr0_ref[...]
        hi_s[...] = hi0_ref[...]

    u2 = p_ref[:, :, 0:BW].reshape(tt * nb, BW)
    ub = u2.astype(BF16)
    xr_s[...] = jnp.dot(ub, bre_ref[...], preferred_element_type=F32).reshape(tt, nb, S5_W)
    xi_s[...] = jnp.dot(ub, bim_ref[...], preferred_element_type=F32).reshape(tt, nb, S5_W)
    ar = jnp.broadcast_to(ar_ref[...], (nb, S5_W))
    ai = jnp.broadcast_to(ai_ref[...], (nb, S5_W))

    def step(t, c):
        hr, hi = c
        nr = ar * hr - ai * hi + xr_s[t]
        ni = ar * hi + ai * hr + xi_s[t]
        xr_s[t] = nr
        xi_s[t] = ni
        return nr, ni

    hr, hi = lax.fori_loop(0, tt, step, (hr_s[...], hi_s[...]), unroll=min(4, tt))
    hr_s[...] = hr
    hi_s[...] = hi
    hrt_ref[...] = hr
    hit_ref[...] = hi

    y = (jnp.dot(xr_s[...].reshape(tt * nb, S5_W).astype(BF16), cre_ref[...], preferred_element_type=F32)
         - jnp.dot(xi_s[...].reshape(tt * nb, S5_W).astype(BF16), cim_ref[...], preferred_element_type=F32)
         + d_ref[...] * u2)
    y = _gelu_tanh(y)
    y = y * _sigmoid(jnp.dot(y.astype(BF16), wglu_ref[...], preferred_element_type=F32))
    out_ref[...] = y.reshape(tt, nb, BW) * _silu(p_ref[:, :, BW:2 * BW])


def _s5(p, hr0, hi0, ar, ai, bre, bim, cre, cim, d, wglu, tt):
    t, nb, _ = p.shape
    c2 = lambda i: (0, 0)
    st = pl.BlockSpec((nb, S5_W), c2)
    row = pl.BlockSpec((1, S5_W), c2)
    return pl.pallas_call(
        _s5_body,
        grid=(t // tt,),
        in_specs=[
            pl.BlockSpec((tt, nb, W_S5), lambda i: (i, 0, 0)),
            st, st, row, row,
            pl.BlockSpec((BW, S5_W), c2), pl.BlockSpec((BW, S5_W), c2),
            pl.BlockSpec((S5_W, BW), c2), pl.BlockSpec((S5_W, BW), c2),
            pl.BlockSpec((1, BW), c2),
            pl.BlockSpec((BW, BW), c2),
        ],
        out_specs=[pl.BlockSpec((tt, nb, BW), lambda i: (i, 0, 0)), st, st],
        out_shape=[
            jax.ShapeDtypeStruct((t, nb, BW), F32),
            jax.ShapeDtypeStruct((nb, S5_W), F32),
            jax.ShapeDtypeStruct((nb, S5_W), F32),
        ],
        scratch_shapes=[
            pltpu.VMEM((tt, nb, S5_W), F32),
            pltpu.VMEM((tt, nb, S5_W), F32),
            pltpu.VMEM((nb, S5_W), F32),
            pltpu.VMEM((nb, S5_W), F32),
        ],
        compiler_params=_cparams(("arbitrary",)),
        name="s5",
    )(p, hr0, hi0, ar, ai, bre, bim, cre, cim, d, wglu)


def _conv_rows(xp_s, p_cols, cw_ref, tg):
    xp_s[8:8 + tg] = p_cols
    y = None
    for j in range(CONV_W):
        term = xp_s[8 - (CONV_W - 1) + j:8 - (CONV_W - 1) + j + tg] * cw_ref[j:j + 1, :]
        y = term if y is None else y + term
    tail = xp_s[8 + tg - (CONV_W - 1):8 + tg]
    xp_s[8 - (CONV_W - 1):8] = tail
    return y, tail


SE = NH * CHUNK


def _se_masks():
    r = _iota((SE, SE), 0)
    c = _iota((SE, SE), 1)
    same = (r // CHUNK) == (c // CHUNK)
    return same, same & ((r % CHUNK) >= (c % CHUNK))


def _cols_masks(n):
    i = _iota((n, BW), 0) % CHUNK
    j = _iota((n, BW), 1) % CHUNK
    return i >= j, i > j, i == j


def _stack_se(x_bf, same):
    return jnp.where(same, jnp.concatenate([x_bf] * NH, axis=0), jnp.zeros((), x_bf.dtype))


def _tile_rows(x):
    return jnp.concatenate([x] * NH, axis=0)


def _fold_se(x_se):
    out = x_se[0:CHUNK]
    for h in range(1, NH):
        out = out + x_se[h * CHUNK:(h + 1) * CHUNK]
    return out


def _load_cols(s_s, blocks_ref):
    for h in range(NH):
        s_s[:, h * DH:(h + 1) * DH] = blocks_ref[h]


def _store_cols(blocks_ref, s_s):
    for h in range(NH):
        blocks_ref[h] = s_s[:, h * DH:(h + 1) * DH]


def _last_rows(x, nc):
    return jnp.concatenate([jnp.broadcast_to(x[c * CHUNK + CHUNK - 1:(c + 1) * CHUNK], (CHUNK, BW)) for c in range(nc)], axis=0)


def _gdn_init(s0_ref, c0_ref, xp_s, s_s):
    _load_cols(s_s, s0_ref)
    xp_s[8 - (CONV_W - 1):8] = c0_ref[...]


def _gdn_main(p_ref, cw_ref, nega_ref, dtb_ref, nw_ref, out_ref, st_ref, ct_ref, xp_s, o_s, s_s):
    tg = p_ref.shape[0]
    nc = tg // CHUNK
    y, tail = _conv_rows(xp_s, p_ref[:, 0:3 * BW], cw_ref, tg)
    ct_ref[...] = tail
    qkv = _silu(y)
    ones_h = _head_ones()
    q = qkv[:, 0:BW]
    k = qkv[:, BW:2 * BW]
    q = q * lax.rsqrt(_dot_sel(q * q, ones_h, split=_split2) + EPS) * (DH ** -0.5)
    k = k * lax.rsqrt(_dot_sel(k * k, ones_h, split=_split2) + EPS)
    v = qkv[:, 2 * BW:3 * BW]
    yield
    ga = p_ref[:, 4 * BW:4 * BW + 128]
    lane = _iota((tg, 128), 1)
    g = nega_ref[...] * _softplus(ga + dtb_ref[...])
    gall = jnp.where(lane < NH, g, _sigmoid(ga))

    same, tri = _se_masks()
    tri_c, strict_c, _ = _cols_masks(tg)
    tri_bf = tri.astype(BF16)
    beta_x = _dot_sel(gall, _expand_sel(NH), split=_split2)
    g_x = _dot_sel(gall, _expand_sel(0), split=_split2)
    gc_x = _sel_dot(tri_bf, g_x, split=_split2)
    gl_x = _last_rows(gc_x, nc)
    egc = jnp.exp(gc_x)
    kb = k * beta_x
    k_b = k.astype(BF16)
    kb_b = kb.astype(BF16)
    q_b = q.astype(BF16)
    vb_b = (v * beta_x).astype(BF16)
    kbg_b = (kb * egc).astype(BF16)
    qd_b = (q * egc).astype(BF16)
    kd_b = (k * jnp.exp(gl_x - gc_x)).astype(BF16)
    yield

    cs = range(nc)
    rows = [slice(c * CHUNK, (c + 1) * CHUNK) for c in cs]
    diff = _sel_dot(tri_bf, jnp.where(strict_c, g_x, 0.0), split=_split2)
    dec_all = jnp.where(tri_c, jnp.exp(diff), 0.0)
    dec_t = [_tile_rows(dec_all[rows[c]]) for c in cs]
    ndec_all = jnp.where(strict_c, -dec_all, 0.0)
    ndec_t = [_tile_rows(ndec_all[rows[c]]) for c in cs]

    dot = functools.partial(jnp.dot, preferred_element_type=F32)
    cat = functools.partial(jnp.concatenate, axis=0)
    k_se = [_stack_se(k_b[rows[c]], same) for c in cs]
    kq = [lax.dot_general(cat([_stack_se(kb_b[rows[c]], same), _stack_se(q_b[rows[c]], same)]), k_se[c], _NT,
                          preferred_element_type=F32) for c in cs]
    attn = [(kq[c][SE:] * dec_t[c]).astype(BF16) for c in cs]
    eye = (_iota((SE, SE), 0) == _iota((SE, SE), 1)).astype(F32)
    npow = [kq[c][:SE] * ndec_t[c] for c in cs]
    tinv = [eye + npow[c] for c in cs]
    nb = [n.astype(BF16) for n in npow]
    yield
    nb = [dot(n, n).astype(BF16) for n in nb]
    for _ in range(4):
        yield
        both = [dot(cat([t.astype(BF16), n]), n) for t, n in zip(tinv, nb)]
        tinv = [t + b2[:SE] for t, b2 in zip(tinv, both)]
        nb = [b2[SE:].astype(BF16) for b2 in both]
    yield
    tinv = [t + dot(t.astype(BF16), n) for t, n in zip(tinv, nb)]
    t_bf = [t.astype(BF16) for t in tinv]
    u = [dot(t_bf[c], _stack_se(vb_b[rows[c]], same)) for c in cs]
    w = [dot(t_bf[c], _stack_se(kbg_b[rows[c]], same)).astype(BF16) for c in cs]

    kd_t = [_stack_se(kd_b[rows[c]], same).T for c in cs]
    s = s_s[...]
    yield _READY
    for c in cs:
        sb = _stack_se(s.astype(BF16), same)
        ws = dot(cat([w[c], _stack_se(qd_b[rows[c]], same)]), sb)
        vb = (u[c] - ws[:SE]).astype(BF16)
        av = dot(cat([attn[c], kd_t[c]]), vb)
        s = s * jnp.exp(gl_x[c * CHUNK:c * CHUNK + 1]) + _fold_se(av[SE:])
        o_s[rows[c], :] = _fold_se(ws[SE:] + av[:SE])
        yield
    s_s[...] = s
    _store_cols(st_ref, s_s)

    o = o_s[...]
    ms = _dot_sel(o * o, ones_h, split=_split2) * (1.0 / DH)
    out_ref[...] = (o * lax.rsqrt(ms + EPS)) * nw_ref[...] * _silu(p_ref[:, 3 * BW:4 * BW])


def _cummax_rows(x):
    row = _iota(x.shape, 0)
    sh = 1
    while sh < x.shape[0]:
        x = jnp.maximum(x, jnp.where(row >= sh, pltpu.roll(x, sh, 0), -jnp.inf))
        sh *= 2
    return x


def _ml_init(c0m_ref, n0_ref, m0_ref, c0_ref, nt_ref, mt_ref, xp_s, cm_s):
    _load_cols(cm_s, c0m_ref)
    nt_ref[...] = n0_ref[...]
    mt_ref[...] = m0_ref[...]
    xp_s[8 - (CONV_W - 1):8] = c0_ref[...]


def _ml_main(p_ref, cw_ref, bi_ref, bf_ref, out_ref, ct_m_ref, nt_ref, mt_ref, cvt_ref, xp_s, h_s, cm_s):
    tg = p_ref.shape[0]
    nc = tg // CHUNK
    y, tail = _conv_rows(xp_s, p_ref[:, 0:2 * BW], cw_ref, tg)
    cvt_ref[...] = tail
    qk = _silu(y)
    q = qk[:, 0:BW]
    k = qk[:, BW:2 * BW] * (DH ** -0.5)
    yield
    ga = p_ref[:, 5 * BW:5 * BW + 128]
    lane = _iota((tg, 128), 1)
    gall = jnp.where(lane < NH, ga + bi_ref[...], -_softplus(-(ga + bf_ref[...])))

    same, tri = _se_masks()
    tri_c, strict_c, eye_c = _cols_masks(tg)
    tri_bf = tri.astype(BF16)
    ones_h = _head_ones()
    i_x = _dot_sel(gall, _expand_sel(0), split=_split2)
    lf_x = _dot_sel(gall, _expand_sel(NH), split=_split2)
    b_x = _sel_dot(tri_bf, lf_x, split=_split2)
    r_x = i_x - b_x
    e_x = _last_rows(b_x, nc) - b_x + i_x
    q_b = q.astype(BF16)
    k_b = k.astype(BF16)
    v_b = p_ref[:, 2 * BW:3 * BW].astype(BF16)
    yield

    cs = range(nc)
    rows = [slice(c * CHUNK, (c + 1) * CHUNK) for c in cs]
    dl_all = (_sel_dot(tri_bf, jnp.where(strict_c, lf_x, 0.0), split=_split2)
              + _sel_dot(same.astype(BF16), jnp.where(eye_c, i_x, 0.0), split=_split2))
    dl_all = jnp.where(tri_c, dl_all, -jnp.inf)
    dlog = [dl_all[rows[c]] for c in cs]
    yield
    dmax = [b_x[rows[c]] + _cummax_rows(r_x[rows[c]]) for c in cs]
    yield
    q_se = [_stack_se(q_b[rows[c]], same) for c in cs]
    v_se = [_stack_se(v_b[rows[c]], same) for c in cs]
    qkm = [_fold_se(lax.dot_general(q_se[c], _stack_se(k_b[rows[c]], same), _NT, preferred_element_type=F32)) for c in cs]

    cm = cm_s[...]
    nrow = nt_ref[...]
    mrow = mt_ref[...]
    yield _READY
    for c in cs:
        bx = b_x[rows[c]]
        inter = bx + mrow
        m_t = jnp.maximum(inter, dmax[c])
        s = qkm[c] * jnp.exp(dlog[c] - m_t)
        w_int = jnp.exp(inter - m_t)
        num = (w_int * _fold_se(jnp.dot(q_se[c], _stack_se(cm.astype(BF16), same), preferred_element_type=F32))
               + _fold_se(jnp.dot(_stack_se(s.astype(BF16), same), v_se[c], preferred_element_type=F32)))
        den = w_int * _dot_sel(q[rows[c]] * nrow, ones_h, split=_split2) + _dot_sel(s, ones_h, split=_split2)
        h_s[rows[c], :] = num / jnp.maximum(jnp.abs(den), jnp.exp(-m_t))
        blast = bx[CHUNK - 1:CHUNK]
        ex = e_x[rows[c]]
        m_new = jnp.maximum(blast + mrow, jnp.max(ex, axis=0, keepdims=True))
        dec = jnp.exp(blast + mrow - m_new)
        wk = jnp.exp(ex - m_new) * k[rows[c]]
        cm = dec * cm + _fold_se(lax.dot_general(_stack_se(wk.astype(BF16), same), v_se[c], _TN, preferred_element_type=F32))
        nrow = dec * nrow + jnp.sum(wk, axis=0, keepdims=True)
        mrow = m_new
        yield
    cm_s[...] = cm
    _store_cols(ct_m_ref, cm_s)
    nt_ref[...] = nrow
    mt_ref[...] = mrow
    out_ref[...] = h_s[...] * _sigmoid(p_ref[:, 3 * BW:4 * BW]) * _silu(p_ref[:, 4 * BW:5 * BW])


_READY, _DONE = object(), object()


PAIR = 2
PRE_PER_REC = 3


class _Cols:
    def __init__(self, ref, c0, w):
        self.ref, self.c0, self.shape = ref, c0, (ref.shape[0], w)

    def _cols(self, idx):
        if idx is Ellipsis:
            return slice(None), slice(self.c0, self.c0 + self.shape[1])
        r, c = idx
        lo = 0 if c.start is None else c.start
        hi = self.shape[1] if c.stop is None else c.stop
        return r, slice(self.c0 + lo, self.c0 + hi)

    def __getitem__(self, idx):
        return self.ref[self._cols(idx)]

    def __setitem__(self, idx, val):
        self.ref[self._cols(idx)] = val


def _chunked_body(pg_ref, s0_ref, gc0_ref, gcw_ref, nega_ref, dtb_ref, gnw_ref,
                  pm_ref, c0m_ref, n0_ref, m0_ref, mc0_ref, mcw_ref, bi_ref, bf_ref,
                  og_ref, st_ref, gct_ref, om_ref, ctm_ref, nt_ref, mt_ref, mct_ref,
                  gxp_s, go_s, gs_s, mxp_s, mh_s, mcm_s):
    @pl.when(pl.program_id(1) == 0)
    def _():
        for r in range(PAIR):
            _gdn_init(s0_ref.at[r], gc0_ref.at[r], gxp_s.at[r], gs_s.at[r])
            _ml_init(c0m_ref.at[r], n0_ref.at[r], m0_ref.at[r], mc0_ref.at[r], nt_ref.at[r], mt_ref.at[r],
                     mxp_s.at[r], mcm_s.at[r])

    rows = []
    for r in range(PAIR):
        rows.append([
            _gdn_main(_Cols(pg_ref, r * W_GDN, W_GDN), gcw_ref, nega_ref, dtb_ref, gnw_ref, _Cols(og_ref, r * BW, BW),
                      st_ref.at[r], gct_ref.at[r], gxp_s.at[r], go_s.at[r], gs_s.at[r]),
            _ml_main(_Cols(pm_ref, r * W_ML, W_ML), mcw_ref, bi_ref, bf_ref, _Cols(om_ref, r * BW, BW),
                     ctm_ref.at[r], nt_ref.at[r], mt_ref.at[r], mct_ref.at[r], mxp_s.at[r], mh_s.at[r], mcm_s.at[r]),
        ])

    def advance_pre(gens):
        return [g for g in gens if next(g) is not _READY]

    def advance_rec(gens):
        return [g for g in gens if next(g, _DONE) is not _DONE]

    pre = list(rows[0])
    while pre:
        pre = advance_pre(pre)
    for r in range(PAIR):
        rec = list(rows[r])
        pre = list(rows[r + 1]) if r + 1 < PAIR else []
        while rec or pre:
            rec = advance_rec(rec)
            for _ in range(PRE_PER_REC):
                pre = advance_pre(pre)


def _chunked(pg, s0, gc0, gcw, neg_a, dt_bias, gnw, pm, c0m, n0, m0, mc0, mcw, b_i, b_f, b, tg):
    assert tg == SE and b % PAIR == 0
    t = pg.shape[0]
    c2 = lambda bp, i: (0, 0)
    per_b3 = lambda bp, i: (bp, 0, 0)
    per_b4 = lambda bp, i: (bp, 0, 0, 0)
    tile = lambda wd: pl.BlockSpec((tg, PAIR * wd), lambda bp, i: (i, bp))
    st4 = pl.BlockSpec((PAIR, NH, DH, DH), per_b4)
    row = pl.BlockSpec((PAIR, 1, BW), per_b3)
    gconv = pl.BlockSpec((PAIR, CONV_W - 1, 3 * BW), per_b3)
    mconv = pl.BlockSpec((PAIR, CONV_W - 1, 2 * BW), per_b3)
    gate = pl.BlockSpec((1, 128), c2)
    return pl.pallas_call(
        _chunked_body,
        grid=(b // PAIR, t // tg),
        in_specs=[
            tile(W_GDN), st4, gconv, pl.BlockSpec((CONV_W, 3 * BW), c2), gate, gate, pl.BlockSpec((1, BW), c2),
            tile(W_ML), st4, row, row, mconv, pl.BlockSpec((CONV_W, 2 * BW), c2), gate, gate,
        ],
        out_specs=[tile(BW), st4, gconv, tile(BW), st4, row, row, mconv],
        out_shape=[
            jax.ShapeDtypeStruct((t, b * BW), F32),
            jax.ShapeDtypeStruct((b, NH, DH, DH), F32),
            jax.ShapeDtypeStruct((b, CONV_W - 1, 3 * BW), F32),
            jax.ShapeDtypeStruct((t, b * BW), F32),
            jax.ShapeDtypeStruct((b, NH, DH, DH), F32),
            jax.ShapeDtypeStruct((b, 1, BW), F32),
            jax.ShapeDtypeStruct((b, 1, BW), F32),
            jax.ShapeDtypeStruct((b, CONV_W - 1, 2 * BW), F32),
        ],
        scratch_shapes=[
            pltpu.VMEM((PAIR, tg + 8, 3 * BW), F32), pltpu.VMEM((PAIR, tg, BW), F32), pltpu.VMEM((PAIR, DH, BW), F32),
            pltpu.VMEM((PAIR, tg + 8, 2 * BW), F32), pltpu.VMEM((PAIR, tg, BW), F32), pltpu.VMEM((PAIR, DH, BW), F32),
        ],
        compiler_params=_cparams(("arbitrary", "arbitrary")),
        name="chunked",
    )(pg, s0, gc0, gcw, neg_a, dt_bias, gnw, pm, c0m, n0, m0, mc0, mcw, b_i, b_f)


def _transpose_rows(x):
    eye = (_iota((BW, BW), 0) == _iota((BW, BW), 1)).astype(BF16)
    return _sel_dot(eye, x, _NT)


def _full(shape):
    nd = len(shape)
    return pl.BlockSpec(shape, lambda i: (0,) * nd)


def _head_rows(ref, h):
    return ref[pl.ds(pl.multiple_of(h * DH, DH), DH), :]


def _gdn_step_body(p_ref, s_ref, c_ref, cw_ref, nega_ref, dtb_ref, nw_ref, out_ref, so_ref, co_ref,
                   kt_s, qet_s, wt_s, egt_s, vbt_s, qkt_s, ot_s):
    h = pl.program_id(0)

    @pl.when(h == 0)
    def _():
        x = p_ref[:, 0:3 * BW]
        xa = x * cw_ref[CONV_W - 1:CONV_W, :]
        for j in range(CONV_W - 1):
            xa = xa + c_ref[j] * cw_ref[j:j + 1, :]
        co_ref[0] = c_ref[1]
        co_ref[1] = c_ref[2]
        co_ref[2] = x
        qkv = _silu(xa)
        ones_h = _head_ones()
        q = qkv[:, 0:BW]
        k = qkv[:, BW:2 * BW]
        q = q * lax.rsqrt(_dot_sel(q * q, ones_h) + EPS) * (DH ** -0.5)
        k = k * lax.rsqrt(_dot_sel(k * k, ones_h) + EPS)
        v = qkv[:, 2 * BW:3 * BW]
        ga = p_ref[:, 4 * BW:4 * BW + 128]
        g = nega_ref[...] * _softplus(ga + dtb_ref[...])
        eg = jnp.exp(_dot_sel(g, _expand_sel(0)))
        be = _dot_sel(_sigmoid(ga), _expand_sel(NH))
        kt_s[...] = _transpose_rows(k)
        qet_s[...] = _transpose_rows(q * eg)
        wt_s[...] = _transpose_rows(k * be * eg)
        egt_s[...] = _transpose_rows(eg)
        vbt_s[...] = _transpose_rows(v * be)
        qkt_s[...] = _transpose_rows(_dot_sel(q * k, ones_h))

    r0 = pl.multiple_of(h * DH, DH)

    def reduce_k(kk, acc):
        aw, aq = acc
        s = s_ref[kk]
        return aw + s * wt_s[pl.ds(r0 + kk, 1), :], aq + s * qet_s[pl.ds(r0 + kk, 1), :]

    nbl = s_ref.shape[-1]
    zero = jnp.zeros((DH, nbl), F32)
    aw, aq = lax.fori_loop(0, DH, reduce_k, (zero, zero), unroll=8)
    v_new = _head_rows(vbt_s, h) - aw
    ot_s[pl.ds(r0, DH), :] = aq + qkt_s[pl.ds(r0, 1), :] * v_new

    def update_k(kk, carry):
        so_ref[kk] = s_ref[kk] * egt_s[pl.ds(r0 + kk, 1), :] + kt_s[pl.ds(r0 + kk, 1), :] * v_new
        return carry

    lax.fori_loop(0, DH, update_k, 0, unroll=8)

    @pl.when(h == NH - 1)
    def _():
        o = ot_s[...].T
        ms = _dot_sel(o * o, _head_ones()) * (1.0 / DH)
        out_ref[...] = (o * lax.rsqrt(ms + EPS)) * nw_ref[...] * _silu(p_ref[:, 3 * BW:4 * BW])


def _lanes_block(nb, l):
    return pl.BlockSpec((None, None, DH, DH, nb), lambda h: (l, h, 0, 0, 0))


def _gdn_step(p, s_all, s_acc, l, c0, conv_w, neg_a, dt_bias, norm_w):
    nb = p.shape[0]
    ins = (p, s_all, c0, conv_w, neg_a, dt_bias, norm_w)
    in_specs = [_full(a.shape) for a in ins]
    in_specs[1] = _lanes_block(nb, l)
    return pl.pallas_call(
        lambda *refs: _gdn_step_body(*refs[:7], *refs[8:]),
        grid=(NH,),
        in_specs=in_specs + [pl.BlockSpec(memory_space=pl.ANY)],
        out_specs=[_full((nb, BW)), _lanes_block(nb, l), _full((CONV_W - 1, nb, 3 * BW))],
        out_shape=[
            jax.ShapeDtypeStruct((nb, BW), F32),
            jax.ShapeDtypeStruct(s_all.shape, F32),
            jax.ShapeDtypeStruct((CONV_W - 1, nb, 3 * BW), F32),
        ],
        input_output_aliases={7: 1},
        scratch_shapes=[pltpu.VMEM((BW, nb), F32) for _ in range(7)],
        compiler_params=_cparams(("arbitrary",)),
        name="gdn_step",
    )(*ins, s_acc)


def _ml_step_body(p_ref, c_ref, n_ref, m_ref, cv_ref, cw_ref, bi_ref, bf_ref, out_ref, co_ref, no_ref, mo_ref, cvo_ref,
                  qwt_s, qt_s, wit_s, wkt_s, vt_s, st_s, flt_s, nt_s, ht_s, nto_s):
    h = pl.program_id(0)

    @pl.when(h == 0)
    def _():
        x = p_ref[:, 0:2 * BW]
        xa = x * cw_ref[CONV_W - 1:CONV_W, :]
        for j in range(CONV_W - 1):
            xa = xa + cv_ref[j] * cw_ref[j:j + 1, :]
        cvo_ref[0] = cv_ref[1]
        cvo_ref[1] = cv_ref[2]
        cvo_ref[2] = x
        qk = _silu(xa)
        q = qk[:, 0:BW]
        k = qk[:, BW:2 * BW] * (DH ** -0.5)
        ga = p_ref[:, 5 * BW:5 * BW + 128]
        ip = _dot_sel(ga + bi_ref[...], _expand_sel(0))
        lf = _dot_sel(-_softplus(-(ga + bf_ref[...])), _expand_sel(NH))
        mm = _dot_sel(m_ref[...], _expand_sel(0))
        inter = lf + mm
        m_t = jnp.maximum(inter, ip)
        w_int = jnp.exp(inter - m_t)
        wgt = jnp.exp(ip - m_t)
        mo_ref[...] = m_t
        qwt_s[...] = _transpose_rows(q * w_int)
        qt_s[...] = _transpose_rows(q)
        wit_s[...] = _transpose_rows(w_int)
        wkt_s[...] = _transpose_rows(wgt * k)
        vt_s[...] = _transpose_rows(p_ref[:, 2 * BW:3 * BW])
        st_s[...] = _transpose_rows(_dot_sel(q * k, _head_ones()) * wgt)
        flt_s[...] = _transpose_rows(jnp.exp(-m_t))
        nt_s[...] = _transpose_rows(n_ref[...])

    r0 = pl.multiple_of(h * DH, DH)
    v_h = _head_rows(vt_s, h)
    w_row = wit_s[pl.ds(r0, 1), :]

    def one_d(d, acc):
        c = c_ref[d]
        co_ref[d] = c * w_row + wkt_s[pl.ds(r0 + d, 1), :] * v_h
        return acc + c * qwt_s[pl.ds(r0 + d, 1), :]

    nbl = c_ref.shape[-1]
    num = lax.fori_loop(0, DH, one_d, jnp.zeros((DH, nbl), F32), unroll=8)
    s_row = st_s[pl.ds(r0, 1), :]
    n_h = _head_rows(nt_s, h)
    qn = jnp.sum(_head_rows(qt_s, h) * n_h, axis=0, keepdims=True)
    den = w_row * qn + s_row
    ht_s[pl.ds(r0, DH), :] = (num + s_row * v_h) / jnp.maximum(jnp.abs(den), flt_s[pl.ds(r0, 1), :])
    nto_s[pl.ds(r0, DH), :] = w_row * n_h + _head_rows(wkt_s, h)

    @pl.when(h == NH - 1)
    def _():
        out_ref[...] = ht_s[...].T * _sigmoid(p_ref[:, 3 * BW:4 * BW]) * _silu(p_ref[:, 4 * BW:5 * BW])
        no_ref[...] = nto_s[...].T


def _ml_step(p, c_all, c_acc, l, n0, m0, c0, conv_w, b_i, b_f):
    nb = p.shape[0]
    ins = (p, c_all, n0, m0, c0, conv_w, b_i, b_f)
    in_specs = [_full(a.shape) for a in ins]
    in_specs[1] = _lanes_block(nb, l)
    return pl.pallas_call(
        lambda *refs: _ml_step_body(*refs[:8], *refs[9:]),
        grid=(NH,),
        in_specs=in_specs + [pl.BlockSpec(memory_space=pl.ANY)],
        out_specs=[_full((nb, BW)), _lanes_block(nb, l), _full((nb, BW)), _full((nb, BW)),
                   _full((CONV_W - 1, nb, 2 * BW))],
        out_shape=[
            jax.ShapeDtypeStruct((nb, BW), F32),
            jax.ShapeDtypeStruct(c_all.shape, F32),
            jax.ShapeDtypeStruct((nb, BW), F32),
            jax.ShapeDtypeStruct((nb, BW), F32),
            jax.ShapeDtypeStruct((CONV_W - 1, nb, 2 * BW), F32),
        ],
        input_output_aliases={8: 1},
        scratch_shapes=[pltpu.VMEM((BW, nb), F32) for _ in range(10)],
        compiler_params=_cparams(("arbitrary",)),
        name="mlstm_step",
    )(*ins, c_acc)


def _pad_lanes(x, width):
    return jnp.pad(x, [(0, 0)] * (x.ndim - 1) + [(0, width - x.shape[-1])])


IN_SEGMENTS = (('lru_x', BW), ('lru_z', BW), ('gdn_q', BW), ('gdn_k', BW), ('gdn_v', BW), ('gdn_z', BW),
               ('gdn_a', NH), ('gdn_b', NH), ('s5_u', BW), ('s5_z', BW),
               ('ml_q', BW), ('ml_k', BW), ('ml_v', BW), ('ml_o', BW), ('ml_z', BW),
               ('ml_i', NH), ('ml_f', NH), ('merge', NH * D_MODEL))
IN_WIDTH = sum(wd for _, wd in IN_SEGMENTS)
PACK_ORDER = (('merge',), ('lru_x', 'lru_z'), ('s5_u', 's5_z'), ('gdn_q', 'gdn_k', 'gdn_v', 'gdn_z'), ('gdn_a', 'gdn_b'),
              ('ml_q', 'ml_k', 'ml_v', 'ml_o', 'ml_z'), ('ml_i', 'ml_f'))
PACK_ROWS = 256


def _pack_body(wt_ref, o_ref):
    rows = o_ref.shape[0]
    eye = (_iota((rows, rows), 0) == _iota((rows, rows), 1)).astype(BF16)
    src = {}
    off = 0
    for name, wd in IN_SEGMENTS:
        src[name] = (off, wd)
        off += wd
    o_ref[:, W_PACK + sum(W_MIX):] = jnp.zeros((rows, W_PACK - sum(W_MIX)), BF16)
    dst = 0
    for group in PACK_ORDER:
        a = src[group[0]][0]
        wd = sum(src[n][1] for n in group)
        if wd % 128:
            o_ref[:, dst:dst + 128] = jnp.zeros((rows, 128), BF16)
        seg = wt_ref[a:a + wd, :].astype(BF16)
        o_ref[:, dst:dst + wd] = lax.dot_general(eye, seg, _NT, preferred_element_type=F32).astype(BF16)
        dst += -(-wd // 128) * 128


def _pack_w_in(w_in):
    depth = w_in.shape[0]
    return pl.pallas_call(
        _pack_body,
        grid=(depth, D_MODEL // PACK_ROWS),
        in_specs=[pl.BlockSpec((None, IN_WIDTH, PACK_ROWS), lambda l, i: (l, 0, i))],
        out_specs=pl.BlockSpec((None, PACK_ROWS, 2 * W_PACK), lambda l, i: (l, i, 0)),
        out_shape=jax.ShapeDtypeStruct((depth, D_MODEL, 2 * W_PACK), BF16),
        compiler_params=_cparams(("arbitrary", "arbitrary")),
        name="pack_w_in",
    )(jnp.swapaxes(w_in, 1, 2))


def _prep_layer(l, prm):
    eye4 = jnp.eye(4, dtype=F32)
    blockdiag = lambda w: jnp.einsum('ncd,nm->ncmd', w, eye4).reshape(BW, BW)
    lru_wg = jnp.concatenate([blockdiag(prm['lru_w_a'][l]), blockdiag(prm['lru_w_x'][l])], axis=1).astype(BF16)
    lru_bg = jnp.concatenate([prm['lru_b_a'][l], prm['lru_b_x'][l]])[None, :]
    lru_sp = jax.nn.softplus(-prm['lru_lambda'][l])[None, :]

    lam_re, lam_im = prm['s5_lam_re'][l], prm['s5_lam_im'][l]
    dt = jnp.exp(prm['s5_log_dt'][l])[:, None]
    mag = jnp.exp(lam_re * dt)
    ang = lam_im * dt
    ar, ai = mag * jnp.cos(ang), mag * jnp.sin(ang)
    den = lam_re * lam_re + lam_im * lam_im
    cr = ((ar - 1.0) * lam_re + ai * lam_im) / den
    ci = (ai * lam_re - (ar - 1.0) * lam_im) / den
    b_re, b_im = prm['s5_b_re'][l], prm['s5_b_im'][l]
    bb_re = cr[..., None] * b_re - ci[..., None] * b_im
    bb_im = cr[..., None] * b_im + ci[..., None] * b_re
    eye_g = jnp.eye(S5_G, dtype=F32)
    in_dense = lambda bb: jnp.einsum('gnc,gh->gchn', bb, eye_g).reshape(BW, S5_W).astype(BF16)
    out_dense = lambda cc: jnp.einsum('gcn,gh->gnhc', cc, eye_g).reshape(S5_W, BW).astype(BF16)

    gate_row = lambda a, b: _pad_lanes(jnp.concatenate([a, b])[None, :], 128)
    zeros4 = jnp.zeros((NH,), F32)
    return dict(
        norm_w=prm['norm_w'][l][None, :],
        w_branch=prm['w_branch'][l].astype(BF16),
        w_out=prm['w_out'][l].astype(BF16),
        lru_cw=prm['lru_conv_w'][l], lru_wg=lru_wg, lru_bg=lru_bg, lru_sp=lru_sp,
        s5_ar=ar.reshape(1, S5_W), s5_ai=ai.reshape(1, S5_W),
        s5_bre=in_dense(bb_re), s5_bim=in_dense(bb_im),
        s5_cre=out_dense(prm['s5_c_re'][l]), s5_cim=out_dense(prm['s5_c_im'][l]),
        s5_d=prm['s5_d'][l][None, :], s5_wglu=prm['s5_w_glu'][l].astype(BF16),
        gdn_cw=prm['gdn_conv_w'][l],
        gdn_nega=gate_row(-jnp.exp(prm['gdn_A_log'][l]), zeros4),
        gdn_dtb=gate_row(prm['gdn_dt_bias'][l], zeros4),
        gdn_nw=jnp.tile(prm['gdn_norm_w'][l], NH)[None, :],
        ml_cw=prm['ml_conv_w'][l],
        ml_bi=gate_row(prm['ml_b_i'][l], zeros4),
        ml_bf=gate_row(zeros4, prm['ml_b_f'][l]),
    )


def _prompt_layer(x, pp, w_pack, l, final_w, final, tm, tt, tg):
    b, t, _ = x.shape
    p_lru, p_s5, p_gdn, p_ml = _in_proj(x, pp['norm_w'], w_pack, l, tm, True)
    z = lambda *s: jnp.zeros(s, F32)
    out_a, lru_h, lru_c = _lru(p_lru, z(b, BW), z(CONV_W - 1, b, BW),
                               pp['lru_cw'], pp['lru_wg'], pp['lru_bg'], pp['lru_sp'], tt)
    out_c, s5_re, s5_im = _s5(p_s5, z(b, S5_W), z(b, S5_W), pp['s5_ar'], pp['s5_ai'],
                              pp['s5_bre'], pp['s5_bim'], pp['s5_cre'], pp['s5_cim'], pp['s5_d'], pp['s5_wglu'], tt)
    out_b, gdn_s, gdn_c, out_d, ml_c, ml_n, ml_m, ml_cv = _chunked(
        p_gdn, z(b, NH, DH, DH), z(b, CONV_W - 1, 3 * BW), pp['gdn_cw'], pp['gdn_nega'], pp['gdn_dtb'], pp['gdn_nw'],
        p_ml, z(b, NH, DH, DH), z(b, 1, BW), z(b, 1, BW), z(b, CONV_W - 1, 2 * BW), pp['ml_cw'], pp['ml_bi'], pp['ml_bf'],
        b, tg)
    y = _merge(x, (out_a, out_b, out_c, out_d), pp['norm_w'], w_pack, l,
               pp['w_branch'], pp['w_out'], final_w, tm, final)
    st = dict(lru_h=lru_h, lru_conv=jnp.transpose(lru_c, (1, 0, 2)), gdn_S=gdn_s, gdn_conv=gdn_c,
              s5_re=s5_re.reshape(b, S5_G, S5_N), s5_im=s5_im.reshape(b, S5_G, S5_N),
              ml_C=ml_c, ml_n=ml_n.reshape(b, NH, DH), ml_m=ml_m.reshape(b, NH, DH)[:, :, 0], ml_conv=ml_cv)
    return y, st


BIG_STATES = ('gdn_S', 'ml_C')
TO_LANES = (0, 2, 3, 4, 1)
FROM_LANES = (0, 4, 1, 2, 3)


def _sample_layer(x, st, big_in, big_acc, l, pp, w_pack, final_w, final):
    nb = x.shape[1]
    p_lru, p_s5, p_gdn, p_ml = _in_proj(x, pp['norm_w'], w_pack, l, nb, False)
    tmaj = lambda c: jnp.transpose(c, (1, 0, 2))
    out_a, lru_h, lru_c = _lru(p_lru.reshape(1, nb, W_LRU), st['lru_h'], tmaj(st['lru_conv']),
                               pp['lru_cw'], pp['lru_wg'], pp['lru_bg'], pp['lru_sp'], 1)
    out_c, s5_re, s5_im = _s5(p_s5.reshape(1, nb, W_S5), st['s5_re'].reshape(nb, S5_W), st['s5_im'].reshape(nb, S5_W),
                              pp['s5_ar'], pp['s5_ai'], pp['s5_bre'], pp['s5_bim'], pp['s5_cre'], pp['s5_cim'],
                              pp['s5_d'], pp['s5_wglu'], 1)
    out_b, gdn_s, gdn_c = _gdn_step(p_gdn, big_in['gdn_S'], big_acc['gdn_S'], l, tmaj(st['gdn_conv']), pp['gdn_cw'],
                                    pp['gdn_nega'], pp['gdn_dtb'], pp['gdn_nw'])
    out_d, ml_c, ml_n, ml_m, ml_cv = _ml_step(p_ml, big_in['ml_C'], big_acc['ml_C'], l,
                                              st['ml_n'].reshape(nb, BW), _pad_lanes(st['ml_m'], 128),
                                              tmaj(st['ml_conv']), pp['ml_cw'], pp['ml_bi'], pp['ml_bf'])
    y = _merge(x, (out_a.reshape(nb, BW), out_b, out_c.reshape(nb, BW), out_d),
               pp['norm_w'], w_pack, l, pp['w_branch'], pp['w_out'], final_w, nb, final)
    new = dict(lru_h=lru_h, lru_conv=tmaj(lru_c), gdn_conv=tmaj(gdn_c),
               s5_re=s5_re.reshape(nb, S5_G, S5_N), s5_im=s5_im.reshape(nb, S5_G, S5_N),
               ml_n=ml_n.reshape(nb, NH, DH), ml_m=ml_m.reshape(nb, NH, DH)[:, :, 0], ml_conv=tmaj(ml_cv))
    return y, new, dict(gdn_S=gdn_s, ml_C=ml_c)


STATE_NAMES = ('lru_h', 'lru_conv', 'gdn_S', 'gdn_conv', 's5_re', 's5_im', 'ml_C', 'ml_n', 'ml_m', 'ml_conv')

TM_ROWS = 64
TT_SCAN = 128


def kernel(x_prompt, x_sample, state_lru_h, state_lru_conv, state_gdn_S, state_gdn_conv, state_s5_re, state_s5_im, state_ml_C, state_ml_n, state_ml_m, state_ml_conv, norm_w, w_in, lru_conv_w, lru_w_a, lru_b_a, lru_w_x, lru_b_x, lru_lambda, gdn_conv_w, gdn_A_log, gdn_dt_bias, gdn_norm_w, s5_lam_re, s5_lam_im, s5_log_dt, s5_b_re, s5_b_im, s5_c_re, s5_c_im, s5_d, s5_w_glu, ml_conv_w, ml_b_i, ml_b_f, w_branch, w_out, final_norm_w):
    prm = dict(norm_w=norm_w, w_in=w_in, lru_conv_w=lru_conv_w, lru_w_a=lru_w_a, lru_b_a=lru_b_a, lru_w_x=lru_w_x,
               lru_b_x=lru_b_x, lru_lambda=lru_lambda, gdn_conv_w=gdn_conv_w, gdn_A_log=gdn_A_log,
               gdn_dt_bias=gdn_dt_bias, gdn_norm_w=gdn_norm_w, s5_lam_re=s5_lam_re, s5_lam_im=s5_lam_im,
               s5_log_dt=s5_log_dt, s5_b_re=s5_b_re, s5_b_im=s5_b_im, s5_c_re=s5_c_re, s5_c_im=s5_c_im, s5_d=s5_d,
               s5_w_glu=s5_w_glu, ml_conv_w=ml_conv_w, ml_b_i=ml_b_i, ml_b_f=ml_b_f, w_branch=w_branch, w_out=w_out)
    depth = w_in.shape[0]
    t = x_prompt.shape[1]
    nb = x_sample.shape[0]
    final_w = final_norm_w[None, :]
    st_in = dict(lru_h=state_lru_h, lru_conv=state_lru_conv, gdn_S=state_gdn_S, gdn_conv=state_gdn_conv,
                 s5_re=state_s5_re, s5_im=state_s5_im, ml_C=state_ml_C, ml_n=state_ml_n, ml_m=state_ml_m,
                 ml_conv=state_ml_conv)
    tm = min(TM_ROWS, t)
    tt = min(TT_SCAN, t)
    tg = min(SE, t)

    xp = x_prompt
    xs = x_sample.reshape(1, nb, D_MODEL)
    small = [n for n in STATE_NAMES if n not in BIG_STATES]
    new_p = {n: [] for n in STATE_NAMES}
    new_s = {n: [] for n in small}
    big_in = {n: jnp.transpose(st_in[n], TO_LANES) for n in BIG_STATES}
    big_acc = {n: jnp.zeros(big_in[n].shape, F32) for n in BIG_STATES}
    w_pack = _pack_w_in(w_in)
    for l in range(depth):
        pp = _prep_layer(l, prm)
        final = l == depth - 1
        xp, sp = _prompt_layer(xp, pp, w_pack, l, final_w, final, tm, tt, tg)
        xs, ss, big_acc = _sample_layer(xs, {n: st_in[n][l] for n in small}, big_in, big_acc, l, pp, w_pack,
                                        final_w, final)
        for n in STATE_NAMES:
            new_p[n].append(sp[n])
        for n in small:
            new_s[n].append(ss[n])
    np_ = {n: jnp.stack(v) for n, v in new_p.items()}
    ns_ = {n: jnp.stack(v) for n, v in new_s.items()}
    ns_.update({n: jnp.transpose(big_acc[n], FROM_LANES) for n in BIG_STATES})
    y_sample = xs.reshape(nb, 1, D_MODEL)
    return (xp, y_sample,
            np_['lru_h'], ns_['lru_h'], np_['lru_conv'], ns_['lru_conv'],
            np_['gdn_S'], ns_['gdn_S'], np_['gdn_conv'], ns_['gdn_conv'],
            np_['s5_re'], ns_['s5_re'], np_['s5_im'], ns_['s5_im'],
            np_['ml_C'], ns_['ml_C'], np_['ml_n'], ns_['ml_n'], np_['ml_m'], ns_['ml_m'],
            np_['ml_conv'], ns_['ml_conv'])
```
